```python
import math
import jax, jax.numpy as jnp
from jax import lax
import numpy as np

D_MODEL = 1024
BATCH = 16
SEQ = 2048
DEPTH = 1

CHUNK = 64
Q_BLOCK = 128
MIX_WIDTH = D_MODEL
LRU_WIDTH = MIX_WIDTH // 2
LRU_BLOCKS = 8
LRU_BLOCK_DIM = LRU_WIDTH // LRU_BLOCKS
LRU_CONV = 4
LRU_C = 8.0
DIFF_WIDTH = MIX_WIDTH - LRU_WIDTH
DIFF_HEADS = 4
DIFF_VDIM = DIFF_WIDTH // DIFF_HEADS
DIFF_QK = DIFF_VDIM // 2
REL_BUCKETS = 32
REL_MAX_DIST = 128
FFN_MULT = 3
D_FF = FFN_MULT * D_MODEL
FFN_CONV = 3
EPS = 1e-6
SUBLN_EPS = 1e-5
NEG_INF = -1e30
IN_COLS = 2 * LRU_WIDTH + 2 * DIFF_WIDTH + DIFF_WIDTH

kernel_name = "hymba_rglru_diffattn_convffn"


def rms_norm(x, g, eps=EPS):
    x32 = x.astype(jnp.float32)
    y = x32 * lax.rsqrt(jnp.mean(x32 * x32, axis=-1, keepdims=True) + eps)
    return (y * g.astype(jnp.float32)).astype(x.dtype)


def causal_dwconv(x, w, b):
    k, c = w.shape
    y = lax.conv_general_dilated(
        x, w[:, None, :].astype(x.dtype), window_strides=(1,),
        padding=[(k - 1, 0)], dimension_numbers=("NWC", "WIO", "NWC"),
        feature_group_count=c)
    return y + b.astype(x.dtype)


def block_diag(x, w, b):
    bsz, s, _ = x.shape
    xb = x.reshape(bsz, s, LRU_BLOCKS, LRU_BLOCK_DIM)
    y = jnp.einsum("bshi,hij->bshj", xb, w) + b
    return y.reshape(bsz, s, LRU_WIDTH)


def rg_lru(x, w_a, b_a, w_x, b_x, lam):
    r = jax.nn.sigmoid(block_diag(x, w_a, b_a).astype(jnp.float32))
    i = jax.nn.sigmoid(block_diag(x, w_x, b_x).astype(jnp.float32))
    log_a = -LRU_C * r * jax.nn.softplus(-lam.astype(jnp.float32))
    a = jnp.exp(log_a)
    mult = jnp.sqrt(1.0 - jnp.exp(2.0 * log_a))
    u = mult * (i * x.astype(jnp.float32))

    def combine(left, right):
        a_l, b_l = left
        a_r, b_r = right
        return a_l * a_r, a_r * b_l + b_r

    _, h = lax.associative_scan(combine, (a, u), axis=1)
    return h.astype(x.dtype)


def rel_bucket(rel):
    half = REL_BUCKETS // 2
    max_exact = half // 2
    ret = (rel > 0).astype(jnp.int32) * half
    n = jnp.abs(rel)
    nf = jnp.maximum(n, 1).astype(jnp.float32)
    large = max_exact + (jnp.log(nf / max_exact) / math.log(REL_MAX_DIST / max_exact)
                         * (half - max_exact)).astype(jnp.int32)
    large = jnp.minimum(large, half - 1)
    return ret + jnp.where(n < max_exact, n, large)


def diff_attention(q, k, v, rel_bias, lam, subln_g, lambda_init):
    bsz, s = q.shape[0], q.shape[1]
    nb = s // Q_BLOCK
    scale = DIFF_QK ** -0.5
    k_pos = jnp.arange(s, dtype=jnp.int32)
    q_blocks = jnp.moveaxis(q.reshape(bsz, nb, Q_BLOCK, DIFF_HEADS, 2, DIFF_QK), 1, 0)

    def attend(args):
        qb, bi = args
        q_pos = bi * Q_BLOCK + jnp.arange(Q_BLOCK, dtype=jnp.int32)
        logits = jnp.einsum("bqhcd,bkhcd->bhcqk", qb, k).astype(jnp.float32) * scale
        bias = rel_bias.astype(jnp.float32)[rel_bucket(k_pos[None, :] - q_pos[:, None])]
        logits = logits + jnp.transpose(bias, (2, 0, 1))[None, :, None]
        mask = (k_pos // CHUNK)[None, :] <= (q_pos // CHUNK)[:, None]
        logits = jnp.where(mask[None, None, None], logits, NEG_INF)
        p = jax.nn.softmax(logits, axis=-1)
        w = p[:, :, 0] - lam * p[:, :, 1]
        return jnp.einsum("bhqk,bkhe->bqhe", w.astype(v.dtype), v)

    out = lax.map(attend, (q_blocks, jnp.arange(nb, dtype=jnp.int32)))
    out = jnp.moveaxis(out, 0, 1).reshape(bsz, s, DIFF_HEADS, DIFF_VDIM)
    out = rms_norm(out, subln_g, SUBLN_EPS) * (1.0 - lambda_init)
    return out.reshape(bsz, s, DIFF_WIDTH)


def setup_inputs(seed: int = 0) -> dict:
    key = jax.random.key(seed)
    ks = jax.random.split(key, 24)
    f32 = jnp.float32
    nrm = lambda k, shape, sc: jax.random.normal(k, shape, f32) * sc
    a_c = jax.random.uniform(ks[8], (DEPTH, LRU_WIDTH), f32, 0.9, 0.999)
    a = a_c ** (1.0 / LRU_C)
    lru_lambda = jnp.log(a) - jnp.log1p(-a)
    return {
        "x": nrm(ks[0], (BATCH, SEQ, D_MODEL), 1.0),
        "norm1_g": 1.0 + nrm(ks[1], (DEPTH, D_MODEL), 0.02),
        "w_in": nrm(ks[2], (DEPTH, D_MODEL, IN_COLS), D_MODEL ** -0.5),
        "lru_conv_w": nrm(ks[3], (DEPTH, LRU_CONV, LRU_WIDTH), LRU_CONV ** -0.5),
        "lru_conv_b": nrm(ks[4], (DEPTH, LRU_WIDTH), 0.01),
        "lru_wa": nrm(ks[5], (DEPTH, LRU_BLOCKS, LRU_BLOCK_DIM, LRU_BLOCK_DIM), LRU_BLOCK_DIM ** -0.5),
        "lru_ba": nrm(ks[6], (DEPTH, LRU_BLOCKS, LRU_BLOCK_DIM), 0.01),
        "lru_wx": nrm(ks[7], (DEPTH, LRU_BLOCKS, LRU_BLOCK_DIM, LRU_BLOCK_DIM), LRU_BLOCK_DIM ** -0.5),
        "lru_bx": nrm(ks[9], (DEPTH, LRU_BLOCKS, LRU_BLOCK_DIM), 0.01),
        "lru_lambda": lru_lambda,
        "diff_lq1": nrm(ks[10], (DEPTH, DIFF_QK), 0.1),
        "diff_lk1": nrm(ks[11], (DEPTH, DIFF_QK), 0.1),
        "diff_lq2": nrm(ks[12], (DEPTH, DIFF_QK), 0.1),
        "diff_lk2": nrm(ks[13], (DEPTH, DIFF_QK), 0.1),
        "diff_subln_g": 1.0 + nrm(ks[14], (DEPTH, DIFF_VDIM), 0.02),
        "rel_bias": nrm(ks[15], (REL_BUCKETS, DIFF_HEADS), 0.5),
        "w_out": nrm(ks[16], (DEPTH, MIX_WIDTH, D_MODEL), MIX_WIDTH ** -0.5),
        "norm2_g": 1.0 + nrm(ks[17], (DEPTH, D_MODEL), 0.02),
        "ffn_w_up": nrm(ks[18], (DEPTH, D_MODEL, 2 * D_FF), D_MODEL ** -0.5),
        "ffn_conv_w": nrm(ks[19], (DEPTH, FFN_CONV, D_FF), FFN_CONV ** -0.5),
        "ffn_conv_b": nrm(ks[20], (DEPTH, D_FF), 0.01),
        "ffn_w_down": nrm(ks[21], (DEPTH, D_FF, D_MODEL), D_FF ** -0.5),
        "final_norm_g": 1.0 + nrm(ks[22], (D_MODEL,), 0.02),
    }


def reference(x, norm1_g, w_in, lru_conv_w, lru_conv_b, lru_wa, lru_ba, lru_wx, lru_bx,
              lru_lambda, diff_lq1, diff_lk1, diff_lq2, diff_lk2, diff_subln_g, rel_bias,
              w_out, norm2_g, ffn_w_up, ffn_conv_w, ffn_conv_b, ffn_w_down, final_norm_g):
    bsz, s, _ = x.shape
    for l in range(DEPTH):
        lambda_init = 0.8 - 0.6 * math.exp(-0.3 * l)
        h = rms_norm(x, norm1_g[l])
        proj = h @ w_in[l]
        lru_x, lru_gate, q, k, v = jnp.split(
            proj, np.cumsum([LRU_WIDTH, LRU_WIDTH, DIFF_WIDTH, DIFF_WIDTH]).tolist(), axis=-1)
        lru_x = causal_dwconv(lru_x, lru_conv_w[l], lru_conv_b[l])
        lru_h = rg_lru(lru_x, lru_wa[l], lru_ba[l], lru_wx[l], lru_bx[l], lru_lambda[l])
        out_a = jax.nn.gelu(lru_gate, approximate=True) * lru_h
        lam = (jnp.exp(jnp.sum(diff_lq1[l] * diff_lk1[l]).astype(jnp.float32))
               - jnp.exp(jnp.sum(diff_lq2[l] * diff_lk2[l]).astype(jnp.float32))
               + lambda_init)
        q = q.reshape(bsz, s, DIFF_HEADS, 2, DIFF_QK)
        k = k.reshape(bsz, s, DIFF_HEADS, 2, DIFF_QK)
        v = v.reshape(bsz, s, DIFF_HEADS, DIFF_VDIM)
        out_b = diff_attention(q, k, v, rel_bias, lam, diff_subln_g[l], lambda_init)
        mixed = jnp.concatenate([out_a, out_b], axis=-1)
        x = x + mixed @ w_out[l]
        h = rms_norm(x, norm2_g[l])
        gate, val = jnp.split(h @ ffn_w_up[l], 2, axis=-1)
        gate = causal_dwconv(gate, ffn_conv_w[l], ffn_conv_b[l])
        x = x + (jax.nn.gelu(gate, approximate=True) * val) @ ffn_w_down[l]
    return rms_norm(x, final_norm_g)
```

```python
import functools
import math

import jax
import jax.numpy as jnp
import numpy as np
from jax import lax
from jax.experimental import pallas as pl
from jax.experimental.pallas import tpu as pltpu

CHUNK = 64
LRU_BLOCKS = 8
LRU_CONV = 4
LRU_C = 8.0
DIFF_HEADS = 4
REL_BUCKETS = 32
REL_MAX_DIST = 128
FFN_CONV = 3
EPS = 1e-6
SUBLN_EPS = 1e-5
NEG_INF = -1e30
LAMBDA_INIT = 0.8 - 0.6 * math.exp(-0.3 * 0)

SUBLANES = 8
VMEM_LIMIT = 56 * 1024 * 1024

BF16 = jnp.bfloat16
F32 = jnp.float32


def _rms(x, g, eps):
    return x * lax.rsqrt(jnp.mean(x * x, axis=-1, keepdims=True) + eps) * g


def _gelu_tanh(x):
    inner = math.sqrt(2.0 / math.pi) * (x + 0.044715 * (x * x * x))
    return 0.5 * x * (1.0 + jnp.tanh(inner))


def _const_spec(shape):
    nd = len(shape)
    return pl.BlockSpec(shape, lambda *_: (0,) * nd, pipeline_mode=pl.Buffered(1))


def _shift_rows(ext, s):
    return pltpu.roll(ext, s, 0)[SUBLANES:]


def _inproj_kernel(x_ref, g_ref, w_ref, lx_ref, lg_ref, q_ref, k_ref, v_ref):
    h = _rms(x_ref[...], g_ref[...], EPS).astype(BF16)
    width = lx_ref.shape[-1]
    for j, o_ref in enumerate((lx_ref, lg_ref, q_ref, k_ref, v_ref)):
        y = jnp.dot(h, w_ref[:, j * width:(j + 1) * width], preferred_element_type=F32)
        o_ref[...] = y.astype(o_ref.dtype)


def _inproj(x2, g, w_bf16, tm):
    n, d = x2.shape
    width = w_bf16.shape[1] // 5
    out_sds = [jax.ShapeDtypeStruct((n, width), F32)] + [jax.ShapeDtypeStruct((n, width), BF16)] * 4
    row_spec = pl.BlockSpec((tm, width), lambda i: (i, 0))
    return pl.pallas_call(
        _inproj_kernel,
        grid=(n // tm,),
        in_specs=[pl.BlockSpec((tm, d), lambda i: (i, 0)), _const_spec((1, d)), _const_spec(w_bf16.shape)],
        out_specs=[row_spec] * 5,
        out_shape=out_sds,
        compiler_params=pltpu.CompilerParams(dimension_semantics=("arbitrary",), vmem_limit_bytes=VMEM_LIMIT),
        name="inproj",
    )(x2, g, w_bf16)


def _lru_kernel(lx_ref, lg_ref, cw_ref, cb_ref, wg_ref, bg_ref, lam_ref, o_ref, tail_ref, h_ref):
    t = pl.program_id(1)
    tile, width = lx_ref.shape

    @pl.when(t == 0)
    def _():
        tail_ref[...] = jnp.zeros_like(tail_ref)
        h_ref[...] = jnp.zeros_like(h_ref)

    x = lx_ref[...]
    ext = jnp.concatenate([tail_ref[...], x], axis=0)
    xc = cb_ref[...] + cw_ref[LRU_CONV - 1:LRU_CONV, :] * x
    for s in range(1, LRU_CONV):
        xc = xc + cw_ref[LRU_CONV - 1 - s:LRU_CONV - s, :] * _shift_rows(ext, s)
    tail_ref[...] = x[tile - SUBLANES:]

    pre = jnp.dot(xc.astype(BF16), wg_ref[...], preferred_element_type=F32) + bg_ref[...]
    r = jax.nn.sigmoid(pre[:, :width])
    gate_i = jax.nn.sigmoid(pre[:, width:])
    z = -lam_ref[...]
    softplus = jnp.maximum(z, 0.0) + jnp.log(1.0 + jnp.exp(-jnp.abs(z)))
    a = jnp.exp((-LRU_C) * r * softplus)
    u = jnp.sqrt(1.0 - a * a) * (gate_i * xc)

    row = lax.broadcasted_iota(jnp.int32, (tile, width), 0)
    d = 1
    while d < tile:
        keep = row >= d
        u = jnp.where(keep, a * pltpu.roll(u, d, 0) + u, u)
        a = jnp.where(keep, a * pltpu.roll(a, d, 0), a)
        d *= 2
    h = a * h_ref[SUBLANES - 1:SUBLANES, :] + u
    h_ref[...] = h[tile - SUBLANES:]
    o_ref[...] = (_gelu_tanh(lg_ref[...].astype(F32)) * h).astype(o_ref.dtype)


def _lru(lx, lg, conv_w, conv_b, w_gates, b_gates, lam, bsz, seq, tile):
    n, width = lx.shape
    steps = seq // tile
    row_spec = pl.BlockSpec((tile, width), lambda b, t: (b * steps + t, 0))
    return pl.pallas_call(
        _lru_kernel,
        grid=(bsz, steps),
        in_specs=[row_spec, row_spec, _const_spec(conv_w.shape), _const_spec(conv_b.shape),
                  _const_spec(w_gates.shape), _const_spec(b_gates.shape), _const_spec(lam.shape)],
        out_specs=row_spec,
        out_shape=jax.ShapeDtypeStruct((n, width), BF16),
        scratch_shapes=[pltpu.VMEM((SUBLANES, width), F32), pltpu.VMEM((SUBLANES, width), F32)],
        compiler_params=pltpu.CompilerParams(dimension_semantics=("arbitrary", "arbitrary"),
                                             vmem_limit_bytes=VMEM_LIMIT),
        name="lru",
    )(lx, lg, conv_w, conv_b, w_gates, b_gates, lam)


def _rel_bucket_np(rel):
    half = REL_BUCKETS // 2
    max_exact = half // 2
    ret = (rel > 0).astype(np.int32) * half
    n = np.abs(rel)
    nf = np.maximum(n, 1).astype(np.float32)
    frac = np.log(nf / np.float32(max_exact)) / np.float32(math.log(REL_MAX_DIST / max_exact))
    large = max_exact + (frac * np.float32(half - max_exact)).astype(np.int32)
    large = np.minimum(large, half - 1)
    return ret + np.where(n < max_exact, n, large)


def _bucket_tiles(tq):
    ql = np.arange(tq)[:, None]
    kl = np.arange(2 * tq)[None, :]
    tiles = []
    for key_off in (0, -tq):
        kpos = kl + key_off
        bucket = _rel_bucket_np(kpos - ql).astype(np.int32)
        allowed = (kpos // CHUNK) <= (ql // CHUNK)
        tiles.append(np.where(allowed, bucket, -1).astype(np.int32))
    return np.stack(tiles)


def _far_bucket(tq, seq):
    far = _rel_bucket_np(-np.arange(tq + 1, seq))
    assert (far == far[0]).all()
    return int(far[0])


def _bias_kernel(rb_ref, bucket_ref, o_ref):
    h = pl.program_id(1)
    bucket = bucket_ref[...]
    acc = jnp.zeros(bucket.shape, F32)
    for b in range(REL_BUCKETS):
        acc = jnp.where(bucket == b, rb_ref[b, h], acc)
    o_ref[...] = jnp.where(bucket < 0, NEG_INF, acc)


def _bias_tiles(rel_bias, tq):
    buckets = jnp.asarray(_bucket_tiles(tq))
    nv, _, tk = buckets.shape
    return pl.pallas_call(
        _bias_kernel,
        grid=(nv, DIFF_HEADS),
        in_specs=[pl.BlockSpec(memory_space=pltpu.SMEM),
                  pl.BlockSpec((None, tq, tk), lambda v, h: (v, 0, 0))],
        out_specs=pl.BlockSpec((None, None, tq, tk), lambda v, h: (v, h, 0, 0)),
        out_shape=jax.ShapeDtypeStruct((nv, DIFF_HEADS, tq, tk), F32),
        name="relbias",
    )(rel_bias, buckets)


def _attn_kernel(far_bucket, rb_ref, q_ref, k_ref, v_ref, bias_ref, lq1_ref, lk1_ref, lq2_ref, lk2_ref,
                 sg_ref, o_ref, m_ref, l_ref, acc_ref):
    h = pl.program_id(1)
    i = pl.program_id(2)
    tq, hd = q_ref.shape
    qk = hd // 2

    q = q_ref[...]
    lane = lax.broadcasted_iota(jnp.int32, (tq, hd), 1)
    scale = qk ** -0.5
    zero = jnp.zeros_like(q)
    qs = (jnp.where(lane < qk, q, zero) * scale, jnp.where(lane >= qk, q, zero) * scale)

    m_ref[...] = jnp.full(m_ref.shape, -jnp.inf, F32)
    l_ref[...] = jnp.zeros_like(l_ref)
    acc_ref[...] = jnp.zeros_like(acc_ref)

    def absorb(kb, vb, bias):
        for c in range(2):
            s = lax.dot_general(qs[c], kb, (((1,), (1,)), ((), ())), preferred_element_type=F32) + bias
            m_old = m_ref[c]
            m_new = jnp.maximum(m_old, jnp.max(s, axis=1, keepdims=True))
            alpha = jnp.exp(m_old - m_new)
            p = jnp.exp(s - m_new)
            l_ref[c] = alpha * l_ref[c] + jnp.sum(p, axis=1, keepdims=True)
            acc_ref[c] = alpha * acc_ref[c] + jnp.dot(p.astype(BF16), vb, preferred_element_type=F32)
            m_ref[c] = m_new

    start = pl.multiple_of(jnp.maximum(i - 1, 0) * tq, tq)
    absorb(k_ref[pl.ds(start, 2 * tq), :], v_ref[pl.ds(start, 2 * tq), :], bias_ref[...])

    far_bias = rb_ref[far_bucket, h]

    def far_step(j, carry):
        s0 = pl.multiple_of(j * tq, tq)
        absorb(k_ref[pl.ds(s0, tq), :], v_ref[pl.ds(s0, tq), :], far_bias)
        return carry

    lax.fori_loop(0, jnp.maximum(i - 1, 0), far_step, 0)

    lam = (jnp.exp(jnp.sum(lq1_ref[...] * lk1_ref[...], axis=1, keepdims=True))
           - jnp.exp(jnp.sum(lq2_ref[...] * lk2_ref[...], axis=1, keepdims=True)) + LAMBDA_INIT)
    out = acc_ref[0] / l_ref[0] - lam * (acc_ref[1] / l_ref[1])
    out = _rms(out, sg_ref[...], SUBLN_EPS) * (1.0 - LAMBDA_INIT)
    o_ref[...] = out.astype(o_ref.dtype)


def _attn(q, k, v, bias, rel_bias, lq1, lk1, lq2, lk2, subln_g, tq):
    bsz, seq, width = q.shape
    hd = width // DIFF_HEADS
    far_bucket = _far_bucket(tq, seq)
    qo_spec = pl.BlockSpec((None, tq, hd), lambda b, h, i: (b, i, h))
    kv_spec = pl.BlockSpec((None, seq, hd), lambda b, h, i: (b, 0, h))
    bias_spec = pl.BlockSpec((None, None, tq, 2 * tq), lambda b, h, i: (jnp.minimum(i, 1), h, 0, 0))
    return pl.pallas_call(
        functools.partial(_attn_kernel, far_bucket),
        grid=(bsz, DIFF_HEADS, seq // tq),
        in_specs=[pl.BlockSpec(memory_space=pltpu.SMEM), qo_spec, kv_spec, kv_spec, bias_spec,
                  _const_spec(lq1.shape), _const_spec(lk1.shape), _const_spec(lq2.shape),
                  _const_spec(lk2.shape), _const_spec(subln_g.shape)],
        out_specs=qo_spec,
        out_shape=jax.ShapeDtypeStruct((bsz, seq, width), BF16),
        scratch_shapes=[pltpu.VMEM((2, tq, 1), F32), pltpu.VMEM((2, tq, 1), F32),
                        pltpu.VMEM((2, tq, hd), F32)],
        compiler_params=pltpu.CompilerParams(dimension_semantics=("arbitrary",) * 3,
                                             vmem_limit_bytes=VMEM_LIMIT),
        name="diffattn",
    )(rel_bias, q, k, v, bias, lq1, lk1, lq2, lk2, subln_g)


def _ffn_kernel(steps_per_seq, n_chunks, x_ref, a_ref, b_ref, wo_ref, g2_ref, wu_ref, cw_ref, cb_ref,
                wd_ref, gf_ref, o_ref, tail_ref):
    tm = x_ref.shape[0]
    half = a_ref.shape[1]
    d_ff = wd_ref.shape[0]
    fc = d_ff // n_chunks

    @pl.when(pl.program_id(0) % steps_per_seq == 0)
    def _():
        tail_ref[...] = jnp.zeros_like(tail_ref)

    x1 = (x_ref[...]
          + jnp.dot(a_ref[...], wo_ref[:half, :], preferred_element_type=F32)
          + jnp.dot(b_ref[...], wo_ref[half:, :], preferred_element_type=F32))
    h2 = _rms(x1, g2_ref[...], EPS).astype(BF16)
    ffn = None
    for c in range(n_chunks):
        cols = slice(c * fc, (c + 1) * fc)
        gate = jnp.dot(h2, wu_ref[:, cols], preferred_element_type=F32)
        val = jnp.dot(h2, wu_ref[:, d_ff + c * fc:d_ff + (c + 1) * fc], preferred_element_type=F32)
        ext = jnp.concatenate([tail_ref[:, cols], gate], axis=0)
        conv = cb_ref[:, cols] + cw_ref[FFN_CONV - 1:FFN_CONV, cols] * gate
        for s in range(1, FFN_CONV):
            conv = conv + cw_ref[FFN_CONV - 1 - s:FFN_CONV - s, cols] * _shift_rows(ext, s)
        tail_ref[:, cols] = gate[tm - SUBLANES:]
        act = (_gelu_tanh(conv) * val).astype(BF16)
        down = jnp.dot(act, wd_ref[cols, :], preferred_element_type=F32)
        ffn = down if ffn is None else ffn + down
    o_ref[...] = _rms(x1 + ffn, gf_ref[...], EPS)


def _ffn(x2, out_a, out_b, w_out, g2, w_up, conv_w, conv_b, w_down, gf, seq, tm, n_chunks):
    n, d = x2.shape
    half = out_a.shape[1]
    d_ff = w_down.shape[0]
    return pl.pallas_call(
        functools.partial(_ffn_kernel, seq // tm, n_chunks),
        grid=(n // tm,),
        in_specs=[pl.BlockSpec((tm, d), lambda i: (i, 0)),
                  pl.BlockSpec((tm, half), lambda i: (i, 0)),
                  pl.BlockSpec((tm, half), lambda i: (i, 0)),
                  _const_spec(w_out.shape), _const_spec(g2.shape), _const_spec(w_up.shape),
                  _const_spec(conv_w.shape), _const_spec(conv_b.shape), _const_spec(w_down.shape),
                  _const_spec(gf.shape)],
        out_specs=pl.BlockSpec((tm, d), lambda i: (i, 0)),
        out_shape=jax.ShapeDtypeStruct((n, d), F32),
        scratch_shapes=[pltpu.VMEM((SUBLANES, d_ff), F32)],
        compiler_params=pltpu.CompilerParams(dimension_semantics=("arbitrary",), vmem_limit_bytes=VMEM_LIMIT),
        name="outproj_ffn",
    )(x2, out_a, out_b, w_out, g2, w_up, conv_w, conv_b, w_down, gf)


def _block_diag(w):
    nb, bd, _ = w.shape
    eye = jnp.eye(nb, dtype=w.dtype)
    return (eye[:, None, :, None] * w[:, :, None, :]).reshape(nb * bd, nb * bd)


def kernel(x, norm1_g, w_in, lru_conv_w, lru_conv_b, lru_wa, lru_ba, lru_wx, lru_bx, lru_lambda, diff_lq1, diff_lk1, diff_lq2, diff_lk2, diff_subln_g, rel_bias, w_out, norm2_g, ffn_w_up, ffn_conv_w, ffn_conv_b, ffn_w_down, final_norm_g):
    bsz, seq, d = x.shape
    assert w_in.shape[0] == 1, "single-layer block"
    n = bsz * seq
    x2 = x.reshape(n, d)
    row = lambda p: p.reshape(1, -1)

    lx, lg, q, k, v = _inproj(x2, row(norm1_g[0]), w_in[0].astype(BF16), tm=512)

    lru_w = lru_wa.shape[1] * lru_wa.shape[2]
    w_gates = jnp.concatenate([_block_diag(lru_wa[0]), _block_diag(lru_wx[0])], axis=1).astype(BF16)
    b_gates = jnp.concatenate([lru_ba[0].reshape(1, lru_w), lru_bx[0].reshape(1, lru_w)], axis=1)
    out_a = _lru(lx, lg, lru_conv_w[0], row(lru_conv_b[0]), w_gates, b_gates, row(lru_lambda[0]),
                 bsz, seq, tile=256)

    tq = 256
    bias = _bias_tiles(rel_bias, tq)
    shape3 = (bsz, seq, q.shape[1])
    out_b = _attn(q.reshape(shape3), k.reshape(shape3), v.reshape(shape3), bias, rel_bias,
                  row(diff_lq1[0]), row(diff_lk1[0]), row(diff_lq2[0]), row(diff_lk2[0]),
                  row(diff_subln_g[0]), tq)

    out = _ffn(x2, out_a, out_b.reshape(n, -1), w_out[0].astype(BF16), row(norm2_g[0]),
               ffn_w_up[0].astype(BF16), ffn_conv_w[0], row(ffn_conv_b[0]), ffn_w_down[0].astype(BF16),
               row(final_norm_g), seq, tm=512, n_chunks=3)
    return out.reshape(bsz, seq, d)
```

```python
import functools
import math

import jax
import jax.numpy as jnp
import numpy as np
from jax import lax
from jax.experimental import pallas as pl
from jax.experimental.pallas import tpu as pltpu

CHUNK = 64
LRU_BLOCKS = 8
LRU_CONV = 4
LRU_C = 8.0
DIFF_HEADS = 4
REL_BUCKETS = 32
REL_MAX_DIST = 128
FFN_CONV = 3
EPS = 1e-6
SUBLN_EPS = 1e-5
NEG_INF = -1e30
LAMBDA_INIT = 0.8 - 0.6 * math.exp(-0.3 * 0)

SUBLANES = 8
VMEM_LIMIT = 56 * 1024 * 1024

BF16 = jnp.bfloat16
F32 = jnp.float32


def _rms(x, g, eps):
    return x * lax.rsqrt(jnp.mean(x * x, axis=-1, keepdims=True) + eps) * g


def _gelu_tanh(x):
    inner = math.sqrt(2.0 / math.pi) * (x + 0.044715 * (x * x * x))
    return 0.5 * x * (1.0 + jnp.tanh(inner))


def _const_spec(shape):
    nd = len(shape)
    return pl.BlockSpec(shape, lambda *_: (0,) * nd, pipeline_mode=pl.Buffered(1))


def _shift_rows(ext, s):
    return pltpu.roll(ext, s, 0)[SUBLANES:]


def _inproj_kernel(qk_scale, x_ref, g_ref, w_ref, wt_ref, lx_ref, lg_ref, k_ref, qt_ref, vt_ref):
    h = _rms(x_ref[...], g_ref[...], EPS).astype(BF16)
    width = lx_ref.shape[-1]
    for j, o_ref in enumerate((lx_ref, lg_ref, k_ref)):
        y = jnp.dot(h, w_ref[:, j * width:(j + 1) * width], preferred_element_type=F32)
        o_ref[...] = y.astype(o_ref.dtype)
    nt = (((1,), (1,)), ((), ()))
    qt = lax.dot_general(wt_ref[:width, :], h, nt, preferred_element_type=F32)
    qt_ref[...] = (qt * qk_scale).astype(qt_ref.dtype)
    vt = lax.dot_general(wt_ref[width:, :], h, nt, preferred_element_type=F32)
    vt_ref[...] = vt.astype(vt_ref.dtype)


def _inproj(x2, g, w_rows, w_cols_t, bsz, seq, tm, qk_scale):
    n, d = x2.shape
    width = w_rows.shape[1] // 3
    spb = seq // tm
    out_sds = [jax.ShapeDtypeStruct((n, width), F32), jax.ShapeDtypeStruct((n, width), BF16),
               jax.ShapeDtypeStruct((n, width), BF16),
               jax.ShapeDtypeStruct((bsz, width, seq), BF16),
               jax.ShapeDtypeStruct((bsz, width, seq), BF16)]
    row_spec = pl.BlockSpec((tm, width), lambda i: (i, 0))
    col_spec = pl.BlockSpec((None, width, tm), lambda i: (i // spb, 0, i % spb))
    return pl.pallas_call(
        functools.partial(_inproj_kernel, qk_scale),
        grid=(n // tm,),
        in_specs=[pl.BlockSpec((tm, d), lambda i: (i, 0)), _const_spec((1, d)), _const_spec(w_rows.shape),
                  _const_spec(w_cols_t.shape)],
        out_specs=[row_spec, row_spec, row_spec, col_spec, col_spec],
        out_shape=out_sds,
        compiler_params=pltpu.CompilerParams(dimension_semantics=("arbitrary",), vmem_limit_bytes=VMEM_LIMIT),
        name="inproj",
    )(x2, g, w_rows, w_cols_t)


def _lru_kernel(lx_ref, lg_ref, cw_ref, cb_ref, wg_ref, bg_ref, lam_ref, o_ref, tail_ref, h_ref):
    t = pl.program_id(1)
    tile, width = lx_ref.shape

    @pl.when(t == 0)
    def _():
        tail_ref[...] = jnp.zeros_like(tail_ref)
        h_ref[...] = jnp.zeros_like(h_ref)

    x = lx_ref[...]
    ext = jnp.concatenate([tail_ref[...], x], axis=0)
    xc = cb_ref[...] + cw_ref[LRU_CONV - 1:LRU_CONV, :] * x
    for s in range(1, LRU_CONV):
        xc = xc + cw_ref[LRU_CONV - 1 - s:LRU_CONV - s, :] * _shift_rows(ext, s)
    tail_ref[...] = x[tile - SUBLANES:]

    pre = jnp.dot(xc.astype(BF16), wg_ref[...], preferred_element_type=F32) + bg_ref[...]
    r = jax.nn.sigmoid(pre[:, :width])
    gate_i = jax.nn.sigmoid(pre[:, width:])
    z = -lam_ref[...]
    softplus = jnp.maximum(z, 0.0) + jnp.log(1.0 + jnp.exp(-jnp.abs(z)))
    a = jnp.exp((-LRU_C) * r * softplus)
    u = jnp.sqrt(1.0 - a * a) * (gate_i * xc)

    row = lax.broadcasted_iota(jnp.int32, (tile, width), 0)
    d = 1
    while d < tile:
        keep = row >= d
        u = jnp.where(keep, a * pltpu.roll(u, d, 0) + u, u)
        a = jnp.where(keep, a * pltpu.roll(a, d, 0), a)
        d *= 2
    h = a * h_ref[SUBLANES - 1:SUBLANES, :] + u
    h_ref[...] = h[tile - SUBLANES:]
    o_ref[...] = (_gelu_tanh(lg_ref[...].astype(F32)) * h).astype(o_ref.dtype)


def _lru(lx, lg, conv_w, conv_b, w_gates, b_gates, lam, bsz, seq, tile):
    n, width = lx.shape
    steps = seq // tile
    row_spec = pl.BlockSpec((tile, width), lambda b, t: (b * steps + t, 0))
    return pl.pallas_call(
        _lru_kernel,
        grid=(bsz, steps),
        in_specs=[row_spec, row_spec, _const_spec(conv_w.shape), _const_spec(conv_b.shape),
                  _const_spec(w_gates.shape), _const_spec(b_gates.shape), _const_spec(lam.shape)],
        out_specs=row_spec,
        out_shape=jax.ShapeDtypeStruct((n, width), BF16),
        scratch_shapes=[pltpu.VMEM((SUBLANES, width), F32), pltpu.VMEM((SUBLANES, width), F32)],
        compiler_params=pltpu.CompilerParams(dimension_semantics=("arbitrary", "arbitrary"),
                                             vmem_limit_bytes=VMEM_LIMIT),
        name="lru",
    )(lx, lg, conv_w, conv_b, w_gates, b_gates, lam)


def _rel_bucket_np(rel):
    half = REL_BUCKETS // 2
    max_exact = half // 2
    ret = (rel > 0).astype(np.int32) * half
    n = np.abs(rel)
    nf = np.maximum(n, 1).astype(np.float32)
    frac = np.log(nf / np.float32(max_exact)) / np.float32(math.log(REL_MAX_DIST / max_exact))
    large = max_exact + (frac * np.float32(half - max_exact)).astype(np.int32)
    large = np.minimum(large, half - 1)
    return ret + np.where(n < max_exact, n, large)


def _bucket_tile(tq):
    ql = np.arange(tq)[None, :]
    kpos = np.arange(2 * tq)[:, None] - tq
    bucket = _rel_bucket_np(kpos - ql).astype(np.int32)
    allowed = (kpos // CHUNK) <= (ql // CHUNK)
    return np.where(allowed, bucket, -1).astype(np.int32)


def _far_bucket(tq, seq):
    far = _rel_bucket_np(-np.arange(tq + 1, seq))
    assert (far == far[0]).all()
    return int(far[0])


def _bias_kernel(rb_ref, bucket_ref, o_ref):
    h = pl.program_id(0)
    bucket = bucket_ref[...]
    acc = jnp.zeros(bucket.shape, F32)
    for b in range(REL_BUCKETS):
        acc = jnp.where(bucket == b, rb_ref[b, h], acc)
    o_ref[...] = jnp.where(bucket < 0, NEG_INF, acc)


def _bias_tiles(rel_bias, tq):
    bucket = jnp.asarray(_bucket_tile(tq))
    tk2 = bucket.shape[0]
    return pl.pallas_call(
        _bias_kernel,
        grid=(DIFF_HEADS,),
        in_specs=[pl.BlockSpec(memory_space=pltpu.SMEM), pl.BlockSpec((tk2, tq), lambda h: (0, 0))],
        out_specs=pl.BlockSpec((None, tk2, tq), lambda h: (h, 0, 0)),
        out_shape=jax.ShapeDtypeStruct((DIFF_HEADS, tk2, tq), F32),
        name="relbias",
    )(rel_bias, bucket)


def _attn_kernel(far_bucket, tq, rb_ref, qt_ref, k_ref, vt_ref, bias_ref, lq1_ref, lk1_ref, lq2_ref, lk2_ref,
                 sg_ref, o_ref):
    h = pl.program_id(1)
    hd, seq = qt_ref.shape
    qk = hd // 2
    far_bias = rb_ref[far_bucket, h]
    lam = (jnp.exp(jnp.sum(lq1_ref[...] * lk1_ref[...], axis=1, keepdims=True))
           - jnp.exp(jnp.sum(lq2_ref[...] * lk2_ref[...], axis=1, keepdims=True)) + LAMBDA_INIT)
    sub = lax.broadcasted_iota(jnp.int32, (hd, tq), 0)

    for i in range(seq // tq):
        qt = qt_ref[:, i * tq:(i + 1) * tq]
        zero = jnp.zeros_like(qt)
        near_lo = max(i - 1, 0) * tq
        hi = (i + 1) * tq
        bias = bias_ref[...] if i > 0 else bias_ref[tq:, :]
        maps = []
        for c in range(2):
            qtc = jnp.where((sub < qk) if c == 0 else (sub >= qk), qt, zero)
            s_near = jnp.dot(k_ref[near_lo:hi, :], qtc, preferred_element_type=F32) + bias
            m = jnp.max(s_near, axis=0, keepdims=True)
            if near_lo > 0:
                s_far = jnp.dot(k_ref[:near_lo, :], qtc, preferred_element_type=F32)
                m = jnp.maximum(m, jnp.max(s_far, axis=0, keepdims=True) + far_bias)
            p_near = jnp.exp(s_near - m)
            l = jnp.sum(p_near, axis=0, keepdims=True)
            pv = jnp.dot(vt_ref[:, near_lo:hi], p_near.astype(BF16), preferred_element_type=F32)
            if near_lo > 0:
                p_far = jnp.exp(s_far - (m - far_bias))
                l = l + jnp.sum(p_far, axis=0, keepdims=True)
                pv = pv + jnp.dot(vt_ref[:, :near_lo], p_far.astype(BF16), preferred_element_type=F32)
            maps.append(pv * (1.0 / l))
        out = maps[0] - lam * maps[1]
        out = out * lax.rsqrt(jnp.mean(out * out, axis=0, keepdims=True) + SUBLN_EPS)
        o_ref[i * tq:(i + 1) * tq, :] = (out.T * sg_ref[...] * (1.0 - LAMBDA_INIT)).astype(o_ref.dtype)


def _attn(qt, k, vt, bias, rel_bias, lq1, lk1, lq2, lk2, subln_g, tq):
    bsz, seq, width = k.shape
    hd = width // DIFF_HEADS
    far_bucket = _far_bucket(tq, seq)
    col_spec = pl.BlockSpec((None, hd, seq), lambda b, h: (b, h, 0))
    row_spec = pl.BlockSpec((None, seq, hd), lambda b, h: (b, 0, h))
    return pl.pallas_call(
        functools.partial(_attn_kernel, far_bucket, tq),
        grid=(bsz, DIFF_HEADS),
        in_specs=[pl.BlockSpec(memory_space=pltpu.SMEM), col_spec, row_spec, col_spec,
                  pl.BlockSpec((None, 2 * tq, tq), lambda b, h: (h, 0, 0)),
                  _const_spec(lq1.shape), _const_spec(lk1.shape), _const_spec(lq2.shape),
                  _const_spec(lk2.shape), _const_spec(subln_g.shape)],
        out_specs=row_spec,
        out_shape=jax.ShapeDtypeStruct((bsz, seq, width), BF16),
        compiler_params=pltpu.CompilerParams(dimension_semantics=("arbitrary",) * 2,
                                             vmem_limit_bytes=VMEM_LIMIT),
        name="diffattn",
    )(rel_bias, qt, k, vt, bias, lq1, lk1, lq2, lk2, subln_g)


def _ffn_kernel(steps_per_seq, n_chunks, x_ref, a_ref, b_ref, wo_ref, g2_ref, wu_ref, cw_ref, cb_ref,
                wd_ref, gf_ref, o_ref, tail_ref):
    tm = x_ref.shape[0]
    half = a_ref.shape[1]
    d_ff = wd_ref.shape[0]
    fc = d_ff // n_chunks

    @pl.when(pl.program_id(0) % steps_per_seq == 0)
    def _():
        tail_ref[...] = jnp.zeros_like(tail_ref)

    x1 = (x_ref[...]
          + jnp.dot(a_ref[...], wo_ref[:half, :], preferred_element_type=F32)
          + jnp.dot(b_ref[...], wo_ref[half:, :], preferred_element_type=F32))
    h2 = _rms(x1, g2_ref[...], EPS).astype(BF16)
    ffn = None
    for c in range(n_chunks):
        cols = slice(c * fc, (c + 1) * fc)
        gate = jnp.dot(h2, wu_ref[:, cols], preferred_element_type=F32)
        val = jnp.dot(h2, wu_ref[:, d_ff + c * fc:d_ff + (c + 1) * fc], preferred_element_type=F32)
        ext = jnp.concatenate([tail_ref[:, cols], gate], axis=0)
        conv = cb_ref[:, cols] + cw_ref[FFN_CONV - 1:FFN_CONV, cols] * gate
        for s in range(1, FFN_CONV):
            conv = conv + cw_ref[FFN_CONV - 1 - s:FFN_CONV - s, cols] * _shift_rows(ext, s)
        tail_ref[:, cols] = gate[tm - SUBLANES:]
        act = (_gelu_tanh(conv) * val).astype(BF16)
        down = jnp.dot(act, wd_ref[cols, :], preferred_element_type=F32)
        ffn = down if ffn is None else ffn + down
    o_ref[...] = _rms(x1 + ffn, gf_ref[...], EPS)


def _ffn(x2, out_a, out_b, w_out, g2, w_up, conv_w, conv_b, w_down, gf, seq, tm, n_chunks):
    n, d = x2.shape
    half = out_a.shape[1]
    d_ff = w_down.shape[0]
    return pl.pallas_call(
        functools.partial(_ffn_kernel, seq // tm, n_chunks),
        grid=(n // tm,),
        in_specs=[pl.BlockSpec((tm, d), lambda i: (i, 0)),
                  pl.BlockSpec((tm, half), lambda i: (i, 0)),
                  pl.BlockSpec((tm, half), lambda i: (i, 0)),
                  _const_spec(w_out.shape), _const_spec(g2.shape), _const_spec(w_up.shape),
                  _const_spec(conv_w.shape), _const_spec(conv_b.shape), _const_spec(w_down.shape),
                  _const_spec(gf.shape)],
        out_specs=pl.BlockSpec((tm, d), lambda i: (i, 0)),
        out_shape=jax.ShapeDtypeStruct((n, d), F32),
        scratch_shapes=[pltpu.VMEM((SUBLANES, d_ff), F32)],
        compiler_params=pltpu.CompilerParams(dimension_semantics=("arbitrary",), vmem_limit_bytes=VMEM_LIMIT),
        name="outproj_ffn",
    )(x2, out_a, out_b, w_out, g2, w_up, conv_w, conv_b, w_down, gf)


def _block_diag(w):
    nb, bd, _ = w.shape
    eye = jnp.eye(nb, dtype=w.dtype)
    return (eye[:, None, :, None] * w[:, :, None, :]).reshape(nb * bd, nb * bd)


def kernel(x, norm1_g, w_in, lru_conv_w, lru_conv_b, lru_wa, lru_ba, lru_wx, lru_bx, lru_lambda, diff_lq1, diff_lk1, diff_lq2, diff_lk2, diff_subln_g, rel_bias, w_out, norm2_g, ffn_w_up, ffn_conv_w, ffn_conv_b, ffn_w_down, final_norm_g):
    bsz, seq, d = x.shape
    assert w_in.shape[0] == 1, "single-layer block"
    n = bsz * seq
    x2 = x.reshape(n, d)
    row = lambda p: p.reshape(1, -1)

    lru_w = lru_wa.shape[1] * lru_wa.shape[2]
    attn_w = (w_in.shape[2] - 2 * lru_w) // 3
    qk_dim = attn_w // DIFF_HEADS // 2
    tq = 256
    w_in_b = w_in[0].astype(BF16)
    q_cols = w_in_b[:, 2 * lru_w:2 * lru_w + attn_w]
    k_cols = w_in_b[:, 2 * lru_w + attn_w:2 * lru_w + 2 * attn_w]
    v_cols = w_in_b[:, 2 * lru_w + 2 * attn_w:]
    w_rows = jnp.concatenate([w_in_b[:, :2 * lru_w], k_cols], axis=1)
    w_cols_t = jnp.concatenate([q_cols, v_cols], axis=1).T
    lx, lg, k, qt, vt = _inproj(x2, row(norm1_g[0]), w_rows, w_cols_t, bsz, seq, tm=512,
                                qk_scale=qk_dim ** -0.5)

    w_gates = jnp.concatenate([_block_diag(lru_wa[0]), _block_diag(lru_wx[0])], axis=1).astype(BF16)
    b_gates = jnp.concatenate([lru_ba[0].reshape(1, lru_w), lru_bx[0].reshape(1, lru_w)], axis=1)
    out_a = _lru(lx, lg, lru_conv_w[0], row(lru_conv_b[0]), w_gates, b_gates, row(lru_lambda[0]),
                 bsz, seq, tile=256)

    bias = _bias_tiles(rel_bias, tq)
    out_b = _attn(qt, k.reshape(bsz, seq, attn_w), vt, bias, rel_bias,
                  row(diff_lq1[0]), row(diff_lk1[0]), row(diff_lq2[0]), row(diff_lk2[0]),
                  row(diff_subln_g[0]), tq)

    out = _ffn(x2, out_a, out_b.reshape(n, -1), w_out[0].astype(BF16), row(norm2_g[0]),
               ffn_w_up[0].astype(BF16), ffn_conv_w[0], row(ffn_conv_b[0]), ffn_w_down[0].astype(BF16),
               row(final_norm_g), seq, tm=512, n_chunks=3)
    return out.reshape(bsz, seq, d)
```

```python
import functools
import math

import jax
import jax.numpy as jnp
import numpy as np
from jax import lax
from jax.experimental import pallas as pl
from jax.experimental.pallas import tpu as pltpu

CHUNK = 64
LRU_BLOCKS = 8
LRU_CONV = 4
LRU_C = 8.0
LRU_GATE_GROUPS = 2
DIFF_HEADS = 4
REL_BUCKETS = 32
REL_MAX_DIST = 128
FFN_CONV = 3
EPS = 1e-6
SUBLN_EPS = 1e-5
NEG_INF = -1e30
LAMBDA_INIT = 0.8 - 0.6 * math.exp(-0.3 * 0)

SUBLANES = 8
VMEM_LIMIT = 56 * 1024 * 1024

BF16 = jnp.bfloat16
F32 = jnp.float32


def _rms(x, g, eps):
    return x * lax.rsqrt(jnp.mean(x * x, axis=-1, keepdims=True) + eps) * g


def _gelu_tanh(x):
    inner = math.sqrt(2.0 / math.pi) * (x + 0.044715 * (x * x * x))
    return 0.5 * x * (1.0 + jnp.tanh(inner))


def _const_spec(shape):
    nd = len(shape)
    return pl.BlockSpec(shape, lambda *_: (0,) * nd, pipeline_mode=pl.Buffered(1))


def _shift_rows(ext, s):
    return pltpu.roll(ext, s, 0)[SUBLANES:]


def _segment_perm(tile):
    seg_len = tile // SUBLANES
    r = np.arange(tile)
    perm = np.zeros((tile, tile), np.float32)
    perm[r, (r % SUBLANES) * seg_len + r // SUBLANES] = 1.0
    return perm


def _lru_conv(lx, lg, p_ref, cw_ref, cb_ref, xtail_ref):
    tile, width = lx.shape
    npos = tile // SUBLANES
    ntap = LRU_CONV - 1
    sub = lax.broadcasted_iota(jnp.int32, (SUBLANES, width), 0)
    slab = lambda v, p: v[p * SUBLANES:(p + 1) * SUBLANES]
    x = jnp.dot(p_ref[...], lx, preferred_element_type=F32)
    gate = jnp.dot(p_ref[...], lg, preferred_element_type=F32)
    wrapped = []
    for q in range(ntap):
        cur = slab(x, npos - ntap + q)
        prev = slab(xtail_ref[...], q)
        wrapped.append(pltpu.roll(jnp.where(sub == SUBLANES - 1, prev, cur), 1, 0))
    xtail_ref[...] = x[tile - ntap * SUBLANES:]
    xc = cb_ref[...] + cw_ref[LRU_CONV - 1:LRU_CONV, :] * x
    for s in range(1, LRU_CONV):
        shifted = jnp.concatenate(wrapped[ntap - s:] + [x[:tile - SUBLANES * s]], axis=0)
        xc = xc + cw_ref[LRU_CONV - 1 - s:LRU_CONV - s, :] * shifted
    return xc, gate


def _lru_gates(xc, wg_ref, bg_ref):
    xcb = xc.astype(BF16)
    ngrp, gw, _ = wg_ref.shape
    pres = [jnp.dot(xcb[:, j * gw:(j + 1) * gw], wg_ref[j], preferred_element_type=F32) for j in range(ngrp)]
    pre_r = jnp.concatenate([pj[:, :gw] for pj in pres], axis=1) + bg_ref[0:1, :]
    pre_i = jnp.concatenate([pj[:, gw:] for pj in pres], axis=1) + bg_ref[1:2, :]
    return pre_r, pre_i


def _lru_scan(xc, gate, pre_r, pre_i, lam_ref, hend_ref):
    tile, width = xc.shape
    npos = tile // SUBLANES
    sub = lax.broadcasted_iota(jnp.int32, (SUBLANES, width), 0)
    slab = lambda v, p: v[p * SUBLANES:(p + 1) * SUBLANES]
    r = jax.nn.sigmoid(pre_r)
    gate_i = jax.nn.sigmoid(pre_i)
    z = -lam_ref[...]
    softplus = jnp.maximum(z, 0.0) + jnp.log(1.0 + jnp.exp(-jnp.abs(z)))
    a = jnp.exp((-LRU_C) * r * softplus)
    y = 1.0 - a * a
    u = jnp.where(y > 0.0, y * lax.rsqrt(y), 0.0) * (gate_i * xc)

    hs = [slab(u, 0)]
    cum = [slab(a, 0)]
    for p in range(1, npos):
        ap = slab(a, p)
        hs.append(ap * hs[-1] + slab(u, p))
        cum.append(ap * cum[-1])
    inc = pltpu.roll(jnp.where(sub == SUBLANES - 1, hend_ref[...], hs[-1]), 1, 0)
    dec = jnp.where(sub == 0, 0.0, pltpu.roll(cum[-1], 1, 0))
    d = 1
    while d < SUBLANES:
        keep = sub >= d
        inc = jnp.where(keep, dec * pltpu.roll(inc, d, 0) + inc, inc)
        dec = jnp.where(keep, dec * pltpu.roll(dec, d, 0), dec)
        d *= 2
    h = jnp.concatenate([cum[p] * inc + hs[p] for p in range(npos)], axis=0)
    hend_ref[...] = h[tile - SUBLANES:]
    return (_gelu_tanh(gate) * h).astype(BF16)


def _inproj_kernel(qk_scale, steps_per_seq, x_ref, g_ref, w_ref, wt_ref, p_ref, pt_ref, cw_ref, cb_ref, wg_ref,
                   bg_ref, lam_ref, oa_ref, k_ref, qt_ref, vt_ref, lx_ref, lg_ref, xtail_ref, hend_ref):
    i = pl.program_id(0)

    @pl.when(i == 0)
    def _():
        lx_ref[...] = jnp.zeros_like(lx_ref)
        lg_ref[...] = jnp.zeros_like(lg_ref)

    @pl.when((i + steps_per_seq - 1) % steps_per_seq == 0)
    def _():
        xtail_ref[...] = jnp.zeros_like(xtail_ref)
        hend_ref[...] = jnp.zeros_like(hend_ref)

    lx_prev = lx_ref[...]
    lg_prev = lg_ref[...]
    tm, width = oa_ref.shape
    tile = p_ref.shape[0]
    assert tm == 2 * tile, "the interleaving below is written for two scan tiles per step"
    rows = (slice(0, tile), slice(tile, tm))
    nt = (((1,), (1,)), ((), ()))
    unperm = lambda out: jnp.dot(pt_ref[...], out, preferred_element_type=F32).astype(oa_ref.dtype)

    h = _rms(x_ref[...], g_ref[...], EPS).astype(BF16)
    xc0, gate0 = _lru_conv(lx_prev[rows[0]], lg_prev[rows[0]], p_ref, cw_ref, cb_ref, xtail_ref)
    lx_ref[...] = jnp.dot(h, w_ref[:, :width], preferred_element_type=F32).astype(BF16)
    pre0 = _lru_gates(xc0, wg_ref, bg_ref)
    xc1, gate1 = _lru_conv(lx_prev[rows[1]], lg_prev[rows[1]], p_ref, cw_ref, cb_ref, xtail_ref)
    lg_ref[...] = jnp.dot(h, w_ref[:, width:2 * width], preferred_element_type=F32).astype(BF16)
    pre1 = _lru_gates(xc1, wg_ref, bg_ref)
    out0 = _lru_scan(xc0, gate0, *pre0, lam_ref, hend_ref)
    k_ref[...] = jnp.dot(h, w_ref[:, 2 * width:], preferred_element_type=F32).astype(k_ref.dtype)
    oa_ref[rows[0], :] = unperm(out0)
    out1 = _lru_scan(xc1, gate1, *pre1, lam_ref, hend_ref)
    qt = lax.dot_general(wt_ref[:width, :], h, nt, preferred_element_type=F32)
    qt_ref[...] = (qt * qk_scale).astype(qt_ref.dtype)
    oa_ref[rows[1], :] = unperm(out1)
    vt = lax.dot_general(wt_ref[width:, :], h, nt, preferred_element_type=F32)
    vt_ref[...] = vt.astype(vt_ref.dtype)


def _inproj(x2, g, w_rows, w_cols_t, conv_w, conv_b, w_gates, b_gates, lam, bsz, seq, tm, lru_tile, qk_scale):
    n, d = x2.shape
    width = w_rows.shape[1] // 3
    spb = seq // tm
    last = n // tm - 1
    perm = _segment_perm(lru_tile)
    p_mat, pt_mat = jnp.asarray(perm, BF16), jnp.asarray(perm.T, BF16)
    consts = (g, w_rows, w_cols_t, p_mat, pt_mat, conv_w, conv_b, w_gates, b_gates, lam)
    proj = lambda i: jnp.minimum(i, last)
    col_spec = pl.BlockSpec((None, width, tm), lambda i: (proj(i) // spb, 0, proj(i) % spb))
    return pl.pallas_call(
        functools.partial(_inproj_kernel, qk_scale, spb),
        grid=(n // tm + 1,),
        in_specs=[pl.BlockSpec((tm, d), lambda i: (proj(i), 0))] + [_const_spec(c.shape) for c in consts],
        out_specs=[pl.BlockSpec((tm, width), lambda i: (jnp.maximum(i - 1, 0), 0)),
                   pl.BlockSpec((tm, width), lambda i: (proj(i), 0)), col_spec, col_spec],
        out_shape=[jax.ShapeDtypeStruct((n, width), BF16), jax.ShapeDtypeStruct((n, width), BF16),
                   jax.ShapeDtypeStruct((bsz, width, seq), BF16), jax.ShapeDtypeStruct((bsz, width, seq), BF16)],
        scratch_shapes=[pltpu.VMEM((tm, width), BF16), pltpu.VMEM((tm, width), BF16),
                        pltpu.VMEM(((LRU_CONV - 1) * SUBLANES, width), F32), pltpu.VMEM((SUBLANES, width), F32)],
        compiler_params=pltpu.CompilerParams(dimension_semantics=("arbitrary",), vmem_limit_bytes=VMEM_LIMIT),
        name="inproj_lru",
    )(x2, *consts)


def _rel_bucket_np(rel):
    half = REL_BUCKETS // 2
    max_exact = half // 2
    ret = (rel > 0).astype(np.int32) * half
    n = np.abs(rel)
    nf = np.maximum(n, 1).astype(np.float32)
    frac = np.log(nf / np.float32(max_exact)) / np.float32(math.log(REL_MAX_DIST / max_exact))
    large = max_exact + (frac * np.float32(half - max_exact)).astype(np.int32)
    large = np.minimum(large, half - 1)
    return ret + np.where(n < max_exact, n, large)


def _bucket_tile(tq):
    ql = np.arange(tq)[None, :]
    kpos = np.arange(2 * tq)[:, None] - tq
    bucket = _rel_bucket_np(kpos - ql).astype(np.int32)
    allowed = (kpos // CHUNK) <= (ql // CHUNK)
    return np.where(allowed, bucket, -1).astype(np.int32)


def _far_bucket(tq, seq):
    far = _rel_bucket_np(-np.arange(tq + 1, seq))
    assert (far == far[0]).all()
    return int(far[0])


def _bias_kernel(rb_ref, bucket_ref, o_ref):
    h = pl.program_id(0)
    bucket = bucket_ref[...]
    acc = jnp.zeros(bucket.shape, F32)
    for b in range(REL_BUCKETS):
        acc = jnp.where(bucket == b, rb_ref[b, h], acc)
    o_ref[...] = jnp.where(bucket < 0, NEG_INF, acc)


def _bias_tiles(rel_bias, tq):
    bucket = jnp.asarray(_bucket_tile(tq))
    tk2 = bucket.shape[0]
    return pl.pallas_call(
        _bias_kernel,
        grid=(DIFF_HEADS,),
        in_specs=[pl.BlockSpec(memory_space=pltpu.SMEM), pl.BlockSpec((tk2, tq), lambda h: (0, 0))],
        out_specs=pl.BlockSpec((None, tk2, tq), lambda h: (h, 0, 0)),
        out_shape=jax.ShapeDtypeStruct((DIFF_HEADS, tk2, tq), F32),
        name="relbias",
    )(rel_bias, bucket)


def _attn_kernel(far_bucket, tq, rb_ref, qt_ref, k_ref, vt_ref, bias_ref, lq1_ref, lk1_ref, lq2_ref, lk2_ref,
                 sg_ref, o_ref):
    h = pl.program_id(1)
    hd, seq = qt_ref.shape
    qk = hd // 2
    far_bias = rb_ref[far_bucket, h]
    lam = (jnp.exp(jnp.sum(lq1_ref[...] * lk1_ref[...], axis=1, keepdims=True))
           - jnp.exp(jnp.sum(lq2_ref[...] * lk2_ref[...], axis=1, keepdims=True)) + LAMBDA_INIT)
    sub = lax.broadcasted_iota(jnp.int32, (hd, tq), 0)

    for i in range(seq // tq):
        qt = qt_ref[:, i * tq:(i + 1) * tq]
        zero = jnp.zeros_like(qt)
        near_lo = max(i - 1, 0) * tq
        hi = (i + 1) * tq
        bias = bias_ref[...] if i > 0 else bias_ref[tq:, :]
        maps = []
        for c in range(2):
            qtc = jnp.where((sub < qk) if c == 0 else (sub >= qk), qt, zero)
            s_near = jnp.dot(k_ref[near_lo:hi, :], qtc, preferred_element_type=F32) + bias
            m = jnp.max(s_near, axis=0, keepdims=True)
            if near_lo > 0:
                s_far = jnp.dot(k_ref[:near_lo, :], qtc, preferred_element_type=F32)
                m = jnp.maximum(m, jnp.max(s_far, axis=0, keepdims=True) + far_bias)
            p_near = jnp.exp(s_near - m)
            l = jnp.sum(p_near, axis=0, keepdims=True)
            pv = jnp.dot(vt_ref[:, near_lo:hi], p_near.astype(BF16), preferred_element_type=F32)
            if near_lo > 0:
                p_far = jnp.exp(s_far - (m - far_bias))
                l = l + jnp.sum(p_far, axis=0, keepdims=True)
                pv = pv + jnp.dot(vt_ref[:, :near_lo], p_far.astype(BF16), preferred_element_type=F32)
            maps.append(pv * (1.0 / l))
        out = maps[0] - lam * maps[1]
        out = out * lax.rsqrt(jnp.mean(out * out, axis=0, keepdims=True) + SUBLN_EPS)
        o_ref[i * tq:(i + 1) * tq, :] = (out.T * sg_ref[...] * (1.0 - LAMBDA_INIT)).astype(o_ref.dtype)


def _attn(qt, k, vt, bias, rel_bias, lq1, lk1, lq2, lk2, subln_g, tq):
    bsz, seq, width = k.shape
    hd = width // DIFF_HEADS
    far_bucket = _far_bucket(tq, seq)
    col_spec = pl.BlockSpec((None, hd, seq), lambda b, h: (b, h, 0))
    row_spec = pl.BlockSpec((None, seq, hd), lambda b, h: (b, 0, h))
    return pl.pallas_call(
        functools.partial(_attn_kernel, far_bucket, tq),
        grid=(bsz, DIFF_HEADS),
        in_specs=[pl.BlockSpec(memory_space=pltpu.SMEM), col_spec, row_spec, col_spec,
                  pl.BlockSpec((None, 2 * tq, tq), lambda b, h: (h, 0, 0)),
                  _const_spec(lq1.shape), _const_spec(lk1.shape), _const_spec(lq2.shape),
                  _const_spec(lk2.shape), _const_spec(subln_g.shape)],
        out_specs=row_spec,
        out_shape=jax.ShapeDtypeStruct((bsz, seq, width), BF16),
        compiler_params=pltpu.CompilerParams(dimension_semantics=("arbitrary",) * 2,
                                             vmem_limit_bytes=VMEM_LIMIT),
        name="diffattn",
    )(rel_bias, qt, k, vt, bias, lq1, lk1, lq2, lk2, subln_g)


def _ffn_kernel(steps_per_seq, n_chunks, x_ref, a_ref, b_ref, wo_ref, g2_ref, wu_ref, cw_ref, cb_ref,
                wd_ref, gf_ref, o_ref, tail_ref):
    tm = x_ref.shape[0]
    half = a_ref.shape[1]
    d_ff = wd_ref.shape[0]
    fc = d_ff // n_chunks

    @pl.when(pl.program_id(0) % steps_per_seq == 0)
    def _():
        tail_ref[...] = jnp.zeros_like(tail_ref)

    x1 = (x_ref[...]
          + jnp.dot(a_ref[...], wo_ref[:half, :], preferred_element_type=F32)
          + jnp.dot(b_ref[...], wo_ref[half:, :], preferred_element_type=F32))
    h2 = _rms(x1, g2_ref[...], EPS).astype(BF16)
    ffn = None
    for c in range(n_chunks):
        cols = slice(c * fc, (c + 1) * fc)
        gate = jnp.dot(h2, wu_ref[:, cols], preferred_element_type=F32)
        val = jnp.dot(h2, wu_ref[:, d_ff + c * fc:d_ff + (c + 1) * fc], preferred_element_type=F32)
        ext = jnp.concatenate([tail_ref[:, cols], gate], axis=0)
        conv = cb_ref[:, cols] + cw_ref[FFN_CONV - 1:FFN_CONV, cols] * gate
        for s in range(1, FFN_CONV):
            conv = conv + cw_ref[FFN_CONV - 1 - s:FFN_CONV - s, cols] * _shift_rows(ext, s)
        tail_ref[:, cols] = gate[tm - SUBLANES:]
        act = (_gelu_tanh(conv) * val).astype(BF16)
        down = jnp.dot(act, wd_ref[cols, :], preferred_element_type=F32)
        ffn = down if ffn is None else ffn + down
    o_ref[...] = _rms(x1 + ffn, gf_ref[...], EPS)


def _ffn(x2, out_a, out_b, w_out, g2, w_up, conv_w, conv_b, w_down, gf, seq, tm, n_chunks):
    n, d = x2.shape
    half = out_a.shape[1]
    d_ff = w_down.shape[0]
    return pl.pallas_call(
        functools.partial(_ffn_kernel, seq // tm, n_chunks),
        grid=(n // tm,),
        in_specs=[pl.BlockSpec((tm, d), lambda i: (i, 0)),
                  pl.BlockSpec((tm, half), lambda i: (i, 0)),
                  pl.BlockSpec((tm, half), lambda i: (i, 0)),
                  _const_spec(w_out.shape), _const_spec(g2.shape), _const_spec(w_up.shape),
                  _const_spec(conv_w.shape), _const_spec(conv_b.shape), _const_spec(w_down.shape),
                  _const_spec(gf.shape)],
        out_specs=pl.BlockSpec((tm, d), lambda i: (i, 0)),
        out_shape=jax.ShapeDtypeStruct((n, d), F32),
        scratch_shapes=[pltpu.VMEM((SUBLANES, d_ff), F32)],
        compiler_params=pltpu.CompilerParams(dimension_semantics=("arbitrary",), vmem_limit_bytes=VMEM_LIMIT),
        name="outproj_ffn",
    )(x2, out_a, out_b, w_out, g2, w_up, conv_w, conv_b, w_down, gf)


def _block_diag(w):
    nb, bd, _ = w.shape
    eye = jnp.eye(nb, dtype=w.dtype)
    return (eye[:, None, :, None] * w[:, :, None, :]).reshape(nb * bd, nb * bd)


def kernel(x, norm1_g, w_in, lru_conv_w, lru_conv_b, lru_wa, lru_ba, lru_wx, lru_bx, lru_lambda, diff_lq1, diff_lk1, diff_lq2, diff_lk2, diff_subln_g, rel_bias, w_out, norm2_g, ffn_w_up, ffn_conv_w, ffn_conv_b, ffn_w_down, final_norm_g):
    bsz, seq, d = x.shape
    assert w_in.shape[0] == 1, "single-layer block"
    n = bsz * seq
    x2 = x.reshape(n, d)
    row = lambda p: p.reshape(1, -1)

    lru_w = lru_wa.shape[1] * lru_wa.shape[2]
    attn_w = (w_in.shape[2] - 2 * lru_w) // 3
    qk_dim = attn_w // DIFF_HEADS // 2
    tq = 256
    w_in_b = w_in[0].astype(BF16)
    q_cols = w_in_b[:, 2 * lru_w:2 * lru_w + attn_w]
    k_cols = w_in_b[:, 2 * lru_w + attn_w:2 * lru_w + 2 * attn_w]
    v_cols = w_in_b[:, 2 * lru_w + 2 * attn_w:]
    w_rows = jnp.concatenate([w_in_b[:, :2 * lru_w], k_cols], axis=1)
    w_cols_t = jnp.concatenate([q_cols, v_cols], axis=1).T
    grp = LRU_BLOCKS // LRU_GATE_GROUPS
    w_gates = jnp.stack([
        jnp.concatenate([_block_diag(lru_wa[0, j * grp:(j + 1) * grp]),
                         _block_diag(lru_wx[0, j * grp:(j + 1) * grp])], axis=1)
        for j in range(LRU_GATE_GROUPS)]).astype(BF16)
    b_gates = jnp.stack([lru_ba[0].reshape(lru_w), lru_bx[0].reshape(lru_w)])
    out_a, k, qt, vt = _inproj(x2, row(norm1_g[0]), w_rows, w_cols_t, lru_conv_w[0], row(lru_conv_b[0]),
                               w_gates, b_gates, row(lru_lambda[0]), bsz, seq, tm=512, lru_tile=256,
                               qk_scale=qk_dim ** -0.5)

    bias = _bias_tiles(rel_bias, tq)
    out_b = _attn(qt, k.reshape(bsz, seq, attn_w), vt, bias, rel_bias,
                  row(diff_lq1[0]), row(diff_lk1[0]), row(diff_lq2[0]), row(diff_lk2[0]),
                  row(diff_subln_g[0]), tq)

    out = _ffn(x2, out_a, out_b.reshape(n, -1), w_out[0].astype(BF16), row(norm2_g[0]),
               ffn_w_up[0].astype(BF16), ffn_conv_w[0], row(ffn_conv_b[0]), ffn_w_down[0].astype(BF16),
               row(final_norm_g), seq, tm=512, n_chunks=3)
    return out.reshape(bsz, seq, d)
```

```python
import functools
import math

import jax
import jax.numpy as jnp
import numpy as np
from jax import lax
from jax.experimental import pallas as pl
from jax.experimental.pallas import tpu as pltpu

CHUNK = 64
LRU_BLOCKS = 8
LRU_CONV = 4
LRU_C = 8.0
LRU_GATE_GROUPS = 2
DIFF_HEADS = 4
REL_BUCKETS = 32
REL_MAX_DIST = 128
FFN_CONV = 3
EPS = 1e-6
SUBLN_EPS = 1e-5
NEG_INF = -1e30
LAMBDA_INIT = 0.8 - 0.6 * math.exp(-0.3 * 0)

SUBLANES = 8
VMEM_LIMIT = 56 * 1024 * 1024

BF16 = jnp.bfloat16
F32 = jnp.float32


def _rms(x, g, eps):
    return x * lax.rsqrt(jnp.mean(x * x, axis=-1, keepdims=True) + eps) * g


def _gelu_tanh(x):
    inner = math.sqrt(2.0 / math.pi) * (x + 0.044715 * (x * x * x))
    return 0.5 * x * (1.0 + jnp.tanh(inner))


def _const_spec(shape):
    nd = len(shape)
    return pl.BlockSpec(shape, lambda *_: (0,) * nd, pipeline_mode=pl.Buffered(1))


def _shift_rows(ext, s):
    return pltpu.roll(ext, s, 0)[SUBLANES:]


def _segment_perm(tile):
    seg_len = tile // SUBLANES
    r = np.arange(tile)
    perm = np.zeros((tile, tile), np.float32)
    perm[r, (r % SUBLANES) * seg_len + r // SUBLANES] = 1.0
    return perm


def _lru_conv(lx, lg, p_ref, cw_ref, cb_ref, xtail_ref):
    tile, width = lx.shape
    npos = tile // SUBLANES
    ntap = LRU_CONV - 1
    sub = lax.broadcasted_iota(jnp.int32, (SUBLANES, width), 0)
    slab = lambda v, p: v[p * SUBLANES:(p + 1) * SUBLANES]
    x = jnp.dot(p_ref[...], lx, preferred_element_type=F32)
    gate = jnp.dot(p_ref[...], lg, preferred_element_type=F32)
    wrapped = []
    for q in range(ntap):
        cur = slab(x, npos - ntap + q)
        prev = slab(xtail_ref[...], q)
        wrapped.append(pltpu.roll(jnp.where(sub == SUBLANES - 1, prev, cur), 1, 0))
    xtail_ref[...] = x[tile - ntap * SUBLANES:]
    xc = cb_ref[...] + cw_ref[LRU_CONV - 1:LRU_CONV, :] * x
    for s in range(1, LRU_CONV):
        shifted = jnp.concatenate(wrapped[ntap - s:] + [x[:tile - SUBLANES * s]], axis=0)
        xc = xc + cw_ref[LRU_CONV - 1 - s:LRU_CONV - s, :] * shifted
    return xc, gate


def _lru_gates(xc, wg_ref, bg_ref):
    xcb = xc.astype(BF16)
    ngrp, gw, _ = wg_ref.shape
    pres = [jnp.dot(xcb[:, j * gw:(j + 1) * gw], wg_ref[j], preferred_element_type=F32) for j in range(ngrp)]
    pre_r = jnp.concatenate([pj[:, :gw] for pj in pres], axis=1) + bg_ref[0:1, :]
    pre_i = jnp.concatenate([pj[:, gw:] for pj in pres], axis=1) + bg_ref[1:2, :]
    return pre_r, pre_i


def _lru_scan(xc, gate, pre_r, pre_i, lam_ref, hend_ref):
    tile, width = xc.shape
    npos = tile // SUBLANES
    sub = lax.broadcasted_iota(jnp.int32, (SUBLANES, width), 0)
    slab = lambda v, p: v[p * SUBLANES:(p + 1) * SUBLANES]
    r = jax.nn.sigmoid(pre_r)
    gate_i = jax.nn.sigmoid(pre_i)
    z = -lam_ref[...]
    softplus = jnp.maximum(z, 0.0) + jnp.log(1.0 + jnp.exp(-jnp.abs(z)))
    a = jnp.exp((-LRU_C) * r * softplus)
    y = 1.0 - a * a
    u = jnp.where(y > 0.0, y * lax.rsqrt(y), 0.0) * (gate_i * xc)

    hs = [slab(u, 0)]
    cum = [slab(a, 0)]
    for p in range(1, npos):
        ap = slab(a, p)
        hs.append(ap * hs[-1] + slab(u, p))
        cum.append(ap * cum[-1])
    inc = pltpu.roll(jnp.where(sub == SUBLANES - 1, hend_ref[...], hs[-1]), 1, 0)
    dec = jnp.where(sub == 0, 0.0, pltpu.roll(cum[-1], 1, 0))
    d = 1
    while d < SUBLANES:
        keep = sub >= d
        inc = jnp.where(keep, dec * pltpu.roll(inc, d, 0) + inc, inc)
        dec = jnp.where(keep, dec * pltpu.roll(dec, d, 0), dec)
        d *= 2
    h = jnp.concatenate([cum[p] * inc + hs[p] for p in range(npos)], axis=0)
    hend_ref[...] = h[tile - SUBLANES:]
    return (_gelu_tanh(gate) * h).astype(BF16)


def _inproj_kernel(qk_scale, steps_per_seq, x_ref, g_ref, w_ref, wt_ref, p_ref, pt_ref, cw_ref, cb_ref, wg_ref,
                   bg_ref, lam_ref, oa_ref, k_ref, qt_ref, vt_ref, lx_ref, lg_ref, xtail_ref, hend_ref):
    i = pl.program_id(0)

    @pl.when(i == 0)
    def _():
        lx_ref[...] = jnp.zeros_like(lx_ref)
        lg_ref[...] = jnp.zeros_like(lg_ref)

    @pl.when((i + steps_per_seq - 1) % steps_per_seq == 0)
    def _():
        xtail_ref[...] = jnp.zeros_like(xtail_ref)
        hend_ref[...] = jnp.zeros_like(hend_ref)

    lx_prev = lx_ref[...]
    lg_prev = lg_ref[...]
    tm, width = oa_ref.shape
    tile = p_ref.shape[0]
    assert tm == 2 * tile, "the interleaving below is written for two scan tiles per step"
    rows = (slice(0, tile), slice(tile, tm))
    nt = (((1,), (1,)), ((), ()))
    unperm = lambda out: jnp.dot(pt_ref[...], out, preferred_element_type=F32).astype(oa_ref.dtype)

    h = _rms(x_ref[...], g_ref[...], EPS).astype(BF16)
    xc0, gate0 = _lru_conv(lx_prev[rows[0]], lg_prev[rows[0]], p_ref, cw_ref, cb_ref, xtail_ref)
    lx_ref[...] = jnp.dot(h, w_ref[:, :width], preferred_element_type=F32).astype(BF16)
    pre0 = _lru_gates(xc0, wg_ref, bg_ref)
    xc1, gate1 = _lru_conv(lx_prev[rows[1]], lg_prev[rows[1]], p_ref, cw_ref, cb_ref, xtail_ref)
    lg_ref[...] = jnp.dot(h, w_ref[:, width:2 * width], preferred_element_type=F32).astype(BF16)
    pre1 = _lru_gates(xc1, wg_ref, bg_ref)
    out0 = _lru_scan(xc0, gate0, *pre0, lam_ref, hend_ref)
    k_ref[...] = jnp.dot(h, w_ref[:, 2 * width:], preferred_element_type=F32).astype(k_ref.dtype)
    oa_ref[rows[0], :] = unperm(out0)
    out1 = _lru_scan(xc1, gate1, *pre1, lam_ref, hend_ref)
    qt = lax.dot_general(wt_ref[:width, :], h, nt, preferred_element_type=F32)
    qt_ref[...] = (qt * qk_scale).astype(qt_ref.dtype)
    oa_ref[rows[1], :] = unperm(out1)
    vt = lax.dot_general(wt_ref[width:, :], h, nt, preferred_element_type=F32)
    vt_ref[...] = vt.astype(vt_ref.dtype)


def _inproj(x2, g, w_rows, w_cols_t, conv_w, conv_b, w_gates, b_gates, lam, bsz, seq, tm, lru_tile, qk_scale):
    n, d = x2.shape
    width = w_rows.shape[1] // 3
    spb = seq // tm
    last = n // tm - 1
    perm = _segment_perm(lru_tile)
    p_mat, pt_mat = jnp.asarray(perm, BF16), jnp.asarray(perm.T, BF16)
    consts = (g, w_rows, w_cols_t, p_mat, pt_mat, conv_w, conv_b, w_gates, b_gates, lam)
    proj = lambda i: jnp.minimum(i, last)
    col_spec = pl.BlockSpec((None, width, tm), lambda i: (proj(i) // spb, 0, proj(i) % spb))
    return pl.pallas_call(
        functools.partial(_inproj_kernel, qk_scale, spb),
        grid=(n // tm + 1,),
        in_specs=[pl.BlockSpec((tm, d), lambda i: (proj(i), 0))] + [_const_spec(c.shape) for c in consts],
        out_specs=[pl.BlockSpec((tm, width), lambda i: (jnp.maximum(i - 1, 0), 0)),
                   pl.BlockSpec((tm, width), lambda i: (proj(i), 0)), col_spec, col_spec],
        out_shape=[jax.ShapeDtypeStruct((n, width), BF16), jax.ShapeDtypeStruct((n, width), BF16),
                   jax.ShapeDtypeStruct((bsz, width, seq), BF16), jax.ShapeDtypeStruct((bsz, width, seq), BF16)],
        scratch_shapes=[pltpu.VMEM((tm, width), BF16), pltpu.VMEM((tm, width), BF16),
                        pltpu.VMEM(((LRU_CONV - 1) * SUBLANES, width), F32), pltpu.VMEM((SUBLANES, width), F32)],
        compiler_params=pltpu.CompilerParams(dimension_semantics=("arbitrary",), vmem_limit_bytes=VMEM_LIMIT),
        name="inproj_lru",
    )(x2, *consts)


def _rel_bucket_np(rel):
    half = REL_BUCKETS // 2
    max_exact = half // 2
    ret = (rel > 0).astype(np.int32) * half
    n = np.abs(rel)
    nf = np.maximum(n, 1).astype(np.float32)
    frac = np.log(nf / np.float32(max_exact)) / np.float32(math.log(REL_MAX_DIST / max_exact))
    large = max_exact + (frac * np.float32(half - max_exact)).astype(np.int32)
    large = np.minimum(large, half - 1)
    return ret + np.where(n < max_exact, n, large)


def _bucket_tile(tq):
    ql = np.arange(tq)[None, :]
    kpos = np.arange(2 * tq)[:, None] - tq
    bucket = _rel_bucket_np(kpos - ql).astype(np.int32)
    allowed = (kpos // CHUNK) <= (ql // CHUNK)
    return np.where(allowed, bucket, -1).astype(np.int32)


def _far_bucket(tq, seq):
    far = _rel_bucket_np(-np.arange(tq + 1, seq))
    assert (far == far[0]).all()
    return int(far[0])


def _bias_kernel(rb_ref, bucket_ref, o_ref):
    h = pl.program_id(0)
    bucket = bucket_ref[...]
    acc = jnp.zeros(bucket.shape, F32)
    for b in range(REL_BUCKETS):
        acc = jnp.where(bucket == b, rb_ref[b, h], acc)
    o_ref[...] = jnp.where(bucket < 0, NEG_INF, acc)


def _bias_tiles(rel_bias, tq):
    bucket = jnp.asarray(_bucket_tile(tq))
    tk2 = bucket.shape[0]
    return pl.pallas_call(
        _bias_kernel,
        grid=(DIFF_HEADS,),
        in_specs=[pl.BlockSpec(memory_space=pltpu.SMEM), pl.BlockSpec((tk2, tq), lambda h: (0, 0))],
        out_specs=pl.BlockSpec((None, tk2, tq), lambda h: (h, 0, 0)),
        out_shape=jax.ShapeDtypeStruct((DIFF_HEADS, tk2, tq), F32),
        name="relbias",
    )(rel_bias, bucket)


def _attn_kernel(far_bucket, tq, rb_ref, qt_ref, k_ref, vt_ref, bias_ref, lq1_ref, lk1_ref, lq2_ref, lk2_ref,
                 sg_ref, o_ref):
    h = pl.program_id(1)
    hd, seq = qt_ref.shape
    qk = hd // 2
    far_bias = rb_ref[far_bucket, h]
    lam = (jnp.exp(jnp.sum(lq1_ref[...] * lk1_ref[...], axis=1, keepdims=True))
           - jnp.exp(jnp.sum(lq2_ref[...] * lk2_ref[...], axis=1, keepdims=True)) + LAMBDA_INIT)
    sub = lax.broadcasted_iota(jnp.int32, (hd, tq), 0)
    near_lo = lambda i: max(i - 1, 0) * tq

    def logits(i, c):
        qt = qt_ref[:, i * tq:(i + 1) * tq]
        qtc = jnp.where((sub < qk) if c == 0 else (sub >= qk), qt, jnp.zeros_like(qt))
        bias = bias_ref[...] if i > 0 else bias_ref[tq:, :]
        s_near = jnp.dot(k_ref[near_lo(i):(i + 1) * tq, :], qtc, preferred_element_type=F32) + bias
        s_far = jnp.dot(k_ref[:near_lo(i), :], qtc, preferred_element_type=F32) if near_lo(i) else None
        return s_near, s_far

    def softmax_pv(i, s_near, s_far):
        m = jnp.max(s_near, axis=0, keepdims=True)
        if s_far is not None:
            m = jnp.maximum(m, jnp.max(s_far, axis=0, keepdims=True) + far_bias)
        p_near = jnp.exp(s_near - m)
        l = jnp.sum(p_near, axis=0, keepdims=True)
        pv = jnp.dot(vt_ref[:, near_lo(i):(i + 1) * tq], p_near.astype(BF16), preferred_element_type=F32)
        if s_far is not None:
            p_far = jnp.exp(s_far - (m - far_bias))
            l = l + jnp.sum(p_far, axis=0, keepdims=True)
            pv = pv + jnp.dot(vt_ref[:, :near_lo(i)], p_far.astype(BF16), preferred_element_type=F32)
        return pv * (1.0 / l)

    units = [(i, c) for i in range(seq // tq) for c in range(2)]
    pending = logits(*units[0])
    maps = []
    for n, (i, c) in enumerate(units):
        current = pending
        if n + 1 < len(units):
            pending = logits(*units[n + 1])
        maps.append(softmax_pv(i, *current))
        if c == 1:
            out = maps[0] - lam * maps[1]
            maps = []
            out = out * lax.rsqrt(jnp.mean(out * out, axis=0, keepdims=True) + SUBLN_EPS)
            o_ref[i * tq:(i + 1) * tq, :] = (out.T * sg_ref[...] * (1.0 - LAMBDA_INIT)).astype(o_ref.dtype)


def _attn(qt, k, vt, bias, rel_bias, lq1, lk1, lq2, lk2, subln_g, tq):
    bsz, seq, width = k.shape
    hd = width // DIFF_HEADS
    far_bucket = _far_bucket(tq, seq)
    col_spec = pl.BlockSpec((None, hd, seq), lambda b, h: (b, h, 0))
    row_spec = pl.BlockSpec((None, seq, hd), lambda b, h: (b, 0, h))
    return pl.pallas_call(
        functools.partial(_attn_kernel, far_bucket, tq),
        grid=(bsz, DIFF_HEADS),
        in_specs=[pl.BlockSpec(memory_space=pltpu.SMEM), col_spec, row_spec, col_spec,
                  pl.BlockSpec((None, 2 * tq, tq), lambda b, h: (h, 0, 0)),
                  _const_spec(lq1.shape), _const_spec(lk1.shape), _const_spec(lq2.shape),
                  _const_spec(lk2.shape), _const_spec(subln_g.shape)],
        out_specs=row_spec,
        out_shape=jax.ShapeDtypeStruct((bsz, seq, width), BF16),
        compiler_params=pltpu.CompilerParams(dimension_semantics=("arbitrary",) * 2,
                                             vmem_limit_bytes=VMEM_LIMIT),
        name="diffattn",
    )(rel_bias, qt, k, vt, bias, lq1, lk1, lq2, lk2, subln_g)


def _ffn_kernel(steps_per_seq, n_chunks, x_ref, a_ref, b_ref, wo_ref, g2_ref, wu_ref, cw_ref, cb_ref,
                wd_ref, gf_ref, o_ref, tail_ref):
    tm = x_ref.shape[0]
    half = a_ref.shape[1]
    d_ff = wd_ref.shape[0]
    fc = d_ff // n_chunks

    @pl.when(pl.program_id(0) % steps_per_seq == 0)
    def _():
        tail_ref[...] = jnp.zeros_like(tail_ref)

    x1 = (x_ref[...]
          + jnp.dot(a_ref[...], wo_ref[:half, :], preferred_element_type=F32)
          + jnp.dot(b_ref[...], wo_ref[half:, :], preferred_element_type=F32))
    h2 = _rms(x1, g2_ref[...], EPS).astype(BF16)
    ffn = None
    for c in range(n_chunks):
        cols = slice(c * fc, (c + 1) * fc)
        gate = jnp.dot(h2, wu_ref[:, cols], preferred_element_type=F32)
        val = jnp.dot(h2, wu_ref[:, d_ff + c * fc:d_ff + (c + 1) * fc], preferred_element_type=F32)
        ext = jnp.concatenate([tail_ref[:, cols], gate], axis=0)
        conv = cb_ref[:, cols] + cw_ref[FFN_CONV - 1:FFN_CONV, cols] * gate
        for s in range(1, FFN_CONV):
            conv = conv + cw_ref[FFN_CONV - 1 - s:FFN_CONV - s, cols] * _shift_rows(ext, s)
        tail_ref[:, cols] = gate[tm - SUBLANES:]
        act = (_gelu_tanh(conv) * val).astype(BF16)
        down = jnp.dot(act, wd_ref[cols, :], preferred_element_type=F32)
        ffn = down if ffn is None else ffn + down
    o_ref[...] = _rms(x1 + ffn, gf_ref[...], EPS)


def _ffn(x2, out_a, out_b, w_out, g2, w_up, conv_w, conv_b, w_down, gf, seq, tm, n_chunks):
    n, d = x2.shape
    half = out_a.shape[1]
    d_ff = w_down.shape[0]
    return pl.pallas_call(
        functools.partial(_ffn_kernel, seq // tm, n_chunks),
        grid=(n // tm,),
        in_specs=[pl.BlockSpec((tm, d), lambda i: (i, 0)),
                  pl.BlockSpec((tm, half), lambda i: (i, 0)),
                  pl.BlockSpec((tm, half), lambda i: (i, 0)),
                  _const_spec(w_out.shape), _const_spec(g2.shape), _const_spec(w_up.shape),
                  _const_spec(conv_w.shape), _const_spec(conv_b.shape), _const_spec(w_down.shape),
                  _const_spec(gf.shape)],
        out_specs=pl.BlockSpec((tm, d), lambda i: (i, 0)),
        out_shape=jax.ShapeDtypeStruct((n, d), F32),
        scratch_shapes=[pltpu.VMEM((SUBLANES, d_ff), F32)],
        compiler_params=pltpu.CompilerParams(dimension_semantics=("arbitrary",), vmem_limit_bytes=VMEM_LIMIT),
        name="outproj_ffn",
    )(x2, out_a, out_b, w_out, g2, w_up, conv_w, conv_b, w_down, gf)


def _block_diag(w):
    nb, bd, _ = w.shape
    eye = jnp.eye(nb, dtype=w.dtype)
    return (eye[:, None, :, None] * w[:, :, None, :]).reshape(nb * bd, nb * bd)


def kernel(x, norm1_g, w_in, lru_conv_w, lru_conv_b, lru_wa, lru_ba, lru_wx, lru_bx, lru_lambda, diff_lq1, diff_lk1, diff_lq2, diff_lk2, diff_subln_g, rel_bias, w_out, norm2_g, ffn_w_up, ffn_conv_w, ffn_conv_b, ffn_w_down, final_norm_g):
    bsz, seq, d = x.shape
    assert w_in.shape[0] == 1, "single-layer block"
    n = bsz * seq
    x2 = x.reshape(n, d)
    row = lambda p: p.reshape(1, -1)

    lru_w = lru_wa.shape[1] * lru_wa.shape[2]
    attn_w = (w_in.shape[2] - 2 * lru_w) // 3
    qk_dim = attn_w // DIFF_HEADS // 2
    tq = 256
    w_in_b = w_in[0].astype(BF16)
    q_cols = w_in_b[:, 2 * lru_w:2 * lru_w + attn_w]
    k_cols = w_in_b[:, 2 * lru_w + attn_w:2 * lru_w + 2 * attn_w]
    v_cols = w_in_b[:, 2 * lru_w + 2 * attn_w:]
    w_rows = jnp.concatenate([w_in_b[:, :2 * lru_w], k_cols], axis=1)
    w_cols_t = jnp.concatenate([q_cols, v_cols], axis=1).T
    grp = LRU_BLOCKS // LRU_GATE_GROUPS
    w_gates = jnp.stack([
        jnp.concatenate([_block_diag(lru_wa[0, j * grp:(j + 1) * grp]),
                         _block_diag(lru_wx[0, j * grp:(j + 1) * grp])], axis=1)
        for j in range(LRU_GATE_GROUPS)]).astype(BF16)
    b_gates = jnp.stack([lru_ba[0].reshape(lru_w), lru_bx[0].reshape(lru_w)])
    out_a, k, qt, vt = _inproj(x2, row(norm1_g[0]), w_rows, w_cols_t, lru_conv_w[0], row(lru_conv_b[0]),
                               w_gates, b_gates, row(lru_lambda[0]), bsz, seq, tm=512, lru_tile=256,
                               qk_scale=qk_dim ** -0.5)

    bias = _bias_tiles(rel_bias, tq)
    out_b = _attn(qt, k.reshape(bsz, seq, attn_w), vt, bias, rel_bias,
                  row(diff_lq1[0]), row(diff_lk1[0]), row(diff_lq2[0]), row(diff_lk2[0]),
                  row(diff_subln_g[0]), tq)

    out = _ffn(x2, out_a, out_b.reshape(n, -1), w_out[0].astype(BF16), row(norm2_g[0]),
               ffn_w_up[0].astype(BF16), ffn_conv_w[0], row(ffn_conv_b[0]), ffn_w_down[0].astype(BF16),
               row(final_norm_g), seq, tm=512, n_chunks=3)
    return out.reshape(bsz, seq, d)
```

```python
import functools
import math

import jax
import jax.numpy as jnp
import numpy as np
from jax import lax
from jax.experimental import pallas as pl
from jax.experimental.pallas import tpu as pltpu

CHUNK = 64
LRU_BLOCKS = 8
LRU_CONV = 4
LRU_C = 8.0
LRU_GATE_GROUPS = 2
DIFF_HEADS = 4
REL_BUCKETS = 32
REL_MAX_DIST = 128
FFN_CONV = 3
EPS = 1e-6
SUBLN_EPS = 1e-5
NEG_INF = -1e30
LAMBDA_INIT = 0.8 - 0.6 * math.exp(-0.3 * 0)

LOG2E = math.log2(math.e)
LOOKAHEAD = 4

SUBLANES = 8
BF16_ROWS = 16
VMEM_LIMIT = 56 * 1024 * 1024

BF16 = jnp.bfloat16
F32 = jnp.float32


def _rms(x, g, eps):
    return x * lax.rsqrt(jnp.mean(x * x, axis=-1, keepdims=True) + eps) * g


def _gelu_tanh(x):
    inner = math.sqrt(2.0 / math.pi) * (x + 0.044715 * (x * x * x))
    return 0.5 * x * (1.0 + jnp.tanh(inner))


def _const_spec(shape):
    nd = len(shape)
    return pl.BlockSpec(shape, lambda *_: (0,) * nd, pipeline_mode=pl.Buffered(1))


def _shift_rows(ext, s):
    return pltpu.roll(ext, s, 0)[SUBLANES:]


def _segment_perm(tile):
    seg_len = tile // SUBLANES
    r = np.arange(tile)
    perm = np.zeros((tile, tile), np.float32)
    perm[r, (r % SUBLANES) * seg_len + r // SUBLANES] = 1.0
    return perm


def _lru_conv(lx, lg, p_ref, cw_ref, cb_ref, xtail_ref):
    tile, width = lx.shape
    npos = tile // SUBLANES
    ntap = LRU_CONV - 1
    sub = lax.broadcasted_iota(jnp.int32, (SUBLANES, width), 0)
    slab = lambda v, p: v[p * SUBLANES:(p + 1) * SUBLANES]
    x = jnp.dot(p_ref[...], lx, preferred_element_type=F32)
    gate = jnp.dot(p_ref[...], lg, preferred_element_type=F32)
    wrapped = []
    for q in range(ntap):
        cur = slab(x, npos - ntap + q)
        prev = slab(xtail_ref[...], q)
        wrapped.append(pltpu.roll(jnp.where(sub == SUBLANES - 1, prev, cur), 1, 0))
    xtail_ref[...] = x[tile - ntap * SUBLANES:]
    xc = cb_ref[...] + cw_ref[LRU_CONV - 1:LRU_CONV, :] * x
    for s in range(1, LRU_CONV):
        shifted = jnp.concatenate(wrapped[ntap - s:] + [x[:tile - SUBLANES * s]], axis=0)
        xc = xc + cw_ref[LRU_CONV - 1 - s:LRU_CONV - s, :] * shifted
    return xc, gate


def _lru_gates(xc, wg_ref, bg_ref):
    xcb = xc.astype(BF16)
    ngrp, gw, _ = wg_ref.shape
    pres = [jnp.dot(xcb[:, j * gw:(j + 1) * gw], wg_ref[j], preferred_element_type=F32) for j in range(ngrp)]
    pre_r = jnp.concatenate([pj[:, :gw] for pj in pres], axis=1) + bg_ref[0:1, :]
    pre_i = jnp.concatenate([pj[:, gw:] for pj in pres], axis=1) + bg_ref[1:2, :]
    return pre_r, pre_i


def _lru_scan(xc, gate, pre_r, pre_i, lam_ref, hend_ref):
    tile, width = xc.shape
    npos = tile // SUBLANES
    sub = lax.broadcasted_iota(jnp.int32, (SUBLANES, width), 0)
    slab = lambda v, p: v[p * SUBLANES:(p + 1) * SUBLANES]
    r = jax.nn.sigmoid(pre_r)
    gate_i = jax.nn.sigmoid(pre_i)
    z = -lam_ref[...]
    softplus = jnp.maximum(z, 0.0) + jnp.log(1.0 + jnp.exp(-jnp.abs(z)))
    a = jnp.exp((-LRU_C) * r * softplus)
    y = 1.0 - a * a
    u = jnp.where(y > 0.0, y * lax.rsqrt(y), 0.0) * (gate_i * xc)

    hs = [slab(u, 0)]
    cum = [slab(a, 0)]
    for p in range(1, npos):
        ap = slab(a, p)
        hs.append(ap * hs[-1] + slab(u, p))
        cum.append(ap * cum[-1])
    inc = pltpu.roll(jnp.where(sub == SUBLANES - 1, hend_ref[...], hs[-1]), 1, 0)
    dec = jnp.where(sub == 0, 0.0, pltpu.roll(cum[-1], 1, 0))
    d = 1
    while d < SUBLANES:
        keep = sub >= d
        inc = jnp.where(keep, dec * pltpu.roll(inc, d, 0) + inc, inc)
        dec = jnp.where(keep, dec * pltpu.roll(dec, d, 0), dec)
        d *= 2
    h = jnp.concatenate([cum[p] * inc + hs[p] for p in range(npos)], axis=0)
    hend_ref[...] = h[tile - SUBLANES:]
    return (_gelu_tanh(gate) * h).astype(BF16)


def _inproj_kernel(qk_scale, steps_per_seq, x_ref, g_ref, w_ref, wt_ref, p_ref, pt_ref, cw_ref, cb_ref, wg_ref,
                   bg_ref, lam_ref, oa_ref, k_ref, qt_ref, vt_ref, lx_ref, lg_ref, xtail_ref, hend_ref):
    i = pl.program_id(0)

    @pl.when(i == 0)
    def _():
        lx_ref[...] = jnp.zeros_like(lx_ref)
        lg_ref[...] = jnp.zeros_like(lg_ref)

    @pl.when((i + steps_per_seq - 1) % steps_per_seq == 0)
    def _():
        xtail_ref[...] = jnp.zeros_like(xtail_ref)
        hend_ref[...] = jnp.zeros_like(hend_ref)

    lx_prev = lx_ref[...]
    lg_prev = lg_ref[...]
    tm, width = oa_ref.shape
    tile = p_ref.shape[0]
    assert tm == 2 * tile, "the interleaving below is written for two scan tiles per step"
    rows = (slice(0, tile), slice(tile, tm))
    nt = (((1,), (1,)), ((), ()))
    unperm = lambda out: jnp.dot(pt_ref[...], out, preferred_element_type=F32).astype(oa_ref.dtype)

    h = _rms(x_ref[...], g_ref[...], EPS).astype(BF16)
    xc0, gate0 = _lru_conv(lx_prev[rows[0]], lg_prev[rows[0]], p_ref, cw_ref, cb_ref, xtail_ref)
    lx_ref[...] = jnp.dot(h, w_ref[:, :width], preferred_element_type=F32).astype(BF16)
    pre0 = _lru_gates(xc0, wg_ref, bg_ref)
    xc1, gate1 = _lru_conv(lx_prev[rows[1]], lg_prev[rows[1]], p_ref, cw_ref, cb_ref, xtail_ref)
    lg_ref[...] = jnp.dot(h, w_ref[:, width:2 * width], preferred_element_type=F32).astype(BF16)
    pre1 = _lru_gates(xc1, wg_ref, bg_ref)
    out0 = _lru_scan(xc0, gate0, *pre0, lam_ref, hend_ref)
    k_ref[...] = jnp.dot(h, w_ref[:, 2 * width:], preferred_element_type=F32).astype(k_ref.dtype)
    oa_ref[rows[0], :] = unperm(out0)
    out1 = _lru_scan(xc1, gate1, *pre1, lam_ref, hend_ref)
    qt = lax.dot_general(wt_ref[:width, :], h, nt, preferred_element_type=F32)
    qt_ref[...] = (qt * qk_scale).astype(qt_ref.dtype)
    oa_ref[rows[1], :] = unperm(out1)
    vt = lax.dot_general(wt_ref[width:, :], h, nt, preferred_element_type=F32)
    vt_ref[...] = vt.astype(vt_ref.dtype)


def _inproj(x2, g, w_rows, w_cols_t, conv_w, conv_b, w_gates, b_gates, lam, bsz, seq, tm, lru_tile, qk_scale):
    n, d = x2.shape
    width = w_rows.shape[1] // 3
    spb = seq // tm
    last = n // tm - 1
    perm = _segment_perm(lru_tile)
    p_mat, pt_mat = jnp.asarray(perm, BF16), jnp.asarray(perm.T, BF16)
    consts = (g, w_rows, w_cols_t, p_mat, pt_mat, conv_w, conv_b, w_gates, b_gates, lam)
    proj = lambda i: jnp.minimum(i, last)
    col_spec = pl.BlockSpec((None, width, tm), lambda i: (proj(i) // spb, 0, proj(i) % spb))
    return pl.pallas_call(
        functools.partial(_inproj_kernel, qk_scale, spb),
        grid=(n // tm + 1,),
        in_specs=[pl.BlockSpec((tm, d), lambda i: (proj(i), 0))] + [_const_spec(c.shape) for c in consts],
        out_specs=[pl.BlockSpec((tm, width), lambda i: (jnp.maximum(i - 1, 0), 0)),
                   pl.BlockSpec((tm, width), lambda i: (proj(i), 0)), col_spec, col_spec],
        out_shape=[jax.ShapeDtypeStruct((n, width), BF16), jax.ShapeDtypeStruct((n, width), BF16),
                   jax.ShapeDtypeStruct((bsz, width, seq), BF16), jax.ShapeDtypeStruct((bsz, width, seq), BF16)],
        scratch_shapes=[pltpu.VMEM((tm, width), BF16), pltpu.VMEM((tm, width), BF16),
                        pltpu.VMEM(((LRU_CONV - 1) * SUBLANES, width), F32), pltpu.VMEM((SUBLANES, width), F32)],
        compiler_params=pltpu.CompilerParams(dimension_semantics=("arbitrary",), vmem_limit_bytes=VMEM_LIMIT),
        name="inproj_lru",
    )(x2, *consts)


def _rel_bucket_np(rel):
    half = REL_BUCKETS // 2
    max_exact = half // 2
    ret = (rel > 0).astype(np.int32) * half
    n = np.abs(rel)
    nf = np.maximum(n, 1).astype(np.float32)
    frac = np.log(nf / np.float32(max_exact)) / np.float32(math.log(REL_MAX_DIST / max_exact))
    large = max_exact + (frac * np.float32(half - max_exact)).astype(np.int32)
    large = np.minimum(large, half - 1)
    return ret + np.where(n < max_exact, n, large)


def _bucket_tile(tq):
    ql = np.arange(tq)[None, :]
    kpos = np.arange(2 * tq)[:, None] - tq
    bucket = _rel_bucket_np(kpos - ql).astype(np.int32)
    allowed = (kpos // CHUNK) <= (ql // CHUNK)
    return np.where(allowed, bucket, -1).astype(np.int32)


def _far_bucket(tq, seq):
    far = _rel_bucket_np(-np.arange(tq + 1, seq))
    assert (far == far[0]).all()
    return int(far[0])


def _bias_kernel(rb_ref, bucket_ref, o_ref):
    h = pl.program_id(0)
    bucket = bucket_ref[...]
    acc = jnp.zeros(bucket.shape, F32)
    for b in range(REL_BUCKETS):
        acc = jnp.where(bucket == b, rb_ref[b, h], acc)
    o_ref[...] = jnp.where(bucket < 0, NEG_INF, acc) * LOG2E


def _bias_tiles(rel_bias, tq):
    bucket = jnp.asarray(_bucket_tile(tq))
    tk2 = bucket.shape[0]
    return pl.pallas_call(
        _bias_kernel,
        grid=(DIFF_HEADS,),
        in_specs=[pl.BlockSpec(memory_space=pltpu.SMEM), pl.BlockSpec((tk2, tq), lambda h: (0, 0))],
        out_specs=pl.BlockSpec((None, tk2, tq), lambda h: (h, 0, 0)),
        out_shape=jax.ShapeDtypeStruct((DIFF_HEADS, tk2, tq), F32),
        name="relbias",
    )(rel_bias, bucket)


def _attn_kernel(far_bucket, tq, rb_ref, qt_ref, k_ref, vt_ref, bias_ref, lq1_ref, lk1_ref, lq2_ref, lk2_ref,
                 sg_ref, o_ref):
    h = pl.program_id(1)
    hd, seq = qt_ref.shape
    qk = hd // 2
    far_bias = rb_ref[far_bucket, h] * LOG2E
    lam = (jnp.exp(jnp.sum(lq1_ref[...] * lk1_ref[...], axis=1, keepdims=True))
           - jnp.exp(jnp.sum(lq2_ref[...] * lk2_ref[...], axis=1, keepdims=True)) + LAMBDA_INIT)
    sub = lax.broadcasted_iota(jnp.int32, (hd, tq), 0)
    near_lo = lambda i: max(i - 1, 0) * tq
    ones = jnp.ones((BF16_ROWS, seq), BF16)
    values = lambda lo, hi: jnp.concatenate([vt_ref[:, lo:hi], ones[:, lo:hi]], axis=0)

    def logits(i, c):
        qt = qt_ref[:, i * tq:(i + 1) * tq]
        qtc = jnp.where((sub < qk) if c == 0 else (sub >= qk), qt, jnp.zeros_like(qt))
        bias = bias_ref[...] if i > 0 else bias_ref[tq:, :]
        s_near = jnp.dot(k_ref[near_lo(i):(i + 1) * tq, :], qtc, preferred_element_type=F32) + bias
        s_far = jnp.dot(k_ref[:near_lo(i), :], qtc, preferred_element_type=F32) if near_lo(i) else None
        return s_near, s_far

    def softmax_pv(i, s_near, s_far):
        m = jnp.max(s_near, axis=0, keepdims=True)
        if s_far is not None:
            m = jnp.maximum(m, jnp.max(s_far, axis=0, keepdims=True) + far_bias)
        p_near = jnp.exp2(s_near - m)
        pv = jnp.dot(values(near_lo(i), (i + 1) * tq), p_near.astype(BF16), preferred_element_type=F32)
        if s_far is not None:
            p_far = jnp.exp2(s_far - (m - far_bias))
            pv = pv + jnp.dot(values(0, near_lo(i)), p_far.astype(BF16), preferred_element_type=F32)
        return pv[:hd] * (1.0 / pv[hd:hd + 1])

    units = [(i, c) for i in range(seq // tq) for c in range(2)]
    pending = [logits(*u) for u in units[:LOOKAHEAD]]
    maps = []
    for n, (i, c) in enumerate(units):
        current = pending.pop(0)
        if n + LOOKAHEAD < len(units):
            pending.append(logits(*units[n + LOOKAHEAD]))
        maps.append(softmax_pv(i, *current))
        if c == 1:
            out = maps[0] - lam * maps[1]
            maps = []
            out = out * lax.rsqrt(jnp.mean(out * out, axis=0, keepdims=True) + SUBLN_EPS)
            o_ref[i * tq:(i + 1) * tq, :] = (out.T * sg_ref[...] * (1.0 - LAMBDA_INIT)).astype(o_ref.dtype)


def _attn(qt, k, vt, bias, rel_bias, lq1, lk1, lq2, lk2, subln_g, tq):
    bsz, seq, width = k.shape
    hd = width // DIFF_HEADS
    far_bucket = _far_bucket(tq, seq)
    col_spec = pl.BlockSpec((None, hd, seq), lambda b, h: (b, h, 0))
    row_spec = pl.BlockSpec((None, seq, hd), lambda b, h: (b, 0, h))
    return pl.pallas_call(
        functools.partial(_attn_kernel, far_bucket, tq),
        grid=(bsz, DIFF_HEADS),
        in_specs=[pl.BlockSpec(memory_space=pltpu.SMEM), col_spec, row_spec, col_spec,
                  pl.BlockSpec((None, 2 * tq, tq), lambda b, h: (h, 0, 0)),
                  _const_spec(lq1.shape), _const_spec(lk1.shape), _const_spec(lq2.shape),
                  _const_spec(lk2.shape), _const_spec(subln_g.shape)],
        out_specs=row_spec,
        out_shape=jax.ShapeDtypeStruct((bsz, seq, width), BF16),
        compiler_params=pltpu.CompilerParams(dimension_semantics=("arbitrary",) * 2,
                                             vmem_limit_bytes=VMEM_LIMIT),
        name="diffattn",
    )(rel_bias, qt, k, vt, bias, lq1, lk1, lq2, lk2, subln_g)


def _ffn_kernel(steps_per_seq, n_chunks, x_ref, a_ref, b_ref, wo_ref, g2_ref, wu_ref, cw_ref, cb_ref,
                wd_ref, gf_ref, o_ref, tail_ref):
    tm = x_ref.shape[0]
    half = a_ref.shape[1]
    d_ff = wd_ref.shape[0]
    fc = d_ff // n_chunks

    @pl.when(pl.program_id(0) % steps_per_seq == 0)
    def _():
        tail_ref[...] = jnp.zeros_like(tail_ref)

    x1 = (x_ref[...]
          + jnp.dot(a_ref[...], wo_ref[:half, :], preferred_element_type=F32)
          + jnp.dot(b_ref[...], wo_ref[half:, :], preferred_element_type=F32))
    h2 = _rms(x1, g2_ref[...], EPS).astype(BF16)
    def up(c):
        gate = jnp.dot(h2, wu_ref[:, c * fc:(c + 1) * fc], preferred_element_type=F32)
        val = jnp.dot(h2, wu_ref[:, d_ff + c * fc:d_ff + (c + 1) * fc], preferred_element_type=F32)
        return gate, val

    def activate(c, gate, val):
        cols = slice(c * fc, (c + 1) * fc)
        ext = jnp.concatenate([tail_ref[:, cols], gate], axis=0)
        conv = cb_ref[:, cols] + cw_ref[FFN_CONV - 1:FFN_CONV, cols] * gate
        for s in range(1, FFN_CONV):
            conv = conv + cw_ref[FFN_CONV - 1 - s:FFN_CONV - s, cols] * _shift_rows(ext, s)
        tail_ref[:, cols] = gate[tm - SUBLANES:]
        return (_gelu_tanh(conv) * val).astype(BF16)

    ffn = None
    pending = up(0)
    for c in range(n_chunks):
        current = pending
        if c + 1 < n_chunks:
            pending = up(c + 1)
        act = activate(c, *current)
        down = jnp.dot(act, wd_ref[c * fc:(c + 1) * fc, :], preferred_element_type=F32)
        ffn = down if ffn is None else ffn + down
    o_ref[...] = _rms(x1 + ffn, gf_ref[...], EPS)


def _ffn(x2, out_a, out_b, w_out, g2, w_up, conv_w, conv_b, w_down, gf, seq, tm, n_chunks):
    n, d = x2.shape
    half = out_a.shape[1]
    d_ff = w_down.shape[0]
    return pl.pallas_call(
        functools.partial(_ffn_kernel, seq // tm, n_chunks),
        grid=(n // tm,),
        in_specs=[pl.BlockSpec((tm, d), lambda i: (i, 0)),
                  pl.BlockSpec((tm, half), lambda i: (i, 0)),
                  pl.BlockSpec((tm, half), lambda i: (i, 0)),
                  _const_spec(w_out.shape), _const_spec(g2.shape), _const_spec(w_up.shape),
                  _const_spec(conv_w.shape), _const_spec(conv_b.shape), _const_spec(w_down.shape),
                  _const_spec(gf.shape)],
        out_specs=pl.BlockSpec((tm, d), lambda i: (i, 0)),
        out_shape=jax.ShapeDtypeStruct((n, d), F32),
        scratch_shapes=[pltpu.VMEM((SUBLANES, d_ff), F32)],
        compiler_params=pltpu.CompilerParams(dimension_semantics=("arbitrary",), vmem_limit_bytes=VMEM_LIMIT),
        name="outproj_ffn",
    )(x2, out_a, out_b, w_out, g2, w_up, conv_w, conv_b, w_down, gf)


def _block_diag(w):
    nb, bd, _ = w.shape
    eye = jnp.eye(nb, dtype=w.dtype)
    return (eye[:, None, :, None] * w[:, :, None, :]).reshape(nb * bd, nb * bd)


def kernel(x, norm1_g, w_in, lru_conv_w, lru_conv_b, lru_wa, lru_ba, lru_wx, lru_bx, lru_lambda, diff_lq1, diff_lk1, diff_lq2, diff_lk2, diff_subln_g, rel_bias, w_out, norm2_g, ffn_w_up, ffn_conv_w, ffn_conv_b, ffn_w_down, final_norm_g):
    bsz, seq, d = x.shape
    assert w_in.shape[0] == 1, "single-layer block"
    n = bsz * seq
    x2 = x.reshape(n, d)
    row = lambda p: p.reshape(1, -1)

    lru_w = lru_wa.shape[1] * lru_wa.shape[2]
    attn_w = (w_in.shape[2] - 2 * lru_w) // 3
    qk_dim = attn_w // DIFF_HEADS // 2
    tq = 256
    w_in_b = w_in[0].astype(BF16)
    q_cols = w_in_b[:, 2 * lru_w:2 * lru_w + attn_w]
    k_cols = w_in_b[:, 2 * lru_w + attn_w:2 * lru_w + 2 * attn_w]
    v_cols = w_in_b[:, 2 * lru_w + 2 * attn_w:]
    w_rows = jnp.concatenate([w_in_b[:, :2 * lru_w], k_cols], axis=1)
    w_cols_t = jnp.concatenate([q_cols, v_cols], axis=1).T
    grp = LRU_BLOCKS // LRU_GATE_GROUPS
    w_gates = jnp.stack([
        jnp.concatenate([_block_diag(lru_wa[0, j * grp:(j + 1) * grp]),
                         _block_diag(lru_wx[0, j * grp:(j + 1) * grp])], axis=1)
        for j in range(LRU_GATE_GROUPS)]).astype(BF16)
    b_gates = jnp.stack([lru_ba[0].reshape(lru_w), lru_bx[0].reshape(lru_w)])
    out_a, k, qt, vt = _inproj(x2, row(norm1_g[0]), w_rows, w_cols_t, lru_conv_w[0], row(lru_conv_b[0]),
                               w_gates, b_gates, row(lru_lambda[0]), bsz, seq, tm=512, lru_tile=256,
                               qk_scale=qk_dim ** -0.5 * LOG2E)

    bias = _bias_tiles(rel_bias, tq)
    out_b = _attn(qt, k.reshape(bsz, seq, attn_w), vt, bias, rel_bias,
                  row(diff_lq1[0]), row(diff_lk1[0]), row(diff_lq2[0]), row(diff_lk2[0]),
                  row(diff_subln_g[0]), tq)

    out = _ffn(x2, out_a, out_b.reshape(n, -1), w_out[0].astype(BF16), row(norm2_g[0]),
               ffn_w_up[0].astype(BF16), ffn_conv_w[0], row(ffn_conv_b[0]), ffn_w_down[0].astype(BF16),
               row(final_norm_g), seq, tm=512, n_chunks=3)
    return out.reshape(bsz, seq, d)
```

```python
import functools
import math

import jax
import jax.numpy as jnp
import numpy as np
from jax import lax
from jax.experimental import pallas as pl
from jax.experimental.pallas import tpu as pltpu

CHUNK = 64
LRU_BLOCKS = 8
LRU_CONV = 4
LRU_C = 8.0
LRU_GATE_GROUPS = 2
DIFF_HEADS = 4
REL_BUCKETS = 32
REL_MAX_DIST = 128
FFN_CONV = 3
EPS = 1e-6
SUBLN_EPS = 1e-5
NEG_INF = -1e30
LAMBDA_INIT = 0.8 - 0.6 * math.exp(-0.3 * 0)

LOG2E = math.log2(math.e)
LOOKAHEAD = 4
MATMUL_ROWS = 256

SUBLANES = 8
BF16_ROWS = 16
VMEM_LIMIT = 56 * 1024 * 1024

BF16 = jnp.bfloat16
F32 = jnp.float32


def _rms(x, g, eps):
    return x * lax.rsqrt(jnp.mean(x * x, axis=-1, keepdims=True) + eps) * g


def _gelu_tanh(x):
    inner = math.sqrt(2.0 / math.pi) * (x + 0.044715 * (x * x * x))
    return 0.5 * x * (1.0 + jnp.tanh(inner))


def _const_spec(shape):
    nd = len(shape)
    return pl.BlockSpec(shape, lambda *_: (0,) * nd, pipeline_mode=pl.Buffered(1))


def _shift_rows(ext, s):
    return pltpu.roll(ext, s, 0)[SUBLANES:]


def _segment_perm(tile):
    seg_len = tile // SUBLANES
    r = np.arange(tile)
    perm = np.zeros((tile, tile), np.float32)
    perm[r, (r % SUBLANES) * seg_len + r // SUBLANES] = 1.0
    return perm


def _lru_conv(lx, lg, p_ref, cw_ref, cb_ref, xtail_ref):
    tile, width = lx.shape
    npos = tile // SUBLANES
    ntap = LRU_CONV - 1
    sub = lax.broadcasted_iota(jnp.int32, (SUBLANES, width), 0)
    slab = lambda v, p: v[p * SUBLANES:(p + 1) * SUBLANES]
    x = jnp.dot(p_ref[...], lx, preferred_element_type=F32)
    gate = jnp.dot(p_ref[...], lg, preferred_element_type=F32)
    wrapped = []
    for q in range(ntap):
        cur = slab(x, npos - ntap + q)
        prev = slab(xtail_ref[...], q)
        wrapped.append(pltpu.roll(jnp.where(sub == SUBLANES - 1, prev, cur), 1, 0))
    xtail_ref[...] = x[tile - ntap * SUBLANES:]
    xc = cb_ref[...] + cw_ref[LRU_CONV - 1:LRU_CONV, :] * x
    for s in range(1, LRU_CONV):
        shifted = jnp.concatenate(wrapped[ntap - s:] + [x[:tile - SUBLANES * s]], axis=0)
        xc = xc + cw_ref[LRU_CONV - 1 - s:LRU_CONV - s, :] * shifted
    return xc, gate


def _lru_gates(xc, wg_ref, bg_ref):
    xcb = xc.astype(BF16)
    ngrp, gw, _ = wg_ref.shape
    pres = [jnp.dot(xcb[:, j * gw:(j + 1) * gw], wg_ref[j], preferred_element_type=F32) for j in range(ngrp)]
    pre_r = jnp.concatenate([pj[:, :gw] for pj in pres], axis=1) + bg_ref[0:1, :]
    pre_i = jnp.concatenate([pj[:, gw:] for pj in pres], axis=1) + bg_ref[1:2, :]
    return pre_r, pre_i


def _lru_scan(xc, gate, pre_r, pre_i, lam_ref, hend_ref):
    tile, width = xc.shape
    npos = tile // SUBLANES
    sub = lax.broadcasted_iota(jnp.int32, (SUBLANES, width), 0)
    slab = lambda v, p: v[p * SUBLANES:(p + 1) * SUBLANES]
    r = jax.nn.sigmoid(pre_r)
    gate_i = jax.nn.sigmoid(pre_i)
    z = -lam_ref[...]
    softplus = jnp.maximum(z, 0.0) + jnp.log(1.0 + jnp.exp(-jnp.abs(z)))
    a = jnp.exp((-LRU_C) * r * softplus)
    y = 1.0 - a * a
    u = jnp.where(y > 0.0, y * lax.rsqrt(y), 0.0) * (gate_i * xc)

    hs = [slab(u, 0)]
    cum = [slab(a, 0)]
    for p in range(1, npos):
        ap = slab(a, p)
        hs.append(ap * hs[-1] + slab(u, p))
        cum.append(ap * cum[-1])
    inc = pltpu.roll(jnp.where(sub == SUBLANES - 1, hend_ref[...], hs[-1]), 1, 0)
    dec = jnp.where(sub == 0, 0.0, pltpu.roll(cum[-1], 1, 0))
    d = 1
    while d < SUBLANES:
        keep = sub >= d
        inc = jnp.where(keep, dec * pltpu.roll(inc, d, 0) + inc, inc)
        dec = jnp.where(keep, dec * pltpu.roll(dec, d, 0), dec)
        d *= 2
    h = jnp.concatenate([cum[p] * inc + hs[p] for p in range(npos)], axis=0)
    hend_ref[...] = h[tile - SUBLANES:]
    return (_gelu_tanh(gate) * h).astype(BF16)


def _inproj_kernel(qk_scale, steps_per_seq, x_ref, g_ref, w_ref, wt_ref, p_ref, pt_ref, cw_ref, cb_ref, wg_ref,
                   bg_ref, lam_ref, oa_ref, k_ref, qt_ref, vt_ref, lx_ref, lg_ref, xtail_ref, hend_ref):
    i = pl.program_id(0)

    @pl.when(i == 0)
    def _():
        lx_ref[...] = jnp.zeros_like(lx_ref)
        lg_ref[...] = jnp.zeros_like(lg_ref)

    @pl.when((i + steps_per_seq - 1) % steps_per_seq == 0)
    def _():
        xtail_ref[...] = jnp.zeros_like(xtail_ref)
        hend_ref[...] = jnp.zeros_like(hend_ref)

    lx_prev = lx_ref[...]
    lg_prev = lg_ref[...]
    tm, width = oa_ref.shape
    tile = p_ref.shape[0]
    assert tm == 2 * tile, "the interleaving below is written for two scan tiles per step"
    rows = (slice(0, tile), slice(tile, tm))
    nt = (((1,), (1,)), ((), ()))
    unperm = lambda out: jnp.dot(pt_ref[...], out, preferred_element_type=F32).astype(oa_ref.dtype)

    h = _rms(x_ref[...], g_ref[...], EPS).astype(BF16)
    xc0, gate0 = _lru_conv(lx_prev[rows[0]], lg_prev[rows[0]], p_ref, cw_ref, cb_ref, xtail_ref)
    lx_ref[...] = jnp.dot(h, w_ref[:, :width], preferred_element_type=F32).astype(BF16)
    pre0 = _lru_gates(xc0, wg_ref, bg_ref)
    xc1, gate1 = _lru_conv(lx_prev[rows[1]], lg_prev[rows[1]], p_ref, cw_ref, cb_ref, xtail_ref)
    lg_ref[...] = jnp.dot(h, w_ref[:, width:2 * width], preferred_element_type=F32).astype(BF16)
    pre1 = _lru_gates(xc1, wg_ref, bg_ref)
    out0 = _lru_scan(xc0, gate0, *pre0, lam_ref, hend_ref)
    k_ref[...] = jnp.dot(h, w_ref[:, 2 * width:], preferred_element_type=F32).astype(k_ref.dtype)
    oa_ref[rows[0], :] = unperm(out0)
    out1 = _lru_scan(xc1, gate1, *pre1, lam_ref, hend_ref)
    qt = lax.dot_general(wt_ref[:width, :], h, nt, preferred_element_type=F32)
    qt_ref[...] = (qt * qk_scale).astype(qt_ref.dtype)
    oa_ref[rows[1], :] = unperm(out1)
    vt = lax.dot_general(wt_ref[width:, :], h, nt, preferred_element_type=F32)
    vt_ref[...] = vt.astype(vt_ref.dtype)


def _inproj(x2, g, w_rows, w_cols_t, conv_w, conv_b, w_gates, b_gates, lam, bsz, seq, tm, lru_tile, qk_scale):
    n, d = x2.shape
    width = w_rows.shape[1] // 3
    spb = seq // tm
    last = n // tm - 1
    perm = _segment_perm(lru_tile)
    p_mat, pt_mat = jnp.asarray(perm, BF16), jnp.asarray(perm.T, BF16)
    consts = (g, w_rows, w_cols_t, p_mat, pt_mat, conv_w, conv_b, w_gates, b_gates, lam)
    proj = lambda i: jnp.minimum(i, last)
    col_spec = pl.BlockSpec((None, width, tm), lambda i: (proj(i) // spb, 0, proj(i) % spb))
    return pl.pallas_call(
        functools.partial(_inproj_kernel, qk_scale, spb),
        grid=(n // tm + 1,),
        in_specs=[pl.BlockSpec((tm, d), lambda i: (proj(i), 0))] + [_const_spec(c.shape) for c in consts],
        out_specs=[pl.BlockSpec((tm, width), lambda i: (jnp.maximum(i - 1, 0), 0)),
                   pl.BlockSpec((tm, width), lambda i: (proj(i), 0)), col_spec, col_spec],
        out_shape=[jax.ShapeDtypeStruct((n, width), BF16), jax.ShapeDtypeStruct((n, width), BF16),
                   jax.ShapeDtypeStruct((bsz, width, seq), BF16), jax.ShapeDtypeStruct((bsz, width, seq), BF16)],
        scratch_shapes=[pltpu.VMEM((tm, width), BF16), pltpu.VMEM((tm, width), BF16),
                        pltpu.VMEM(((LRU_CONV - 1) * SUBLANES, width), F32), pltpu.VMEM((SUBLANES, width), F32)],
        compiler_params=pltpu.CompilerParams(dimension_semantics=("arbitrary",), vmem_limit_bytes=VMEM_LIMIT),
        name="inproj_lru",
    )(x2, *consts)


def _rel_bucket_np(rel):
    half = REL_BUCKETS // 2
    max_exact = half // 2
    ret = (rel > 0).astype(np.int32) * half
    n = np.abs(rel)
    nf = np.maximum(n, 1).astype(np.float32)
    frac = np.log(nf / np.float32(max_exact)) / np.float32(math.log(REL_MAX_DIST / max_exact))
    large = max_exact + (frac * np.float32(half - max_exact)).astype(np.int32)
    large = np.minimum(large, half - 1)
    return ret + np.where(n < max_exact, n, large)


def _bucket_tile(tq):
    ql = np.arange(tq)[None, :]
    kpos = np.arange(2 * tq)[:, None] - tq
    bucket = _rel_bucket_np(kpos - ql).astype(np.int32)
    allowed = (kpos // CHUNK) <= (ql // CHUNK)
    return np.where(allowed, bucket, -1).astype(np.int32)


def _far_bucket(tq, seq):
    far = _rel_bucket_np(-np.arange(tq + 1, seq))
    assert (far == far[0]).all()
    return int(far[0])


def _bias_kernel(rb_ref, bucket_ref, o_ref):
    h = pl.program_id(0)
    bucket = bucket_ref[...]
    acc = jnp.zeros(bucket.shape, F32)
    for b in range(REL_BUCKETS):
        acc = jnp.where(bucket == b, rb_ref[b, h], acc)
    o_ref[...] = jnp.where(bucket < 0, NEG_INF, acc) * LOG2E


def _bias_tiles(rel_bias, tq):
    bucket = jnp.asarray(_bucket_tile(tq))
    tk2 = bucket.shape[0]
    return pl.pallas_call(
        _bias_kernel,
        grid=(DIFF_HEADS,),
        in_specs=[pl.BlockSpec(memory_space=pltpu.SMEM), pl.BlockSpec((tk2, tq), lambda h: (0, 0))],
        out_specs=pl.BlockSpec((None, tk2, tq), lambda h: (h, 0, 0)),
        out_shape=jax.ShapeDtypeStruct((DIFF_HEADS, tk2, tq), F32),
        name="relbias",
    )(rel_bias, bucket)


def _attn_kernel(far_bucket, tq, rb_ref, qt_ref, k_ref, vt_ref, bias_ref, lq1_ref, lk1_ref, lq2_ref, lk2_ref,
                 sg_ref, o_ref):
    h = pl.program_id(1)
    hd, seq = qt_ref.shape
    qk = hd // 2
    far_bias = rb_ref[far_bucket, h] * LOG2E
    lam = (jnp.exp(jnp.sum(lq1_ref[...] * lk1_ref[...], axis=1, keepdims=True))
           - jnp.exp(jnp.sum(lq2_ref[...] * lk2_ref[...], axis=1, keepdims=True)) + LAMBDA_INIT)
    sub = lax.broadcasted_iota(jnp.int32, (hd, tq), 0)
    near_lo = lambda i: max(i - 1, 0) * tq
    ones = jnp.ones((BF16_ROWS, seq), BF16)
    values = lambda lo, hi: jnp.concatenate([vt_ref[:, lo:hi], ones[:, lo:hi]], axis=0)

    def logits(i, c):
        qt = qt_ref[:, i * tq:(i + 1) * tq]
        qtc = jnp.where((sub < qk) if c == 0 else (sub >= qk), qt, jnp.zeros_like(qt))
        bias = bias_ref[...] if i > 0 else bias_ref[tq:, :]
        s_near = jnp.dot(k_ref[near_lo(i):(i + 1) * tq, :], qtc, preferred_element_type=F32) + bias
        s_far = jnp.dot(k_ref[:near_lo(i), :], qtc, preferred_element_type=F32) if near_lo(i) else None
        return s_near, s_far

    def softmax_pv(i, s_near, s_far):
        m = jnp.max(s_near, axis=0, keepdims=True)
        if s_far is not None:
            m = jnp.maximum(m, jnp.max(s_far, axis=0, keepdims=True) + far_bias)
        p_near = jnp.exp2(s_near - m)
        pv = jnp.dot(values(near_lo(i), (i + 1) * tq), p_near.astype(BF16), preferred_element_type=F32)
        if s_far is not None:
            p_far = jnp.exp2(s_far - (m - far_bias))
            pv = pv + jnp.dot(values(0, near_lo(i)), p_far.astype(BF16), preferred_element_type=F32)
        return pv[:hd] * (1.0 / pv[hd:hd + 1])

    units = [(i, c) for i in range(seq // tq) for c in range(2)]
    pending = [logits(*u) for u in units[:LOOKAHEAD]]
    maps = []
    for n, (i, c) in enumerate(units):
        current = pending.pop(0)
        if n + LOOKAHEAD < len(units):
            pending.append(logits(*units[n + LOOKAHEAD]))
        maps.append(softmax_pv(i, *current))
        if c == 1:
            out = maps[0] - lam * maps[1]
            maps = []
            out = out * lax.rsqrt(jnp.mean(out * out, axis=0, keepdims=True) + SUBLN_EPS)
            o_ref[i * tq:(i + 1) * tq, :] = (out.T * sg_ref[...] * (1.0 - LAMBDA_INIT)).astype(o_ref.dtype)


def _attn(qt, k, vt, bias, rel_bias, lq1, lk1, lq2, lk2, subln_g, tq):
    bsz, seq, width = k.shape
    hd = width // DIFF_HEADS
    far_bucket = _far_bucket(tq, seq)
    col_spec = pl.BlockSpec((None, hd, seq), lambda b, h: (b, h, 0))
    row_spec = pl.BlockSpec((None, seq, hd), lambda b, h: (b, 0, h))
    return pl.pallas_call(
        functools.partial(_attn_kernel, far_bucket, tq),
        grid=(bsz, DIFF_HEADS),
        in_specs=[pl.BlockSpec(memory_space=pltpu.SMEM), col_spec, row_spec, col_spec,
                  pl.BlockSpec((None, 2 * tq, tq), lambda b, h: (h, 0, 0)),
                  _const_spec(lq1.shape), _const_spec(lk1.shape), _const_spec(lq2.shape),
                  _const_spec(lk2.shape), _const_spec(subln_g.shape)],
        out_specs=row_spec,
        out_shape=jax.ShapeDtypeStruct((bsz, seq, width), BF16),
        compiler_params=pltpu.CompilerParams(dimension_semantics=("arbitrary",) * 2,
                                             vmem_limit_bytes=VMEM_LIMIT),
        name="diffattn",
    )(rel_bias, qt, k, vt, bias, lq1, lk1, lq2, lk2, subln_g)


def _ffn_kernel(steps_per_seq, n_chunks, x_ref, a_ref, b_ref, wo_ref, g2_ref, wu_ref, cw_ref, cb_ref,
                wd_ref, gf_ref, o_ref, tail_ref):
    tm = x_ref.shape[0]
    half = a_ref.shape[1]
    d_ff = wd_ref.shape[0]
    fc = d_ff // n_chunks

    @pl.when(pl.program_id(0) % steps_per_seq == 0)
    def _():
        tail_ref[...] = jnp.zeros_like(tail_ref)

    def rows_dot(lhs, rhs):
        parts = [jnp.dot(lhs[r:r + MATMUL_ROWS], rhs, preferred_element_type=F32)
                 for r in range(0, tm, MATMUL_ROWS)]
        return jnp.concatenate(parts, axis=0)

    x1 = x_ref[...] + (rows_dot(a_ref[...], wo_ref[:half, :]) + rows_dot(b_ref[...], wo_ref[half:, :]))
    h2 = _rms(x1, g2_ref[...], EPS).astype(BF16)

    def up(c):
        gate = rows_dot(h2, wu_ref[:, c * fc:(c + 1) * fc])
        val = rows_dot(h2, wu_ref[:, d_ff + c * fc:d_ff + (c + 1) * fc])
        return gate, val

    def activate(c, gate, val):
        cols = slice(c * fc, (c + 1) * fc)
        ext = jnp.concatenate([tail_ref[:, cols], gate], axis=0)
        conv = cb_ref[:, cols] + cw_ref[FFN_CONV - 1:FFN_CONV, cols] * gate
        for s in range(1, FFN_CONV):
            conv = conv + cw_ref[FFN_CONV - 1 - s:FFN_CONV - s, cols] * _shift_rows(ext, s)
        tail_ref[:, cols] = gate[tm - SUBLANES:]
        return (_gelu_tanh(conv) * val).astype(BF16)

    ffn = None
    pending = up(0)
    for c in range(n_chunks):
        current = pending
        if c + 1 < n_chunks:
            pending = up(c + 1)
        act = activate(c, *current)
        down = rows_dot(act, wd_ref[c * fc:(c + 1) * fc, :])
        ffn = down if ffn is None else ffn + down
    o_ref[...] = _rms(x1 + ffn, gf_ref[...], EPS)


def _ffn(x2, out_a, out_b, w_out, g2, w_up, conv_w, conv_b, w_down, gf, seq, tm, n_chunks):
    n, d = x2.shape
    half = out_a.shape[1]
    d_ff = w_down.shape[0]
    return pl.pallas_call(
        functools.partial(_ffn_kernel, seq // tm, n_chunks),
        grid=(n // tm,),
        in_specs=[pl.BlockSpec((tm, d), lambda i: (i, 0)),
                  pl.BlockSpec((tm, half), lambda i: (i, 0)),
                  pl.BlockSpec((tm, half), lambda i: (i, 0)),
                  _const_spec(w_out.shape), _const_spec(g2.shape), _const_spec(w_up.shape),
                  _const_spec(conv_w.shape), _const_spec(conv_b.shape), _const_spec(w_down.shape),
                  _const_spec(gf.shape)],
        out_specs=pl.BlockSpec((tm, d), lambda i: (i, 0)),
        out_shape=jax.ShapeDtypeStruct((n, d), F32),
        scratch_shapes=[pltpu.VMEM((SUBLANES, d_ff), F32)],
        compiler_params=pltpu.CompilerParams(dimension_semantics=("arbitrary",), vmem_limit_bytes=VMEM_LIMIT),
        name="outproj_ffn",
    )(x2, out_a, out_b, w_out, g2, w_up, conv_w, conv_b, w_down, gf)


def _block_diag(w):
    nb, bd, _ = w.shape
    eye = jnp.eye(nb, dtype=w.dtype)
    return (eye[:, None, :, None] * w[:, :, None, :]).reshape(nb * bd, nb * bd)


def kernel(x, norm1_g, w_in, lru_conv_w, lru_conv_b, lru_wa, lru_ba, lru_wx, lru_bx, lru_lambda, diff_lq1, diff_lk1, diff_lq2, diff_lk2, diff_subln_g, rel_bias, w_out, norm2_g, ffn_w_up, ffn_conv_w, ffn_conv_b, ffn_w_down, final_norm_g):
    bsz, seq, d = x.shape
    assert w_in.shape[0] == 1, "single-layer block"
    n = bsz * seq
    x2 = x.reshape(n, d)
    row = lambda p: p.reshape(1, -1)

    lru_w = lru_wa.shape[1] * lru_wa.shape[2]
    attn_w = (w_in.shape[2] - 2 * lru_w) // 3
    qk_dim = attn_w // DIFF_HEADS // 2
    tq = 256
    w_in_b = w_in[0].astype(BF16)
    q_cols = w_in_b[:, 2 * lru_w:2 * lru_w + attn_w]
    k_cols = w_in_b[:, 2 * lru_w + attn_w:2 * lru_w + 2 * attn_w]
    v_cols = w_in_b[:, 2 * lru_w + 2 * attn_w:]
    w_rows = jnp.concatenate([w_in_b[:, :2 * lru_w], k_cols], axis=1)
    w_cols_t = jnp.concatenate([q_cols, v_cols], axis=1).T
    grp = LRU_BLOCKS // LRU_GATE_GROUPS
    w_gates = jnp.stack([
        jnp.concatenate([_block_diag(lru_wa[0, j * grp:(j + 1) * grp]),
                         _block_diag(lru_wx[0, j * grp:(j + 1) * grp])], axis=1)
        for j in range(LRU_GATE_GROUPS)]).astype(BF16)
    b_gates = jnp.stack([lru_ba[0].reshape(lru_w), lru_bx[0].reshape(lru_w)])
    out_a, k, qt, vt = _inproj(x2, row(norm1_g[0]), w_rows, w_cols_t, lru_conv_w[0], row(lru_conv_b[0]),
                               w_gates, b_gates, row(lru_lambda[0]), bsz, seq, tm=512, lru_tile=256,
                               qk_scale=qk_dim ** -0.5 * LOG2E)

    bias = _bias_tiles(rel_bias, tq)
    out_b = _attn(qt, k.reshape(bsz, seq, attn_w), vt, bias, rel_bias,
                  row(diff_lq1[0]), row(diff_lk1[0]), row(diff_lq2[0]), row(diff_lk2[0]),
                  row(diff_subln_g[0]), tq)

    out = _ffn(x2, out_a, out_b.reshape(n, -1), w_out[0].astype(BF16), row(norm2_g[0]),
               ffn_w_up[0].astype(BF16), ffn_conv_w[0], row(ffn_conv_b[0]), ffn_w_down[0].astype(BF16),
               row(final_norm_g), seq, tm=512, n_chunks=1)
    return out.reshape(bsz, seq, d)
```

```python
import functools
import math

import jax
import jax.numpy as jnp
import numpy as np
from jax import lax
from jax.experimental import pallas as pl
from jax.experimental.pallas import tpu as pltpu

CHUNK = 64
LRU_CONV = 4
LRU_C = 8.0
LRU_GATE_GROUPS = 2
DIFF_HEADS = 4
REL_BUCKETS = 32
REL_MAX_DIST = 128
FFN_CONV = 3
EPS = 1e-6
SUBLN_EPS = 1e-5
NEG_INF = -1e30
LAMBDA_INIT = 0.8 - 0.6 * math.exp(-0.3 * 0)

LOG2E = math.log2(math.e)
LOOKAHEAD = 4
MATMUL_ROWS = 256
KEY_BLOCK = 256

SUBLANES = 8
BF16_ROWS = 16
VMEM_LIMIT = 56 * 1024 * 1024

BF16 = jnp.bfloat16
F32 = jnp.float32


def _rms(x, g, eps):
    return x * lax.rsqrt(jnp.mean(x * x, axis=-1, keepdims=True) + eps) * g


def _gelu_tanh(x):
    inner = math.sqrt(2.0 / math.pi) * (x + 0.044715 * (x * x * x))
    return 0.5 * x * (1.0 + jnp.tanh(inner))


def _const_spec(shape):
    nd = len(shape)
    return pl.BlockSpec(shape, lambda *_: (0,) * nd, pipeline_mode=pl.Buffered(1))


def _shift_rows(ext, s):
    return pltpu.roll(ext, s, 0)[SUBLANES:]


def _segment_perm(tile):
    seg_len = tile // SUBLANES
    r = np.arange(tile)
    perm = np.zeros((tile, tile), np.float32)
    perm[r, (r % SUBLANES) * seg_len + r // SUBLANES] = 1.0
    return perm


def _lru_conv(lx, lg, p_ref, cw_ref, cb_ref, xtail_ref):
    tile, width = lx.shape
    npos = tile // SUBLANES
    ntap = LRU_CONV - 1
    sub = lax.broadcasted_iota(jnp.int32, (SUBLANES, width), 0)
    slab = lambda v, p: v[p * SUBLANES:(p + 1) * SUBLANES]
    x = jnp.dot(p_ref[...], lx, preferred_element_type=F32)
    gate = jnp.dot(p_ref[...], lg, preferred_element_type=F32)
    wrapped = []
    for q in range(ntap):
        cur = slab(x, npos - ntap + q)
        prev = slab(xtail_ref[...], q)
        wrapped.append(pltpu.roll(jnp.where(sub == SUBLANES - 1, prev, cur), 1, 0))
    xtail_ref[...] = x[tile - ntap * SUBLANES:]
    xc = cb_ref[...] + cw_ref[LRU_CONV - 1:LRU_CONV, :] * x
    for s in range(1, LRU_CONV):
        shifted = jnp.concatenate(wrapped[ntap - s:] + [x[:tile - SUBLANES * s]], axis=0)
        xc = xc + cw_ref[LRU_CONV - 1 - s:LRU_CONV - s, :] * shifted
    return xc, gate


def _lru_gates(xc, wg_ref, bg_ref):
    xcb = xc.astype(BF16)
    ngrp, gw, _ = wg_ref.shape
    pres = [jnp.dot(xcb[:, j * gw:(j + 1) * gw], wg_ref[j], preferred_element_type=F32) for j in range(ngrp)]
    pre_r = jnp.concatenate([pj[:, :gw] for pj in pres], axis=1) + bg_ref[0:1, :]
    pre_i = jnp.concatenate([pj[:, gw:] for pj in pres], axis=1) + bg_ref[1:2, :]
    return pre_r, pre_i


def _lru_scan(xc, gate, pre_r, pre_i, lam_ref, hend_ref):
    tile, width = xc.shape
    npos = tile // SUBLANES
    sub = lax.broadcasted_iota(jnp.int32, (SUBLANES, width), 0)
    slab = lambda v, p: v[p * SUBLANES:(p + 1) * SUBLANES]
    r = jax.nn.sigmoid(pre_r)
    gate_i = jax.nn.sigmoid(pre_i)
    z = -lam_ref[...]
    softplus = jnp.maximum(z, 0.0) + jnp.log(1.0 + jnp.exp(-jnp.abs(z)))
    a = jnp.exp((-LRU_C) * r * softplus)
    y = 1.0 - a * a
    u = jnp.where(y > 0.0, y * lax.rsqrt(y), 0.0) * (gate_i * xc)

    hs = [slab(u, 0)]
    cum = [slab(a, 0)]
    for p in range(1, npos):
        ap = slab(a, p)
        hs.append(ap * hs[-1] + slab(u, p))
        cum.append(ap * cum[-1])
    inc = pltpu.roll(jnp.where(sub == SUBLANES - 1, hend_ref[...], hs[-1]), 1, 0)
    dec = jnp.where(sub == 0, 0.0, pltpu.roll(cum[-1], 1, 0))
    d = 1
    while d < SUBLANES:
        keep = sub >= d
        inc = jnp.where(keep, dec * pltpu.roll(inc, d, 0) + inc, inc)
        dec = jnp.where(keep, dec * pltpu.roll(dec, d, 0), dec)
        d *= 2
    h = jnp.concatenate([cum[p] * inc + hs[p] for p in range(npos)], axis=0)
    hend_ref[...] = h[tile - SUBLANES:]
    return (_gelu_tanh(gate) * h).astype(BF16)


def _inproj_kernel(qk_scale, steps_per_seq, x_ref, g_ref, w_ref, wt_ref, p_ref, pt_ref, cw_ref, cb_ref, wg_ref,
                   bg_ref, lam_ref, oa_ref, k_ref, qt_ref, vt_ref, lx_ref, lg_ref, xtail_ref, hend_ref):
    i = pl.program_id(0)

    @pl.when(i == 0)
    def _():
        lx_ref[...] = jnp.zeros_like(lx_ref)
        lg_ref[...] = jnp.zeros_like(lg_ref)

    @pl.when((i + steps_per_seq - 1) % steps_per_seq == 0)
    def _():
        xtail_ref[...] = jnp.zeros_like(xtail_ref)
        hend_ref[...] = jnp.zeros_like(hend_ref)

    lx_prev = lx_ref[...]
    lg_prev = lg_ref[...]
    tm, width = oa_ref.shape
    tile = p_ref.shape[0]
    assert tm == 2 * tile, "the interleaving below is written for two scan tiles per step"
    rows = (slice(0, tile), slice(tile, tm))
    nt = (((1,), (1,)), ((), ()))
    unperm = lambda out: jnp.dot(pt_ref[...], out, preferred_element_type=F32).astype(oa_ref.dtype)

    h = _rms(x_ref[...], g_ref[...], EPS).astype(BF16)
    xc0, gate0 = _lru_conv(lx_prev[rows[0]], lg_prev[rows[0]], p_ref, cw_ref, cb_ref, xtail_ref)
    lx_ref[...] = jnp.dot(h, w_ref[:, :width], preferred_element_type=F32).astype(BF16)
    pre0 = _lru_gates(xc0, wg_ref, bg_ref)
    xc1, gate1 = _lru_conv(lx_prev[rows[1]], lg_prev[rows[1]], p_ref, cw_ref, cb_ref, xtail_ref)
    lg_ref[...] = jnp.dot(h, w_ref[:, width:2 * width], preferred_element_type=F32).astype(BF16)
    pre1 = _lru_gates(xc1, wg_ref, bg_ref)
    out0 = _lru_scan(xc0, gate0, *pre0, lam_ref, hend_ref)
    k_ref[...] = jnp.dot(h, w_ref[:, 3 * width:4 * width], preferred_element_type=F32).astype(k_ref.dtype)
    oa_ref[rows[0], :] = unperm(out0)
    out1 = _lru_scan(xc1, gate1, *pre1, lam_ref, hend_ref)
    qt = lax.dot_general(wt_ref[2 * width:3 * width, :], h, nt, preferred_element_type=F32)
    qt_ref[...] = (qt * qk_scale).astype(qt_ref.dtype)
    oa_ref[rows[1], :] = unperm(out1)
    vt = lax.dot_general(wt_ref[4 * width:, :], h, nt, preferred_element_type=F32)
    vt_ref[...] = vt.astype(vt_ref.dtype)


def _inproj(x2, g, w, w_t, conv_w, conv_b, w_gates, b_gates, lam, bsz, seq, tm, lru_tile, qk_scale):
    n, d = x2.shape
    width = w.shape[1] // 5
    spb = seq // tm
    last = n // tm - 1
    perm = _segment_perm(lru_tile)
    p_mat, pt_mat = jnp.asarray(perm, BF16), jnp.asarray(perm.T, BF16)
    consts = (g, w, w_t, p_mat, pt_mat, conv_w, conv_b, w_gates, b_gates, lam)
    proj = lambda i: jnp.minimum(i, last)
    col_spec = pl.BlockSpec((None, width, tm), lambda i: (proj(i) // spb, 0, proj(i) % spb))
    return pl.pallas_call(
        functools.partial(_inproj_kernel, qk_scale, spb),
        grid=(n // tm + 1,),
        in_specs=[pl.BlockSpec((tm, d), lambda i: (proj(i), 0))] + [_const_spec(c.shape) for c in consts],
        out_specs=[pl.BlockSpec((tm, width), lambda i: (jnp.maximum(i - 1, 0), 0)),
                   pl.BlockSpec((tm, width), lambda i: (proj(i), 0)), col_spec, col_spec],
        out_shape=[jax.ShapeDtypeStruct((n, width), BF16), jax.ShapeDtypeStruct((n, width), BF16),
                   jax.ShapeDtypeStruct((bsz, width, seq), BF16), jax.ShapeDtypeStruct((bsz, width, seq), BF16)],
        scratch_shapes=[pltpu.VMEM((tm, width), BF16), pltpu.VMEM((tm, width), BF16),
                        pltpu.VMEM(((LRU_CONV - 1) * SUBLANES, width), F32), pltpu.VMEM((SUBLANES, width), F32)],
        compiler_params=pltpu.CompilerParams(dimension_semantics=("arbitrary",), vmem_limit_bytes=VMEM_LIMIT),
        name="inproj_lru",
    )(x2, *consts)


def _rel_bucket_np(rel):
    half = REL_BUCKETS // 2
    max_exact = half // 2
    ret = (rel > 0).astype(np.int32) * half
    n = np.abs(rel)
    nf = np.maximum(n, 1).astype(np.float32)
    frac = np.log(nf / np.float32(max_exact)) / np.float32(math.log(REL_MAX_DIST / max_exact))
    large = max_exact + (frac * np.float32(half - max_exact)).astype(np.int32)
    large = np.minimum(large, half - 1)
    return ret + np.where(n < max_exact, n, large)


def _bucket_tile(tq):
    ql = np.arange(tq)[None, :]
    kpos = np.arange(2 * tq)[:, None] - tq
    bucket = _rel_bucket_np(kpos - ql).astype(np.int32)
    allowed = (kpos // CHUNK) <= (ql // CHUNK)
    return np.where(allowed, bucket, -1).astype(np.int32)


def _far_bucket(tq, seq):
    far = _rel_bucket_np(-np.arange(tq + 1, seq))
    assert (far == far[0]).all()
    return int(far[0])


def _bias_kernel(rb_ref, bucket_ref, o_ref):
    h = pl.program_id(0)
    bucket = bucket_ref[...]
    acc = jnp.zeros(bucket.shape, F32)
    for b in range(REL_BUCKETS):
        acc = jnp.where(bucket == b, rb_ref[b, h], acc)
    o_ref[...] = jnp.where(bucket < 0, NEG_INF, acc) * LOG2E


def _bias_tiles(rel_bias, tq):
    bucket = jnp.asarray(_bucket_tile(tq))
    tk2 = bucket.shape[0]
    return pl.pallas_call(
        _bias_kernel,
        grid=(DIFF_HEADS,),
        in_specs=[pl.BlockSpec(memory_space=pltpu.SMEM), pl.BlockSpec((tk2, tq), lambda h: (0, 0))],
        out_specs=pl.BlockSpec((None, tk2, tq), lambda h: (h, 0, 0)),
        out_shape=jax.ShapeDtypeStruct((DIFF_HEADS, tk2, tq), F32),
        name="relbias",
    )(rel_bias, bucket)


def _attn_kernel(far_bucket, tq, rb_ref, qt_ref, k_ref, vt_ref, bias_ref, lq1_ref, lk1_ref, lq2_ref, lk2_ref,
                 sg_ref, o_ref):
    h = pl.program_id(1)
    hd, seq = qt_ref.shape
    qk = hd // 2
    far_bias = rb_ref[far_bucket, h] * LOG2E
    lam = (jnp.exp(jnp.sum(lq1_ref[...] * lk1_ref[...], axis=1, keepdims=True))
           - jnp.exp(jnp.sum(lq2_ref[...] * lk2_ref[...], axis=1, keepdims=True)) + LAMBDA_INIT)
    sub = lax.broadcasted_iota(jnp.int32, (hd, tq), 0)
    near_lo = lambda i: max(i - 1, 0) * tq
    ones = jnp.ones((BF16_ROWS, seq), BF16)
    values = lambda lo, hi: jnp.concatenate([vt_ref[:, lo:hi], ones[:, lo:hi]], axis=0)

    def logits(i, c):
        qt = qt_ref[:, i * tq:(i + 1) * tq]
        qtc = jnp.where((sub < qk) if c == 0 else (sub >= qk), qt, jnp.zeros_like(qt))
        bias = bias_ref[...] if i > 0 else bias_ref[tq:, :]
        s_near = jnp.dot(k_ref[near_lo(i):(i + 1) * tq, :], qtc, preferred_element_type=F32) + bias
        s_far = jnp.dot(k_ref[:near_lo(i), :], qtc, preferred_element_type=F32) if near_lo(i) else None
        return s_near, s_far

    def softmax_pv(i, s_near, s_far):
        m = jnp.max(s_near, axis=0, keepdims=True)
        if s_far is not None:
            m = jnp.maximum(m, jnp.max(s_far, axis=0, keepdims=True) + far_bias)
        pv = None
        pieces = [(s_near, near_lo(i), m)] + ([(s_far, 0, m - far_bias)] if s_far is not None else [])
        for s, lo, shift in pieces:
            for r in range(0, s.shape[0], KEY_BLOCK):
                e = min(r + KEY_BLOCK, s.shape[0])
                p = jnp.exp2(s[r:e] - shift).astype(BF16)
                d = jnp.dot(values(lo + r, lo + e), p, preferred_element_type=F32)
                pv = d if pv is None else pv + d
        return pv[:hd] * (1.0 / pv[hd:hd + 1])

    units = [(i, c) for i in range(seq // tq) for c in range(2)]
    pending = [logits(*u) for u in units[:LOOKAHEAD]]
    maps = []
    for n, (i, c) in enumerate(units):
        current = pending.pop(0)
        if n + LOOKAHEAD < len(units):
            pending.append(logits(*units[n + LOOKAHEAD]))
        maps.append(softmax_pv(i, *current))
        if c == 1:
            out = maps[0] - lam * maps[1]
            maps = []
            out = out * lax.rsqrt(jnp.mean(out * out, axis=0, keepdims=True) + SUBLN_EPS)
            o_ref[i * tq:(i + 1) * tq, :] = (out.T * sg_ref[...] * (1.0 - LAMBDA_INIT)).astype(o_ref.dtype)


def _attn(qt, k, vt, bias, rel_bias, lq1, lk1, lq2, lk2, subln_g, tq):
    bsz, seq, width = k.shape
    hd = width // DIFF_HEADS
    far_bucket = _far_bucket(tq, seq)
    col_spec = pl.BlockSpec((None, hd, seq), lambda b, h: (b, h, 0))
    row_spec = pl.BlockSpec((None, seq, hd), lambda b, h: (b, 0, h))
    return pl.pallas_call(
        functools.partial(_attn_kernel, far_bucket, tq),
        grid=(bsz, DIFF_HEADS),
        in_specs=[pl.BlockSpec(memory_space=pltpu.SMEM), col_spec, row_spec, col_spec,
                  pl.BlockSpec((None, 2 * tq, tq), lambda b, h: (h, 0, 0)),
                  _const_spec(lq1.shape), _const_spec(lk1.shape), _const_spec(lq2.shape),
                  _const_spec(lk2.shape), _const_spec(subln_g.shape)],
        out_specs=row_spec,
        out_shape=jax.ShapeDtypeStruct((bsz, seq, width), BF16),
        compiler_params=pltpu.CompilerParams(dimension_semantics=("arbitrary",) * 2,
                                             vmem_limit_bytes=VMEM_LIMIT),
        name="diffattn",
    )(rel_bias, qt, k, vt, bias, lq1, lk1, lq2, lk2, subln_g)


def _ffn_kernel(steps_per_seq, n_chunks, x_ref, a_ref, b_ref, wo_ref, g2_ref, wu_ref, cw_ref, cb_ref,
                wd_ref, gf_ref, o_ref, tail_ref):
    tm = x_ref.shape[0]
    half = a_ref.shape[1]
    d_ff = wd_ref.shape[0]
    fc = d_ff // n_chunks

    @pl.when(pl.program_id(0) % steps_per_seq == 0)
    def _():
        tail_ref[...] = jnp.zeros_like(tail_ref)

    def rows_dot(lhs, rhs):
        parts = [jnp.dot(lhs[r:r + MATMUL_ROWS], rhs, preferred_element_type=F32)
                 for r in range(0, tm, MATMUL_ROWS)]
        return jnp.concatenate(parts, axis=0)

    x1 = x_ref[...] + (rows_dot(a_ref[...], wo_ref[:half, :]) + rows_dot(b_ref[...], wo_ref[half:, :]))
    h2 = _rms(x1, g2_ref[...], EPS).astype(BF16)

    def up(c):
        gate = rows_dot(h2, wu_ref[:, c * fc:(c + 1) * fc])
        val = rows_dot(h2, wu_ref[:, d_ff + c * fc:d_ff + (c + 1) * fc])
        return gate, val

    def activate(c, gate, val):
        cols = slice(c * fc, (c + 1) * fc)
        ext = jnp.concatenate([tail_ref[:, cols], gate], axis=0)
        conv = cb_ref[:, cols] + cw_ref[FFN_CONV - 1:FFN_CONV, cols] * gate
        for s in range(1, FFN_CONV):
            conv = conv + cw_ref[FFN_CONV - 1 - s:FFN_CONV - s, cols] * _shift_rows(ext, s)
        tail_ref[:, cols] = gate[tm - SUBLANES:]
        return (_gelu_tanh(conv) * val).astype(BF16)

    ffn = None
    pending = up(0)
    for c in range(n_chunks):
        current = pending
        if c + 1 < n_chunks:
            pending = up(c + 1)
        act = activate(c, *current)
        down = rows_dot(act, wd_ref[c * fc:(c + 1) * fc, :])
        ffn = down if ffn is None else ffn + down
    o_ref[...] = _rms(x1 + ffn, gf_ref[...], EPS)


def _ffn(x2, out_a, out_b, w_out, g2, w_up, conv_w, conv_b, w_down, gf, seq, tm, n_chunks):
    n, d = x2.shape
    half = out_a.shape[1]
    d_ff = w_down.shape[0]
    return pl.pallas_call(
        functools.partial(_ffn_kernel, seq // tm, n_chunks),
        grid=(n // tm,),
        in_specs=[pl.BlockSpec((tm, d), lambda i: (i, 0)),
                  pl.BlockSpec((tm, half), lambda i: (i, 0)),
                  pl.BlockSpec((tm, half), lambda i: (i, 0)),
                  _const_spec(w_out.shape), _const_spec(g2.shape), _const_spec(w_up.shape),
                  _const_spec(conv_w.shape), _const_spec(conv_b.shape), _const_spec(w_down.shape),
                  _const_spec(gf.shape)],
        out_specs=pl.BlockSpec((tm, d), lambda i: (i, 0)),
        out_shape=jax.ShapeDtypeStruct((n, d), F32),
        scratch_shapes=[pltpu.VMEM((SUBLANES, d_ff), F32)],
        compiler_params=pltpu.CompilerParams(dimension_semantics=("arbitrary",), vmem_limit_bytes=VMEM_LIMIT),
        name="outproj_ffn",
    )(x2, out_a, out_b, w_out, g2, w_up, conv_w, conv_b, w_down, gf)


def _gate_weights(wa, wx, groups):
    nb, bd, _ = wa.shape
    per = nb // groups
    w = jnp.stack([wa, wx]).reshape(2, groups, per, bd, bd)
    dense = jnp.einsum("tjbio,bc->jbitco", w, jnp.eye(per, dtype=w.dtype))
    return dense.reshape(groups, per * bd, 2 * per * bd)


def kernel(x, norm1_g, w_in, lru_conv_w, lru_conv_b, lru_wa, lru_ba, lru_wx, lru_bx, lru_lambda, diff_lq1, diff_lk1, diff_lq2, diff_lk2, diff_subln_g, rel_bias, w_out, norm2_g, ffn_w_up, ffn_conv_w, ffn_conv_b, ffn_w_down, final_norm_g):
    bsz, seq, d = x.shape
    assert w_in.shape[0] == 1, "single-layer block"
    n = bsz * seq
    x2 = x.reshape(n, d)
    row = lambda p: p.reshape(1, -1)

    lru_w = lru_wa.shape[1] * lru_wa.shape[2]
    attn_w = (w_in.shape[2] - 2 * lru_w) // 3
    qk_dim = attn_w // DIFF_HEADS // 2
    tq = 256
    assert attn_w == lru_w, "the projection kernel slices five equal column groups"
    w_gates = _gate_weights(lru_wa[0], lru_wx[0], LRU_GATE_GROUPS).astype(BF16)
    b_gates = jnp.stack([lru_ba[0].reshape(lru_w), lru_bx[0].reshape(lru_w)])
    out_a, k, qt, vt = _inproj(x2, row(norm1_g[0]), w_in[0].astype(BF16), w_in[0].T.astype(BF16),
                               lru_conv_w[0], row(lru_conv_b[0]),
                               w_gates, b_gates, row(lru_lambda[0]), bsz, seq, tm=512, lru_tile=256,
                               qk_scale=qk_dim ** -0.5 * LOG2E)

    bias = _bias_tiles(rel_bias, tq)
    out_b = _attn(qt, k.reshape(bsz, seq, attn_w), vt, bias, rel_bias,
                  row(diff_lq1[0]), row(diff_lk1[0]), row(diff_lq2[0]), row(diff_lk2[0]),
                  row(diff_subln_g[0]), tq)

    out = _ffn(x2, out_a, out_b.reshape(n, -1), w_out[0].astype(BF16), row(norm2_g[0]),
               ffn_w_up[0].astype(BF16), ffn_conv_w[0], row(ffn_conv_b[0]), ffn_w_down[0].astype(BF16),
               row(final_norm_g), seq, tm=512, n_chunks=1)
    return out.reshape(bsz, seq, d)
```

```python
import functools
import math

import jax
import jax.numpy as jnp
import numpy as np
from jax import lax
from jax.experimental import pallas as pl
from jax.experimental.pallas import tpu as pltpu

CHUNK = 64
LRU_CONV = 4
LRU_C = 8.0
LRU_GATE_GROUPS = 2
DIFF_HEADS = 4
REL_BUCKETS = 32
REL_MAX_DIST = 128
FFN_CONV = 3
EPS = 1e-6
SUBLN_EPS = 1e-5
NEG_INF = -1e30
LAMBDA_INIT = 0.8 - 0.6 * math.exp(-0.3 * 0)

LOG2E = math.log2(math.e)
LOOKAHEAD = 4
MATMUL_ROWS = 256
KEY_BLOCK = 256
HEADS_PER_STEP = 2

SUBLANES = 8
BF16_ROWS = 16
VMEM_LIMIT = 56 * 1024 * 1024

BF16 = jnp.bfloat16
F32 = jnp.float32


def _rms(x, g, eps):
    return x * lax.rsqrt(jnp.mean(x * x, axis=-1, keepdims=True) + eps) * g


def _gelu_tanh(x):
    inner = math.sqrt(2.0 / math.pi) * (x + 0.044715 * (x * x * x))
    return 0.5 * x * (1.0 + jnp.tanh(inner))


def _const_spec(shape):
    nd = len(shape)
    return pl.BlockSpec(shape, lambda *_: (0,) * nd, pipeline_mode=pl.Buffered(1))


def _shift_rows(ext, s):
    return pltpu.roll(ext, s, 0)[SUBLANES:]


def _segment_perm(tile):
    seg_len = tile // SUBLANES
    r = np.arange(tile)
    perm = np.zeros((tile, tile), np.float32)
    perm[r, (r % SUBLANES) * seg_len + r // SUBLANES] = 1.0
    return perm


def _lru_conv(lx, lg, p_ref, cw_ref, cb_ref, xtail_ref):
    tile, width = lx.shape
    npos = tile // SUBLANES
    ntap = LRU_CONV - 1
    sub = lax.broadcasted_iota(jnp.int32, (SUBLANES, width), 0)
    slab = lambda v, p: v[p * SUBLANES:(p + 1) * SUBLANES]
    x = jnp.dot(p_ref[...], lx, preferred_element_type=F32)
    gate = jnp.dot(p_ref[...], lg, preferred_element_type=F32)
    wrapped = []
    for q in range(ntap):
        cur = slab(x, npos - ntap + q)
        prev = slab(xtail_ref[...], q)
        wrapped.append(pltpu.roll(jnp.where(sub == SUBLANES - 1, prev, cur), 1, 0))
    xtail_ref[...] = x[tile - ntap * SUBLANES:]
    xc = cb_ref[...] + cw_ref[LRU_CONV - 1:LRU_CONV, :] * x
    for s in range(1, LRU_CONV):
        shifted = jnp.concatenate(wrapped[ntap - s:] + [x[:tile - SUBLANES * s]], axis=0)
        xc = xc + cw_ref[LRU_CONV - 1 - s:LRU_CONV - s, :] * shifted
    return xc, gate


def _lru_gates(xc, wg_ref, bg_ref):
    xcb = xc.astype(BF16)
    ngrp, gw, _ = wg_ref.shape
    pres = [jnp.dot(xcb[:, j * gw:(j + 1) * gw], wg_ref[j], preferred_element_type=F32) for j in range(ngrp)]
    pre_r = jnp.concatenate([pj[:, :gw] for pj in pres], axis=1) + bg_ref[0:1, :]
    pre_i = jnp.concatenate([pj[:, gw:] for pj in pres], axis=1) + bg_ref[1:2, :]
    return pre_r, pre_i


def _lru_scan(xc, gate, pre_r, pre_i, lam_ref, hend_ref):
    tile, width = xc.shape
    npos = tile // SUBLANES
    sub = lax.broadcasted_iota(jnp.int32, (SUBLANES, width), 0)
    slab = lambda v, p: v[p * SUBLANES:(p + 1) * SUBLANES]
    r = jax.nn.sigmoid(pre_r)
    gate_i = jax.nn.sigmoid(pre_i)
    z = -lam_ref[...]
    softplus = jnp.maximum(z, 0.0) + jnp.log(1.0 + jnp.exp(-jnp.abs(z)))
    a = jnp.exp((-LRU_C) * r * softplus)
    y = 1.0 - a * a
    u = jnp.where(y > 0.0, y * lax.rsqrt(y), 0.0) * (gate_i * xc)

    hs = [slab(u, 0)]
    cum = [slab(a, 0)]
    for p in range(1, npos):
        ap = slab(a, p)
        hs.append(ap * hs[-1] + slab(u, p))
        cum.append(ap * cum[-1])
    inc = pltpu.roll(jnp.where(sub == SUBLANES - 1, hend_ref[...], hs[-1]), 1, 0)
    dec = jnp.where(sub == 0, 0.0, pltpu.roll(cum[-1], 1, 0))
    d = 1
    while d < SUBLANES:
        keep = sub >= d
        inc = jnp.where(keep, dec * pltpu.roll(inc, d, 0) + inc, inc)
        dec = jnp.where(keep, dec * pltpu.roll(dec, d, 0), dec)
        d *= 2
    h = jnp.concatenate([cum[p] * inc + hs[p] for p in range(npos)], axis=0)
    hend_ref[...] = h[tile - SUBLANES:]
    return (_gelu_tanh(gate) * h).astype(BF16)


def _inproj_kernel(qk_scale, steps_per_seq, x_ref, g_ref, w_ref, wt_ref, p_ref, pt_ref, cw_ref, cb_ref, wg_ref,
                   bg_ref, lam_ref, oa_ref, k_ref, qt_ref, vt_ref, lx_ref, lg_ref, xtail_ref, hend_ref):
    i = pl.program_id(0)

    @pl.when(i == 0)
    def _():
        lx_ref[...] = jnp.zeros_like(lx_ref)
        lg_ref[...] = jnp.zeros_like(lg_ref)

    @pl.when((i + steps_per_seq - 1) % steps_per_seq == 0)
    def _():
        xtail_ref[...] = jnp.zeros_like(xtail_ref)
        hend_ref[...] = jnp.zeros_like(hend_ref)

    lx_prev = lx_ref[...]
    lg_prev = lg_ref[...]
    tm, width = oa_ref.shape
    tile = p_ref.shape[0]
    assert tm == 2 * tile, "the interleaving below is written for two scan tiles per step"
    rows = (slice(0, tile), slice(tile, tm))
    nt = (((1,), (1,)), ((), ()))
    unperm = lambda out: jnp.dot(pt_ref[...], out, preferred_element_type=F32).astype(oa_ref.dtype)

    h = _rms(x_ref[...], g_ref[...], EPS).astype(BF16)
    xc0, gate0 = _lru_conv(lx_prev[rows[0]], lg_prev[rows[0]], p_ref, cw_ref, cb_ref, xtail_ref)
    lx_ref[...] = jnp.dot(h, w_ref[:, :width], preferred_element_type=F32).astype(BF16)
    pre0 = _lru_gates(xc0, wg_ref, bg_ref)
    xc1, gate1 = _lru_conv(lx_prev[rows[1]], lg_prev[rows[1]], p_ref, cw_ref, cb_ref, xtail_ref)
    lg_ref[...] = jnp.dot(h, w_ref[:, width:2 * width], preferred_element_type=F32).astype(BF16)
    pre1 = _lru_gates(xc1, wg_ref, bg_ref)
    out0 = _lru_scan(xc0, gate0, *pre0, lam_ref, hend_ref)
    k_ref[...] = jnp.dot(h, w_ref[:, 3 * width:4 * width], preferred_element_type=F32).astype(k_ref.dtype)
    oa_ref[rows[0], :] = unperm(out0)
    out1 = _lru_scan(xc1, gate1, *pre1, lam_ref, hend_ref)
    qt = lax.dot_general(wt_ref[2 * width:3 * width, :], h, nt, preferred_element_type=F32)
    qt_ref[...] = (qt * qk_scale).astype(qt_ref.dtype)
    oa_ref[rows[1], :] = unperm(out1)
    vt = lax.dot_general(wt_ref[4 * width:, :], h, nt, preferred_element_type=F32)
    vt_ref[...] = vt.astype(vt_ref.dtype)


def _inproj(x2, g, w, w_t, conv_w, conv_b, w_gates, b_gates, lam, bsz, seq, tm, lru_tile, qk_scale):
    n, d = x2.shape
    width = w.shape[1] // 5
    spb = seq // tm
    last = n // tm - 1
    perm = _segment_perm(lru_tile)
    p_mat, pt_mat = jnp.asarray(perm, BF16), jnp.asarray(perm.T, BF16)
    consts = (g, w, w_t, p_mat, pt_mat, conv_w, conv_b, w_gates, b_gates, lam)
    proj = lambda i: jnp.minimum(i, last)
    col_spec = pl.BlockSpec((None, width, tm), lambda i: (proj(i) // spb, 0, proj(i) % spb))
    return pl.pallas_call(
        functools.partial(_inproj_kernel, qk_scale, spb),
        grid=(n // tm + 1,),
        in_specs=[pl.BlockSpec((tm, d), lambda i: (proj(i), 0))] + [_const_spec(c.shape) for c in consts],
        out_specs=[pl.BlockSpec((tm, width), lambda i: (jnp.maximum(i - 1, 0), 0)),
                   pl.BlockSpec((tm, width), lambda i: (proj(i), 0)), col_spec, col_spec],
        out_shape=[jax.ShapeDtypeStruct((n, width), BF16), jax.ShapeDtypeStruct((n, width), BF16),
                   jax.ShapeDtypeStruct((bsz, width, seq), BF16), jax.ShapeDtypeStruct((bsz, width, seq), BF16)],
        scratch_shapes=[pltpu.VMEM((tm, width), BF16), pltpu.VMEM((tm, width), BF16),
                        pltpu.VMEM(((LRU_CONV - 1) * SUBLANES, width), F32), pltpu.VMEM((SUBLANES, width), F32)],
        compiler_params=pltpu.CompilerParams(dimension_semantics=("arbitrary",), vmem_limit_bytes=VMEM_LIMIT),
        name="inproj_lru",
    )(x2, *consts)


def _rel_bucket_np(rel):
    half = REL_BUCKETS // 2
    max_exact = half // 2
    ret = (rel > 0).astype(np.int32) * half
    n = np.abs(rel)
    nf = np.maximum(n, 1).astype(np.float32)
    frac = np.log(nf / np.float32(max_exact)) / np.float32(math.log(REL_MAX_DIST / max_exact))
    large = max_exact + (frac * np.float32(half - max_exact)).astype(np.int32)
    large = np.minimum(large, half - 1)
    return ret + np.where(n < max_exact, n, large)


def _bucket_tile(tq):
    ql = np.arange(tq)[None, :]
    kpos = np.arange(2 * tq)[:, None] - tq
    bucket = _rel_bucket_np(kpos - ql).astype(np.int32)
    allowed = (kpos // CHUNK) <= (ql // CHUNK)
    return np.where(allowed, bucket, -1).astype(np.int32)


def _far_bucket(tq, seq):
    far = _rel_bucket_np(-np.arange(tq + 1, seq))
    assert (far == far[0]).all()
    return int(far[0])


def _bias_kernel(rb_ref, bucket_ref, o_ref):
    h = pl.program_id(0)
    bucket = bucket_ref[...]
    acc = jnp.zeros(bucket.shape, F32)
    for b in range(REL_BUCKETS):
        acc = jnp.where(bucket == b, rb_ref[b, h], acc)
    o_ref[...] = jnp.where(bucket < 0, NEG_INF, acc) * LOG2E


def _bias_tiles(rel_bias, tq):
    bucket = jnp.asarray(_bucket_tile(tq))
    tk2 = bucket.shape[0]
    return pl.pallas_call(
        _bias_kernel,
        grid=(DIFF_HEADS,),
        in_specs=[pl.BlockSpec(memory_space=pltpu.SMEM), pl.BlockSpec((tk2, tq), lambda h: (0, 0))],
        out_specs=pl.BlockSpec((None, tk2, tq), lambda h: (h, 0, 0)),
        out_shape=jax.ShapeDtypeStruct((DIFF_HEADS, tk2, tq), F32),
        name="relbias",
    )(rel_bias, bucket)


def _attn_kernel(far_bucket, tq, rb_ref, qt_ref, k_ref, vt_ref, bias_ref, lq1_ref, lk1_ref, lq2_ref, lk2_ref,
                 sg_ref, o_ref):
    seq = qt_ref.shape[1]
    hd = qt_ref.shape[0] // HEADS_PER_STEP
    qk = hd // 2
    head0 = pl.program_id(1) * HEADS_PER_STEP
    far_bias = [rb_ref[far_bucket, head0 + hh] * LOG2E for hh in range(HEADS_PER_STEP)]
    lam = (jnp.exp(jnp.sum(lq1_ref[...] * lk1_ref[...], axis=1, keepdims=True))
           - jnp.exp(jnp.sum(lq2_ref[...] * lk2_ref[...], axis=1, keepdims=True)) + LAMBDA_INIT)
    sub = lax.broadcasted_iota(jnp.int32, (hd, tq), 0)
    near_lo = lambda i: max(i - 1, 0) * tq
    ones = jnp.ones((BF16_ROWS, seq), BF16)
    head = lambda hh: slice(hh * hd, (hh + 1) * hd)
    values = lambda hh, lo, hi: jnp.concatenate([vt_ref[head(hh), lo:hi], ones[:, lo:hi]], axis=0)

    def logits(hh, i, c):
        qt = qt_ref[head(hh), i * tq:(i + 1) * tq]
        qtc = jnp.where((sub < qk) if c == 0 else (sub >= qk), qt, jnp.zeros_like(qt))
        bias = bias_ref[hh] if i > 0 else bias_ref[hh, tq:, :]
        s_near = jnp.dot(k_ref[near_lo(i):(i + 1) * tq, head(hh)], qtc, preferred_element_type=F32) + bias
        s_far = (jnp.dot(k_ref[:near_lo(i), head(hh)], qtc, preferred_element_type=F32)
                 if near_lo(i) else None)
        return s_near, s_far

    def softmax_pv(hh, i, s_near, s_far):
        m = jnp.max(s_near, axis=0, keepdims=True)
        if s_far is not None:
            m = jnp.maximum(m, jnp.max(s_far, axis=0, keepdims=True) + far_bias[hh])
        pv = None
        pieces = [(s_near, near_lo(i), m)] + ([(s_far, 0, m - far_bias[hh])] if s_far is not None else [])
        for s, lo, shift in pieces:
            for r in range(0, s.shape[0], KEY_BLOCK):
                e = min(r + KEY_BLOCK, s.shape[0])
                p = jnp.exp2(s[r:e] - shift).astype(BF16)
                d = jnp.dot(values(hh, lo + r, lo + e), p, preferred_element_type=F32)
                pv = d if pv is None else pv + d
        return pv[:hd] * (1.0 / pv[hd:hd + 1])

    units = [(hh, i, c) for hh in range(HEADS_PER_STEP) for i in range(seq // tq) for c in range(2)]
    pending = [logits(*u) for u in units[:LOOKAHEAD]]
    maps = []
    for n, (hh, i, c) in enumerate(units):
        current = pending.pop(0)
        if n + LOOKAHEAD < len(units):
            pending.append(logits(*units[n + LOOKAHEAD]))
        maps.append(softmax_pv(hh, i, *current))
        if c == 1:
            out = maps[0] - lam * maps[1]
            maps = []
            out = out * lax.rsqrt(jnp.mean(out * out, axis=0, keepdims=True) + SUBLN_EPS)
            out = (out.T * sg_ref[...] * (1.0 - LAMBDA_INIT)).astype(o_ref.dtype)
            o_ref[i * tq:(i + 1) * tq, head(hh)] = out


def _attn(qt, k, vt, bias, rel_bias, lq1, lk1, lq2, lk2, subln_g, tq):
    bsz, seq, width = k.shape
    hw = width // DIFF_HEADS * HEADS_PER_STEP
    far_bucket = _far_bucket(tq, seq)
    col_spec = pl.BlockSpec((None, hw, seq), lambda b, g: (b, g, 0))
    row_spec = pl.BlockSpec((None, seq, hw), lambda b, g: (b, 0, g))
    return pl.pallas_call(
        functools.partial(_attn_kernel, far_bucket, tq),
        grid=(bsz, DIFF_HEADS // HEADS_PER_STEP),
        in_specs=[pl.BlockSpec(memory_space=pltpu.SMEM), col_spec, row_spec, col_spec,
                  pl.BlockSpec((HEADS_PER_STEP, 2 * tq, tq), lambda b, g: (g, 0, 0)),
                  _const_spec(lq1.shape), _const_spec(lk1.shape), _const_spec(lq2.shape),
                  _const_spec(lk2.shape), _const_spec(subln_g.shape)],
        out_specs=row_spec,
        out_shape=jax.ShapeDtypeStruct((bsz, seq, width), BF16),
        compiler_params=pltpu.CompilerParams(dimension_semantics=("arbitrary",) * 2,
                                             vmem_limit_bytes=VMEM_LIMIT),
        name="diffattn",
    )(rel_bias, qt, k, vt, bias, lq1, lk1, lq2, lk2, subln_g)


def _ffn_kernel(steps_per_seq, n_chunks, x_ref, a_ref, b_ref, wo_ref, g2_ref, wu_ref, cw_ref, cb_ref,
                wd_ref, gf_ref, o_ref, tail_ref):
    tm = x_ref.shape[0]
    half = a_ref.shape[1]
    d_ff = wd_ref.shape[0]
    fc = d_ff // n_chunks

    @pl.when(pl.program_id(0) % steps_per_seq == 0)
    def _():
        tail_ref[...] = jnp.zeros_like(tail_ref)

    def rows_dot(lhs, rhs):
        parts = [jnp.dot(lhs[r:r + MATMUL_ROWS], rhs, preferred_element_type=F32)
                 for r in range(0, tm, MATMUL_ROWS)]
        return jnp.concatenate(parts, axis=0)

    x1 = x_ref[...] + (rows_dot(a_ref[...], wo_ref[:half, :]) + rows_dot(b_ref[...], wo_ref[half:, :]))
    h2 = _rms(x1, g2_ref[...], EPS).astype(BF16)

    def up(c):
        gate = rows_dot(h2, wu_ref[:, c * fc:(c + 1) * fc])
        val = rows_dot(h2, wu_ref[:, d_ff + c * fc:d_ff + (c + 1) * fc])
        return gate, val

    def activate(c, gate, val):
        cols = slice(c * fc, (c + 1) * fc)
        ext = jnp.concatenate([tail_ref[:, cols], gate], axis=0)
        conv = cb_ref[:, cols] + cw_ref[FFN_CONV - 1:FFN_CONV, cols] * gate
        for s in range(1, FFN_CONV):
            conv = conv + cw_ref[FFN_CONV - 1 - s:FFN_CONV - s, cols] * _shift_rows(ext, s)
        tail_ref[:, cols] = gate[tm - SUBLANES:]
        return (_gelu_tanh(conv) * val).astype(BF16)

    ffn = None
    pending = up(0)
    for c in range(n_chunks):
        current = pending
        if c + 1 < n_chunks:
            pending = up(c + 1)
        act = activate(c, *current)
        down = rows_dot(act, wd_ref[c * fc:(c + 1) * fc, :])
        ffn = down if ffn is None else ffn + down
    o_ref[...] = _rms(x1 + ffn, gf_ref[...], EPS)


def _ffn(x2, out_a, out_b, w_out, g2, w_up, conv_w, conv_b, w_down, gf, seq, tm, n_chunks):
    n, d = x2.shape
    half = out_a.shape[1]
    d_ff = w_down.shape[0]
    return pl.pallas_call(
        functools.partial(_ffn_kernel, seq // tm, n_chunks),
        grid=(n // tm,),
        in_specs=[pl.BlockSpec((tm, d), lambda i: (i, 0)),
                  pl.BlockSpec((tm, half), lambda i: (i, 0)),
                  pl.BlockSpec((tm, half), lambda i: (i, 0)),
                  _const_spec(w_out.shape), _const_spec(g2.shape), _const_spec(w_up.shape),
                  _const_spec(conv_w.shape), _const_spec(conv_b.shape), _const_spec(w_down.shape),
                  _const_spec(gf.shape)],
        out_specs=pl.BlockSpec((tm, d), lambda i: (i, 0)),
        out_shape=jax.ShapeDtypeStruct((n, d), F32),
        scratch_shapes=[pltpu.VMEM((SUBLANES, d_ff), F32)],
        compiler_params=pltpu.CompilerParams(dimension_semantics=("arbitrary",), vmem_limit_bytes=VMEM_LIMIT),
        name="outproj_ffn",
    )(x2, out_a, out_b, w_out, g2, w_up, conv_w, conv_b, w_down, gf)


def _gate_weights(wa, wx, groups):
    nb, bd, _ = wa.shape
    per = nb // groups
    w = jnp.stack([wa, wx]).reshape(2, groups, per, bd, bd)
    dense = jnp.einsum("tjbio,bc->jbitco", w, jnp.eye(per, dtype=w.dtype))
    return dense.reshape(groups, per * bd, 2 * per * bd)


def kernel(x, norm1_g, w_in, lru_conv_w, lru_conv_b, lru_wa, lru_ba, lru_wx, lru_bx, lru_lambda, diff_lq1, diff_lk1, diff_lq2, diff_lk2, diff_subln_g, rel_bias, w_out, norm2_g, ffn_w_up, ffn_conv_w, ffn_conv_b, ffn_w_down, final_norm_g):
    bsz, seq, d = x.shape
    assert w_in.shape[0] == 1, "single-layer block"
    n = bsz * seq
    x2 = x.reshape(n, d)
    row = lambda p: p.reshape(1, -1)

    lru_w = lru_wa.shape[1] * lru_wa.shape[2]
    attn_w = (w_in.shape[2] - 2 * lru_w) // 3
    qk_dim = attn_w // DIFF_HEADS // 2
    tq = 256
    assert attn_w == lru_w, "the projection kernel slices five equal column groups"
    w_gates = _gate_weights(lru_wa[0], lru_wx[0], LRU_GATE_GROUPS).astype(BF16)
    b_gates = jnp.stack([lru_ba[0].reshape(lru_w), lru_bx[0].reshape(lru_w)])
    out_a, k, qt, vt = _inproj(x2, row(norm1_g[0]), w_in[0].astype(BF16), w_in[0].T.astype(BF16),
                               lru_conv_w[0], row(lru_conv_b[0]),
                               w_gates, b_gates, row(lru_lambda[0]), bsz, seq, tm=512, lru_tile=256,
                               qk_scale=qk_dim ** -0.5 * LOG2E)

    bias = _bias_tiles(rel_bias, tq)
    out_b = _attn(qt, k.reshape(bsz, seq, attn_w), vt, bias, rel_bias,
                  row(diff_lq1[0]), row(diff_lk1[0]), row(diff_lq2[0]), row(diff_lk2[0]),
                  row(diff_subln_g[0]), tq)

    out = _ffn(x2, out_a, out_b.reshape(n, -1), w_out[0].astype(BF16), row(norm2_g[0]),
               ffn_w_up[0].astype(BF16), ffn_conv_w[0], row(ffn_conv_b[0]), ffn_w_down[0].astype(BF16),
               row(final_norm_g), seq, tm=512, n_chunks=1)
    return out.reshape(bsz, seq, d)
```

```python
import functools
import math

import jax
import jax.numpy as jnp
import numpy as np
from jax import lax
from jax.experimental import pallas as pl
from jax.experimental.pallas import tpu as pltpu

CHUNK = 64
LRU_CONV = 4
LRU_C = 8.0
LRU_GATE_GROUPS = 2
DIFF_HEADS = 4
REL_BUCKETS = 32
REL_MAX_DIST = 128
FFN_CONV = 3
EPS = 1e-6
SUBLN_EPS = 1e-5
NEG_INF = -1e30
LAMBDA_INIT = 0.8 - 0.6 * math.exp(-0.3 * 0)

LOG2E = math.log2(math.e)
LOOKAHEAD = 6
MATMUL_ROWS = 256
KEY_BLOCK = 256

SUBLANES = 8
BF16_ROWS = 16
VMEM_LIMIT = 56 * 1024 * 1024

BF16 = jnp.bfloat16
F32 = jnp.float32


def _rms(x, g, eps):
    return x * lax.rsqrt(jnp.mean(x * x, axis=-1, keepdims=True) + eps) * g


def _gelu_tanh(x):
    inner = math.sqrt(2.0 / math.pi) * (x + 0.044715 * (x * x * x))
    return 0.5 * x * (1.0 + jnp.tanh(inner))


def _const_spec(shape):
    nd = len(shape)
    return pl.BlockSpec(shape, lambda *_: (0,) * nd, pipeline_mode=pl.Buffered(1))


def _shift_rows(ext, s):
    return pltpu.roll(ext, s, 0)[SUBLANES:]


def _segment_perm(tile):
    seg_len = tile // SUBLANES
    r = np.arange(tile)
    perm = np.zeros((tile, tile), np.float32)
    perm[r, (r % SUBLANES) * seg_len + r // SUBLANES] = 1.0
    return perm


def _lru_conv(lx, lg, p_ref, cw_ref, cb_ref, xtail_ref):
    tile, width = lx.shape
    npos = tile // SUBLANES
    ntap = LRU_CONV - 1
    sub = lax.broadcasted_iota(jnp.int32, (SUBLANES, width), 0)
    slab = lambda v, p: v[p * SUBLANES:(p + 1) * SUBLANES]
    x = jnp.dot(p_ref[...], lx, preferred_element_type=F32)
    gate = jnp.dot(p_ref[...], lg, preferred_element_type=F32)
    wrapped = []
    for q in range(ntap):
        cur = slab(x, npos - ntap + q)
        prev = slab(xtail_ref[...], q)
        wrapped.append(pltpu.roll(jnp.where(sub == SUBLANES - 1, prev, cur), 1, 0))
    xtail_ref[...] = x[tile - ntap * SUBLANES:]
    xc = cb_ref[...] + cw_ref[LRU_CONV - 1:LRU_CONV, :] * x
    for s in range(1, LRU_CONV):
        shifted = jnp.concatenate(wrapped[ntap - s:] + [x[:tile - SUBLANES * s]], axis=0)
        xc = xc + cw_ref[LRU_CONV - 1 - s:LRU_CONV - s, :] * shifted
    return xc, gate


def _lru_gates(xc, wg_ref, bg_ref):
    xcb = xc.astype(BF16)
    ngrp, gw, _ = wg_ref.shape
    pres = [jnp.dot(xcb[:, j * gw:(j + 1) * gw], wg_ref[j], preferred_element_type=F32) for j in range(ngrp)]
    pre_r = jnp.concatenate([pj[:, :gw] for pj in pres], axis=1) + bg_ref[0:1, :]
    pre_i = jnp.concatenate([pj[:, gw:] for pj in pres], axis=1) + bg_ref[1:2, :]
    return pre_r, pre_i


def _lru_scan(xc, gate, pre_r, pre_i, lam_ref, hend_ref):
    tile, width = xc.shape
    npos = tile // SUBLANES
    sub = lax.broadcasted_iota(jnp.int32, (SUBLANES, width), 0)
    slab = lambda v, p: v[p * SUBLANES:(p + 1) * SUBLANES]
    r = jax.nn.sigmoid(pre_r)
    gate_i = jax.nn.sigmoid(pre_i)
    z = -lam_ref[...]
    softplus = jnp.maximum(z, 0.0) + jnp.log(1.0 + jnp.exp(-jnp.abs(z)))
    a = jnp.exp((-LRU_C) * r * softplus)
    y = 1.0 - a * a
    u = jnp.where(y > 0.0, y * lax.rsqrt(y), 0.0) * (gate_i * xc)

    hs = [slab(u, 0)]
    cum = [slab(a, 0)]
    for p in range(1, npos):
        ap = slab(a, p)
        hs.append(ap * hs[-1] + slab(u, p))
        cum.append(ap * cum[-1])
    inc = pltpu.roll(jnp.where(sub == SUBLANES - 1, hend_ref[...], hs[-1]), 1, 0)
    dec = jnp.where(sub == 0, 0.0, pltpu.roll(cum[-1], 1, 0))
    d = 1
    while d < SUBLANES:
        keep = sub >= d
        inc = jnp.where(keep, dec * pltpu.roll(inc, d, 0) + inc, inc)
        dec = jnp.where(keep, dec * pltpu.roll(dec, d, 0), dec)
        d *= 2
    h = jnp.concatenate([cum[p] * inc + hs[p] for p in range(npos)], axis=0)
    hend_ref[...] = h[tile - SUBLANES:]
    return (_gelu_tanh(gate) * h).astype(BF16)


def _inproj_kernel(qk_scale, steps_per_seq, x_ref, g_ref, w_ref, wt_ref, p_ref, pt_ref, cw_ref, cb_ref, wg_ref,
                   bg_ref, lam_ref, oa_ref, k_ref, qt_ref, vt_ref, lx_ref, lg_ref, xtail_ref, hend_ref):
    i = pl.program_id(0)

    @pl.when(i == 0)
    def _():
        lx_ref[...] = jnp.zeros_like(lx_ref)
        lg_ref[...] = jnp.zeros_like(lg_ref)

    @pl.when((i + steps_per_seq - 1) % steps_per_seq == 0)
    def _():
        xtail_ref[...] = jnp.zeros_like(xtail_ref)
        hend_ref[...] = jnp.zeros_like(hend_ref)

    lx_prev = lx_ref[...]
    lg_prev = lg_ref[...]
    tm, width = oa_ref.shape
    tile = p_ref.shape[0]
    assert tm == 2 * tile, "the interleaving below is written for two scan tiles per step"
    rows = (slice(0, tile), slice(tile, tm))
    nt = (((1,), (1,)), ((), ()))
    unperm = lambda out: jnp.dot(pt_ref[...], out, preferred_element_type=F32).astype(oa_ref.dtype)

    h = _rms(x_ref[...], g_ref[...], EPS).astype(BF16)
    xc0, gate0 = _lru_conv(lx_prev[rows[0]], lg_prev[rows[0]], p_ref, cw_ref, cb_ref, xtail_ref)
    lx_ref[...] = jnp.dot(h, w_ref[:, :width], preferred_element_type=F32).astype(BF16)
    pre0 = _lru_gates(xc0, wg_ref, bg_ref)
    xc1, gate1 = _lru_conv(lx_prev[rows[1]], lg_prev[rows[1]], p_ref, cw_ref, cb_ref, xtail_ref)
    lg_ref[...] = jnp.dot(h, w_ref[:, width:2 * width], preferred_element_type=F32).astype(BF16)
    pre1 = _lru_gates(xc1, wg_ref, bg_ref)
    out0 = _lru_scan(xc0, gate0, *pre0, lam_ref, hend_ref)
    k_ref[...] = jnp.dot(h, w_ref[:, 3 * width:4 * width], preferred_element_type=F32).astype(k_ref.dtype)
    oa_ref[rows[0], :] = unperm(out0)
    out1 = _lru_scan(xc1, gate1, *pre1, lam_ref, hend_ref)
    qt = lax.dot_general(wt_ref[2 * width:3 * width, :], h, nt, preferred_element_type=F32)
    qt_ref[...] = (qt * qk_scale).astype(qt_ref.dtype)
    oa_ref[rows[1], :] = unperm(out1)
    vt = lax.dot_general(wt_ref[4 * width:, :], h, nt, preferred_element_type=F32)
    vt_ref[...] = vt.astype(vt_ref.dtype)


def _inproj(x2, g, w, w_t, conv_w, conv_b, w_gates, b_gates, lam, bsz, seq, tm, lru_tile, qk_scale):
    n, d = x2.shape
    width = w.shape[1] // 5
    spb = seq // tm
    last = n // tm - 1
    perm = _segment_perm(lru_tile)
    p_mat, pt_mat = jnp.asarray(perm, BF16), jnp.asarray(perm.T, BF16)
    consts = (g, w, w_t, p_mat, pt_mat, conv_w, conv_b, w_gates, b_gates, lam)
    proj = lambda i: jnp.minimum(i, last)
    col_spec = pl.BlockSpec((None, width, tm), lambda i: (proj(i) // spb, 0, proj(i) % spb))
    return pl.pallas_call(
        functools.partial(_inproj_kernel, qk_scale, spb),
        grid=(n // tm + 1,),
        in_specs=[pl.BlockSpec((tm, d), lambda i: (proj(i), 0))] + [_const_spec(c.shape) for c in consts],
        out_specs=[pl.BlockSpec((tm, width), lambda i: (jnp.maximum(i - 1, 0), 0)),
                   pl.BlockSpec((tm, width), lambda i: (proj(i), 0)), col_spec, col_spec],
        out_shape=[jax.ShapeDtypeStruct((n, width), BF16), jax.ShapeDtypeStruct((n, width), BF16),
                   jax.ShapeDtypeStruct((bsz, width, seq), BF16), jax.ShapeDtypeStruct((bsz, width, seq), BF16)],
        scratch_shapes=[pltpu.VMEM((tm, width), BF16), pltpu.VMEM((tm, width), BF16),
                        pltpu.VMEM(((LRU_CONV - 1) * SUBLANES, width), F32), pltpu.VMEM((SUBLANES, width), F32)],
        compiler_params=pltpu.CompilerParams(dimension_semantics=("arbitrary",), vmem_limit_bytes=VMEM_LIMIT),
        name="inproj_lru",
    )(x2, *consts)


def _rel_bucket_np(rel):
    half = REL_BUCKETS // 2
    max_exact = half // 2
    ret = (rel > 0).astype(np.int32) * half
    n = np.abs(rel)
    nf = np.maximum(n, 1).astype(np.float32)
    frac = np.log(nf / np.float32(max_exact)) / np.float32(math.log(REL_MAX_DIST / max_exact))
    large = max_exact + (frac * np.float32(half - max_exact)).astype(np.int32)
    large = np.minimum(large, half - 1)
    return ret + np.where(n < max_exact, n, large)


def _bucket_tile(tq):
    ql = np.arange(tq)[None, :]
    kpos = np.arange(2 * tq)[:, None] - tq
    bucket = _rel_bucket_np(kpos - ql).astype(np.int32)
    allowed = (kpos // CHUNK) <= (ql // CHUNK)
    tile = np.where(allowed, bucket, -1).astype(np.int32)
    return np.concatenate([tile, tile], axis=1)


def _far_bucket(tq, seq):
    far = _rel_bucket_np(-np.arange(tq + 1, seq))
    assert (far == far[0]).all()
    return int(far[0])


def _bias_kernel(rb_ref, bucket_ref, o_ref):
    h = pl.program_id(0)
    bucket = bucket_ref[...]
    acc = jnp.zeros(bucket.shape, F32)
    for b in range(REL_BUCKETS):
        acc = jnp.where(bucket == b, rb_ref[b, h], acc)
    o_ref[...] = jnp.where(bucket < 0, NEG_INF, acc) * LOG2E


def _bias_tiles(rel_bias, tq):
    bucket = jnp.asarray(_bucket_tile(tq))
    tk2, lanes = bucket.shape
    return pl.pallas_call(
        _bias_kernel,
        grid=(DIFF_HEADS,),
        in_specs=[pl.BlockSpec(memory_space=pltpu.SMEM), pl.BlockSpec((tk2, lanes), lambda h: (0, 0))],
        out_specs=pl.BlockSpec((None, tk2, lanes), lambda h: (h, 0, 0)),
        out_shape=jax.ShapeDtypeStruct((DIFF_HEADS, tk2, lanes), F32),
        name="relbias",
    )(rel_bias, bucket)


def _attn_kernel(far_bucket, tq, rb_ref, qt_ref, k_ref, vt_ref, bias_ref, lq1_ref, lk1_ref, lq2_ref, lk2_ref,
                 sg_ref, o_ref):
    h = pl.program_id(1)
    hd, seq = qt_ref.shape
    qk = hd // 2
    far_bias = rb_ref[far_bucket, h] * LOG2E
    lam = (jnp.exp(jnp.sum(lq1_ref[...] * lk1_ref[...], axis=1, keepdims=True))
           - jnp.exp(jnp.sum(lq2_ref[...] * lk2_ref[...], axis=1, keepdims=True)) + LAMBDA_INIT)
    sub = lax.broadcasted_iota(jnp.int32, (hd, 2 * tq), 0)
    lane = lax.broadcasted_iota(jnp.int32, (hd, 2 * tq), 1)
    own_rows = (sub < qk) == (lane < tq)
    near_lo = lambda i: max(i - 1, 0) * tq
    ones = jnp.ones((BF16_ROWS, seq), BF16)
    values = lambda lo, hi: jnp.concatenate([vt_ref[:, lo:hi], ones[:, lo:hi]], axis=0)

    def logits(i):
        qt = qt_ref[:, i * tq:(i + 1) * tq]
        qt = jnp.concatenate([qt, qt], axis=1)
        qtc = jnp.where(own_rows, qt, jnp.zeros_like(qt))
        bias = bias_ref[...] if i > 0 else bias_ref[tq:, :]
        s_near = jnp.dot(k_ref[near_lo(i):(i + 1) * tq, :], qtc, preferred_element_type=F32) + bias
        s_far = jnp.dot(k_ref[:near_lo(i), :], qtc, preferred_element_type=F32) if near_lo(i) else None
        return s_near, s_far

    def softmax_pv(i, s_near, s_far):
        m = jnp.max(s_near, axis=0, keepdims=True)
        if s_far is not None:
            m = jnp.maximum(m, jnp.max(s_far, axis=0, keepdims=True) + far_bias)
        pv = None
        pieces = [(s_near, near_lo(i), m)] + ([(s_far, 0, m - far_bias)] if s_far is not None else [])
        for s, lo, shift in pieces:
            for r in range(0, s.shape[0], KEY_BLOCK):
                e = min(r + KEY_BLOCK, s.shape[0])
                p = jnp.exp2(s[r:e] - shift).astype(BF16)
                d = jnp.dot(values(lo + r, lo + e), p, preferred_element_type=F32)
                pv = d if pv is None else pv + d
        return pv[:hd] * (1.0 / pv[hd:hd + 1])

    nblk = seq // tq
    pending = [logits(i) for i in range(min(LOOKAHEAD, nblk))]
    for i in range(nblk):
        current = pending.pop(0)
        if i + LOOKAHEAD < nblk:
            pending.append(logits(i + LOOKAHEAD))
        maps = softmax_pv(i, *current)
        out = maps[:, :tq] - lam * maps[:, tq:]
        out = out * lax.rsqrt(jnp.mean(out * out, axis=0, keepdims=True) + SUBLN_EPS)
        o_ref[i * tq:(i + 1) * tq, :] = (out.T * sg_ref[...] * (1.0 - LAMBDA_INIT)).astype(o_ref.dtype)


def _attn(qt, k, vt, bias, rel_bias, lq1, lk1, lq2, lk2, subln_g, tq):
    bsz, seq, width = k.shape
    hd = width // DIFF_HEADS
    far_bucket = _far_bucket(tq, seq)
    col_spec = pl.BlockSpec((None, hd, seq), lambda b, h: (b, h, 0))
    row_spec = pl.BlockSpec((None, seq, hd), lambda b, h: (b, 0, h))
    return pl.pallas_call(
        functools.partial(_attn_kernel, far_bucket, tq),
        grid=(bsz, DIFF_HEADS),
        in_specs=[pl.BlockSpec(memory_space=pltpu.SMEM), col_spec, row_spec, col_spec,
                  pl.BlockSpec((None, 2 * tq, 2 * tq), lambda b, h: (h, 0, 0)),
                  _const_spec(lq1.shape), _const_spec(lk1.shape), _const_spec(lq2.shape),
                  _const_spec(lk2.shape), _const_spec(subln_g.shape)],
        out_specs=row_spec,
        out_shape=jax.ShapeDtypeStruct((bsz, seq, width), BF16),
        compiler_params=pltpu.CompilerParams(dimension_semantics=("arbitrary",) * 2,
                                             vmem_limit_bytes=VMEM_LIMIT),
        name="diffattn",
    )(rel_bias, qt, k, vt, bias, lq1, lk1, lq2, lk2, subln_g)


def _ffn_kernel(steps_per_seq, n_chunks, x_ref, a_ref, b_ref, wo_ref, g2_ref, wu_ref, cw_ref, cb_ref,
                wd_ref, gf_ref, o_ref, tail_ref):
    tm = x_ref.shape[0]
    half = a_ref.shape[1]
    d_ff = wd_ref.shape[0]
    fc = d_ff // n_chunks

    @pl.when(pl.program_id(0) % steps_per_seq == 0)
    def _():
        tail_ref[...] = jnp.zeros_like(tail_ref)

    def rows_dot(lhs, rhs):
        parts = [jnp.dot(lhs[r:r + MATMUL_ROWS], rhs, preferred_element_type=F32)
                 for r in range(0, tm, MATMUL_ROWS)]
        return jnp.concatenate(parts, axis=0)

    x1 = x_ref[...] + (rows_dot(a_ref[...], wo_ref[:half, :]) + rows_dot(b_ref[...], wo_ref[half:, :]))
    h2 = _rms(x1, g2_ref[...], EPS).astype(BF16)

    def up(c):
        gate = rows_dot(h2, wu_ref[:, c * fc:(c + 1) * fc])
        val = rows_dot(h2, wu_ref[:, d_ff + c * fc:d_ff + (c + 1) * fc])
        return gate, val

    def activate(c, gate, val):
        cols = slice(c * fc, (c + 1) * fc)
        ext = jnp.concatenate([tail_ref[:, cols], gate], axis=0)
        conv = cb_ref[:, cols] + cw_ref[FFN_CONV - 1:FFN_CONV, cols] * gate
        for s in range(1, FFN_CONV):
            conv = conv + cw_ref[FFN_CONV - 1 - s:FFN_CONV - s, cols] * _shift_rows(ext, s)
        tail_ref[:, cols] = gate[tm - SUBLANES:]
        return (_gelu_tanh(conv) * val).astype(BF16)

    ffn = None
    pending = up(0)
    for c in range(n_chunks):
        current = pending
        if c + 1 < n_chunks:
            pending = up(c + 1)
        act = activate(c, *current)
        down = rows_dot(act, wd_ref[c * fc:(c + 1) * fc, :])
        ffn = down if ffn is None else ffn + down
    o_ref[...] = _rms(x1 + ffn, gf_ref[...], EPS)


def _ffn(x2, out_a, out_b, w_out, g2, w_up, conv_w, conv_b, w_down, gf, seq, tm, n_chunks):
    n, d = x2.shape
    half = out_a.shape[1]
    d_ff = w_down.shape[0]
    return pl.pallas_call(
        functools.partial(_ffn_kernel, seq // tm, n_chunks),
        grid=(n // tm,),
        in_specs=[pl.BlockSpec((tm, d), lambda i: (i, 0)),
                  pl.BlockSpec((tm, half), lambda i: (i, 0)),
                  pl.BlockSpec((tm, half), lambda i: (i, 0)),
                  _const_spec(w_out.shape), _const_spec(g2.shape), _const_spec(w_up.shape),
                  _const_spec(conv_w.shape), _const_spec(conv_b.shape), _const_spec(w_down.shape),
                  _const_spec(gf.shape)],
        out_specs=pl.BlockSpec((tm, d), lambda i: (i, 0)),
        out_shape=jax.ShapeDtypeStruct((n, d), F32),
        scratch_shapes=[pltpu.VMEM((SUBLANES, d_ff), F32)],
        compiler_params=pltpu.CompilerParams(dimension_semantics=("arbitrary",), vmem_limit_bytes=VMEM_LIMIT),
        name="outproj_ffn",
    )(x2, out_a, out_b, w_out, g2, w_up, conv_w, conv_b, w_down, gf)


def _gate_weights(wa, wx, groups):
    nb, bd, _ = wa.shape
    per = nb // groups
    w = jnp.stack([wa, wx]).reshape(2, groups, per, bd, bd)
    dense = jnp.einsum("tjbio,bc->jbitco", w, jnp.eye(per, dtype=w.dtype))
    return dense.reshape(groups, per * bd, 2 * per * bd)


def kernel(x, norm1_g, w_in, lru_conv_w, lru_conv_b, lru_wa, lru_ba, lru_wx, lru_bx, lru_lambda, diff_lq1, diff_lk1, diff_lq2, diff_lk2, diff_subln_g, rel_bias, w_out, norm2_g, ffn_w_up, ffn_conv_w, ffn_conv_b, ffn_w_down, final_norm_g):
    bsz, seq, d = x.shape
    assert w_in.shape[0] == 1, "single-layer block"
    n = bsz * seq
    x2 = x.reshape(n, d)
    row = lambda p: p.reshape(1, -1)

    lru_w = lru_wa.shape[1] * lru_wa.shape[2]
    attn_w = (w_in.shape[2] - 2 * lru_w) // 3
    qk_dim = attn_w // DIFF_HEADS // 2
    tq = 128
    assert attn_w == lru_w, "the projection kernel slices five equal column groups"
    w_gates = _gate_weights(lru_wa[0], lru_wx[0], LRU_GATE_GROUPS).astype(BF16)
    b_gates = jnp.stack([lru_ba[0].reshape(lru_w), lru_bx[0].reshape(lru_w)])
    out_a, k, qt, vt = _inproj(x2, row(norm1_g[0]), w_in[0].astype(BF16), w_in[0].T.astype(BF16),
                               lru_conv_w[0], row(lru_conv_b[0]),
                               w_gates, b_gates, row(lru_lambda[0]), bsz, seq, tm=512, lru_tile=256,
                               qk_scale=qk_dim ** -0.5 * LOG2E)

    bias = _bias_tiles(rel_bias, tq)
    out_b = _attn(qt, k.reshape(bsz, seq, attn_w), vt, bias, rel_bias,
                  row(diff_lq1[0]), row(diff_lk1[0]), row(diff_lq2[0]), row(diff_lk2[0]),
                  row(diff_subln_g[0]), tq)

    out = _ffn(x2, out_a, out_b.reshape(n, -1), w_out[0].astype(BF16), row(norm2_g[0]),
               ffn_w_up[0].astype(BF16), ffn_conv_w[0], row(ffn_conv_b[0]), ffn_w_down[0].astype(BF16),
               row(final_norm_g), seq, tm=512, n_chunks=1)
    return out.reshape(bsz, seq, d)
```

```python
import functools
import math

import jax
import jax.numpy as jnp
import numpy as np
from jax import lax
from jax.experimental import pallas as pl
from jax.experimental.pallas import tpu as pltpu

CHUNK = 64
LRU_CONV = 4
LRU_C = 8.0
LRU_GATE_GROUPS = 2
DIFF_HEADS = 4
REL_BUCKETS = 32
REL_MAX_DIST = 128
FFN_CONV = 3
EPS = 1e-6
SUBLN_EPS = 1e-5
NEG_INF = -1e30
LAMBDA_INIT = 0.8 - 0.6 * math.exp(-0.3 * 0)

LOG2E = math.log2(math.e)
LOOKAHEAD = 6
MATMUL_ROWS = 256
KEY_BLOCK = 256

SUBLANES = 8
BF16_ROWS = 16
VMEM_LIMIT = 56 * 1024 * 1024

BF16 = jnp.bfloat16
F32 = jnp.float32


def _rms(x, g, eps):
    return x * lax.rsqrt(jnp.mean(x * x, axis=-1, keepdims=True) + eps) * g


def _gelu_tanh(x):
    inner = math.sqrt(2.0 / math.pi) * (x + 0.044715 * (x * x * x))
    return 0.5 * x * (1.0 + jnp.tanh(inner))


def _const_spec(shape):
    nd = len(shape)
    return pl.BlockSpec(shape, lambda *_: (0,) * nd, pipeline_mode=pl.Buffered(1))


def _shift_rows(ext, s):
    return pltpu.roll(ext, s, 0)[SUBLANES:]


def _segment_perm(tile):
    seg_len = tile // SUBLANES
    r = np.arange(tile)
    perm = np.zeros((tile, tile), np.float32)
    perm[r, (r % SUBLANES) * seg_len + r // SUBLANES] = 1.0
    return perm


def _lru_conv(lx, lg, p_ref, cw_ref, cb_ref, xtail_ref):
    tile, width = lx.shape
    npos = tile // SUBLANES
    ntap = LRU_CONV - 1
    sub = lax.broadcasted_iota(jnp.int32, (SUBLANES, width), 0)
    slab = lambda v, p: v[p * SUBLANES:(p + 1) * SUBLANES]
    x = jnp.dot(p_ref[...], lx, preferred_element_type=F32)
    gate = jnp.dot(p_ref[...], lg, preferred_element_type=F32)
    wrapped = []
    for q in range(ntap):
        cur = slab(x, npos - ntap + q)
        prev = slab(xtail_ref[...], q)
        wrapped.append(pltpu.roll(jnp.where(sub == SUBLANES - 1, prev, cur), 1, 0))
    xtail_ref[...] = x[tile - ntap * SUBLANES:]
    xc = cb_ref[...] + cw_ref[LRU_CONV - 1:LRU_CONV, :] * x
    for s in range(1, LRU_CONV):
        shifted = jnp.concatenate(wrapped[ntap - s:] + [x[:tile - SUBLANES * s]], axis=0)
        xc = xc + cw_ref[LRU_CONV - 1 - s:LRU_CONV - s, :] * shifted
    return xc, gate


def _lru_gates(xc, wg_ref, bg_ref):
    xcb = xc.astype(BF16)
    ngrp, gw, _ = wg_ref.shape
    pres = [jnp.dot(xcb[:, j * gw:(j + 1) * gw], wg_ref[j], preferred_element_type=F32) for j in range(ngrp)]
    pre_r = jnp.concatenate([pj[:, :gw] for pj in pres], axis=1) + bg_ref[0:1, :]
    pre_i = jnp.concatenate([pj[:, gw:] for pj in pres], axis=1) + bg_ref[1:2, :]
    return pre_r, pre_i


def _lru_scan(xc, gate, pre_r, pre_i, lam_ref, hend_ref):
    tile, width = xc.shape
    npos = tile // SUBLANES
    sub = lax.broadcasted_iota(jnp.int32, (SUBLANES, width), 0)
    slab = lambda v, p: v[p * SUBLANES:(p + 1) * SUBLANES]
    r = jax.nn.sigmoid(pre_r)
    gate_i = jax.nn.sigmoid(pre_i)
    z = -lam_ref[...]
    softplus = jnp.maximum(z, 0.0) + jnp.log(1.0 + jnp.exp(-jnp.abs(z)))
    a = jnp.exp((-LRU_C) * r * softplus)
    y = 1.0 - a * a
    u = jnp.where(y > 0.0, y * lax.rsqrt(y), 0.0) * (gate_i * xc)

    hs = [slab(u, 0)]
    cum = [slab(a, 0)]
    for p in range(1, npos):
        ap = slab(a, p)
        hs.append(ap * hs[-1] + slab(u, p))
        cum.append(ap * cum[-1])
    inc = pltpu.roll(jnp.where(sub == SUBLANES - 1, hend_ref[...], hs[-1]), 1, 0)
    dec = jnp.where(sub == 0, 0.0, pltpu.roll(cum[-1], 1, 0))
    d = 1
    while d < SUBLANES:
        keep = sub >= d
        inc = jnp.where(keep, dec * pltpu.roll(inc, d, 0) + inc, inc)
        dec = jnp.where(keep, dec * pltpu.roll(dec, d, 0), dec)
        d *= 2
    h = jnp.concatenate([cum[p] * inc + hs[p] for p in range(npos)], axis=0)
    hend_ref[...] = h[tile - SUBLANES:]
    return (_gelu_tanh(gate) * h).astype(BF16)


def _inproj_kernel(qk_scale, steps_per_seq, x_ref, g_ref, w_ref, wt_ref, p_ref, pt_ref, cw_ref, cb_ref, wg_ref,
                   bg_ref, lam_ref, oa_ref, k_ref, qt_ref, vt_ref, lx_ref, lg_ref, xtail_ref, hend_ref):
    i = pl.program_id(0)

    @pl.when(i == 0)
    def _():
        lx_ref[...] = jnp.zeros_like(lx_ref)
        lg_ref[...] = jnp.zeros_like(lg_ref)

    @pl.when((i + steps_per_seq - 1) % steps_per_seq == 0)
    def _():
        xtail_ref[...] = jnp.zeros_like(xtail_ref)
        hend_ref[...] = jnp.zeros_like(hend_ref)

    lx_prev = lx_ref[...]
    lg_prev = lg_ref[...]
    tm, width = oa_ref.shape
    tile = p_ref.shape[0]
    assert tm == 2 * tile, "the interleaving below is written for two scan tiles per step"
    rows = (slice(0, tile), slice(tile, tm))
    nt = (((1,), (1,)), ((), ()))
    unperm = lambda out: jnp.dot(pt_ref[...], out, preferred_element_type=F32).astype(oa_ref.dtype)

    h = _rms(x_ref[...], g_ref[...], EPS).astype(BF16)
    xc0, gate0 = _lru_conv(lx_prev[rows[0]], lg_prev[rows[0]], p_ref, cw_ref, cb_ref, xtail_ref)
    lx_ref[...] = jnp.dot(h, w_ref[:, :width], preferred_element_type=F32).astype(BF16)
    pre0 = _lru_gates(xc0, wg_ref, bg_ref)
    xc1, gate1 = _lru_conv(lx_prev[rows[1]], lg_prev[rows[1]], p_ref, cw_ref, cb_ref, xtail_ref)
    lg_ref[...] = jnp.dot(h, w_ref[:, width:2 * width], preferred_element_type=F32).astype(BF16)
    pre1 = _lru_gates(xc1, wg_ref, bg_ref)
    out0 = _lru_scan(xc0, gate0, *pre0, lam_ref, hend_ref)
    k_ref[...] = jnp.dot(h, w_ref[:, 3 * width:4 * width], preferred_element_type=F32).astype(k_ref.dtype)
    oa_ref[rows[0], :] = unperm(out0)
    out1 = _lru_scan(xc1, gate1, *pre1, lam_ref, hend_ref)
    qt = lax.dot_general(wt_ref[2 * width:3 * width, :], h, nt, preferred_element_type=F32)
    qt_ref[...] = (qt * qk_scale).astype(qt_ref.dtype)
    oa_ref[rows[1], :] = unperm(out1)
    vt = lax.dot_general(wt_ref[4 * width:, :], h, nt, preferred_element_type=F32)
    vt_ref[...] = vt.astype(vt_ref.dtype)


def _inproj(x2, g, w, w_t, conv_w, conv_b, w_gates, b_gates, lam, bsz, seq, tm, lru_tile, qk_scale):
    n, d = x2.shape
    width = w.shape[1] // 5
    spb = seq // tm
    last = n // tm - 1
    perm = _segment_perm(lru_tile)
    p_mat, pt_mat = jnp.asarray(perm, BF16), jnp.asarray(perm.T, BF16)
    consts = (g, w, w_t, p_mat, pt_mat, conv_w, conv_b, w_gates, b_gates, lam)
    proj = lambda i: jnp.minimum(i, last)
    col_spec = pl.BlockSpec((None, width, tm), lambda i: (proj(i) // spb, 0, proj(i) % spb))
    return pl.pallas_call(
        functools.partial(_inproj_kernel, qk_scale, spb),
        grid=(n // tm + 1,),
        in_specs=[pl.BlockSpec((tm, d), lambda i: (proj(i), 0))] + [_const_spec(c.shape) for c in consts],
        out_specs=[pl.BlockSpec((tm, width), lambda i: (jnp.maximum(i - 1, 0), 0)),
                   pl.BlockSpec((tm, width), lambda i: (proj(i), 0)), col_spec, col_spec],
        out_shape=[jax.ShapeDtypeStruct((n, width), BF16), jax.ShapeDtypeStruct((n, width), BF16),
                   jax.ShapeDtypeStruct((bsz, width, seq), BF16), jax.ShapeDtypeStruct((bsz, width, seq), BF16)],
        scratch_shapes=[pltpu.VMEM((tm, width), BF16), pltpu.VMEM((tm, width), BF16),
                        pltpu.VMEM(((LRU_CONV - 1) * SUBLANES, width), F32), pltpu.VMEM((SUBLANES, width), F32)],
        compiler_params=pltpu.CompilerParams(dimension_semantics=("arbitrary",), vmem_limit_bytes=VMEM_LIMIT),
        name="inproj_lru",
    )(x2, *consts)


def _rel_bucket_np(rel):
    half = REL_BUCKETS // 2
    max_exact = half // 2
    ret = (rel > 0).astype(np.int32) * half
    n = np.abs(rel)
    nf = np.maximum(n, 1).astype(np.float32)
    frac = np.log(nf / np.float32(max_exact)) / np.float32(math.log(REL_MAX_DIST / max_exact))
    large = max_exact + (frac * np.float32(half - max_exact)).astype(np.int32)
    large = np.minimum(large, half - 1)
    return ret + np.where(n < max_exact, n, large)


def _bucket_tile(tq):
    ql = np.arange(tq)[None, :]
    kpos = np.arange(2 * tq)[:, None] - tq
    bucket = _rel_bucket_np(kpos - ql).astype(np.int32)
    allowed = (kpos // CHUNK) <= (ql // CHUNK)
    tile = np.where(allowed, bucket, -1).astype(np.int32)
    return np.concatenate([tile, tile], axis=1)


def _far_bucket(tq, seq):
    far = _rel_bucket_np(-np.arange(tq + 1, seq))
    assert (far == far[0]).all()
    return int(far[0])


def _bias_kernel(rb_ref, bucket_ref, o_ref):
    h = pl.program_id(0)
    bucket = bucket_ref[...]
    acc = jnp.zeros(bucket.shape, F32)
    for b in range(REL_BUCKETS):
        acc = jnp.where(bucket == b, rb_ref[b, h], acc)
    o_ref[...] = jnp.where(bucket < 0, NEG_INF, acc) * LOG2E


def _bias_tiles(rel_bias, tq):
    bucket = jnp.asarray(_bucket_tile(tq))
    tk2, lanes = bucket.shape
    return pl.pallas_call(
        _bias_kernel,
        grid=(DIFF_HEADS,),
        in_specs=[pl.BlockSpec(memory_space=pltpu.SMEM), pl.BlockSpec((tk2, lanes), lambda h: (0, 0))],
        out_specs=pl.BlockSpec((None, tk2, lanes), lambda h: (h, 0, 0)),
        out_shape=jax.ShapeDtypeStruct((DIFF_HEADS, tk2, lanes), F32),
        name="relbias",
    )(rel_bias, bucket)


def _attn_kernel(far_bucket, tq, rb_ref, qt_ref, k_ref, vt_ref, bias_ref, lq1_ref, lk1_ref, lq2_ref, lk2_ref,
                 sg_ref, o_ref):
    h = pl.program_id(1)
    hd, seq = qt_ref.shape
    qk = hd // 2
    far_bias = rb_ref[far_bucket, h] * LOG2E
    lam = (jnp.exp(jnp.sum(lq1_ref[...] * lk1_ref[...], axis=1, keepdims=True))
           - jnp.exp(jnp.sum(lq2_ref[...] * lk2_ref[...], axis=1, keepdims=True)) + LAMBDA_INIT)
    sub = lax.broadcasted_iota(jnp.int32, (hd, 2 * tq), 0)
    lane = lax.broadcasted_iota(jnp.int32, (hd, 2 * tq), 1)
    own_rows = (sub < qk) == (lane < tq)
    near_lo = lambda i: max(i - 1, 0) * tq
    ones = jnp.ones((BF16_ROWS, seq), BF16)
    values = lambda lo, hi: jnp.concatenate([vt_ref[:, lo:hi], ones[:, lo:hi]], axis=0)

    def logits(i):
        qt = qt_ref[:, i * tq:(i + 1) * tq]
        qt = jnp.concatenate([qt, qt], axis=1)
        qtc = jnp.where(own_rows, qt, jnp.zeros_like(qt))
        bias = bias_ref[...] if i > 0 else bias_ref[tq:, :]
        s_near = jnp.dot(k_ref[near_lo(i):(i + 1) * tq, :], qtc, preferred_element_type=F32) + bias
        s_far = jnp.dot(k_ref[:near_lo(i), :], qtc, preferred_element_type=F32) if near_lo(i) else None
        return s_near, s_far

    def softmax_pv(i, s_near, s_far):
        m = jnp.max(s_near, axis=0, keepdims=True)
        if s_far is not None:
            m = jnp.maximum(m, jnp.max(s_far, axis=0, keepdims=True) + far_bias)
        pv = None
        pieces = [(s_near, near_lo(i), m)] + ([(s_far, 0, m - far_bias)] if s_far is not None else [])
        for s, lo, shift in pieces:
            for r in range(0, s.shape[0], KEY_BLOCK):
                e = min(r + KEY_BLOCK, s.shape[0])
                p = jnp.exp2((s[r:e] - shift).astype(BF16))
                d = jnp.dot(values(lo + r, lo + e), p, preferred_element_type=F32)
                pv = d if pv is None else pv + d
        return pv[:hd] * (1.0 / pv[hd:hd + 1])

    nblk = seq // tq
    pending = [logits(i) for i in range(min(LOOKAHEAD, nblk))]
    for i in range(nblk):
        current = pending.pop(0)
        if i + LOOKAHEAD < nblk:
            pending.append(logits(i + LOOKAHEAD))
        maps = softmax_pv(i, *current)
        out = maps[:, :tq] - lam * maps[:, tq:]
        out = out * lax.rsqrt(jnp.mean(out * out, axis=0, keepdims=True) + SUBLN_EPS)
        o_ref[i * tq:(i + 1) * tq, :] = (out.T * sg_ref[...] * (1.0 - LAMBDA_INIT)).astype(o_ref.dtype)


def _attn(qt, k, vt, bias, rel_bias, lq1, lk1, lq2, lk2, subln_g, tq):
    bsz, seq, width = k.shape
    hd = width // DIFF_HEADS
    far_bucket = _far_bucket(tq, seq)
    col_spec = pl.BlockSpec((None, hd, seq), lambda b, h: (b, h, 0))
    row_spec = pl.BlockSpec((None, seq, hd), lambda b, h: (b, 0, h))
    return pl.pallas_call(
        functools.partial(_attn_kernel, far_bucket, tq),
        grid=(bsz, DIFF_HEADS),
        in_specs=[pl.BlockSpec(memory_space=pltpu.SMEM), col_spec, row_spec, col_spec,
                  pl.BlockSpec((None, 2 * tq, 2 * tq), lambda b, h: (h, 0, 0)),
                  _const_spec(lq1.shape), _const_spec(lk1.shape), _const_spec(lq2.shape),
                  _const_spec(lk2.shape), _const_spec(subln_g.shape)],
        out_specs=row_spec,
        out_shape=jax.ShapeDtypeStruct((bsz, seq, width), BF16),
        compiler_params=pltpu.CompilerParams(dimension_semantics=("arbitrary",) * 2,
                                             vmem_limit_bytes=VMEM_LIMIT),
        name="diffattn",
    )(rel_bias, qt, k, vt, bias, lq1, lk1, lq2, lk2, subln_g)


def _ffn_kernel(steps_per_seq, n_chunks, x_ref, a_ref, b_ref, wo_ref, g2_ref, wu_ref, cw_ref, cb_ref,
                wd_ref, gf_ref, o_ref, tail_ref):
    tm = x_ref.shape[0]
    half = a_ref.shape[1]
    d_ff = wd_ref.shape[0]
    fc = d_ff // n_chunks

    @pl.when(pl.program_id(0) % steps_per_seq == 0)
    def _():
        tail_ref[...] = jnp.zeros_like(tail_ref)

    def rows_dot(lhs, rhs):
        parts = [jnp.dot(lhs[r:r + MATMUL_ROWS], rhs, preferred_element_type=F32)
                 for r in range(0, tm, MATMUL_ROWS)]
        return jnp.concatenate(parts, axis=0)

    x1 = x_ref[...] + (rows_dot(a_ref[...], wo_ref[:half, :]) + rows_dot(b_ref[...], wo_ref[half:, :]))
    h2 = _rms(x1, g2_ref[...], EPS).astype(BF16)

    def up(c):
        gate = rows_dot(h2, wu_ref[:, c * fc:(c + 1) * fc])
        val = rows_dot(h2, wu_ref[:, d_ff + c * fc:d_ff + (c + 1) * fc])
        return gate, val

    def activate(c, gate, val):
        cols = slice(c * fc, (c + 1) * fc)
        ext = jnp.concatenate([tail_ref[:, cols], gate], axis=0)
        conv = cb_ref[:, cols] + cw_ref[FFN_CONV - 1:FFN_CONV, cols] * gate
        for s in range(1, FFN_CONV):
            conv = conv + cw_ref[FFN_CONV - 1 - s:FFN_CONV - s, cols] * _shift_rows(ext, s)
        tail_ref[:, cols] = gate[tm - SUBLANES:]
        return (_gelu_tanh(conv) * val).astype(BF16)

    ffn = None
    pending = up(0)
    for c in range(n_chunks):
        current = pending
        if c + 1 < n_chunks:
            pending = up(c + 1)
        act = activate(c, *current)
        down = rows_dot(act, wd_ref[c * fc:(c + 1) * fc, :])
        ffn = down if ffn is None else ffn + down
    o_ref[...] = _rms(x1 + ffn, gf_ref[...], EPS)


def _ffn(x2, out_a, out_b, w_out, g2, w_up, conv_w, conv_b, w_down, gf, seq, tm, n_chunks):
    n, d = x2.shape
    half = out_a.shape[1]
    d_ff = w_down.shape[0]
    return pl.pallas_call(
        functools.partial(_ffn_kernel, seq // tm, n_chunks),
        grid=(n // tm,),
        in_specs=[pl.BlockSpec((tm, d), lambda i: (i, 0)),
                  pl.BlockSpec((tm, half), lambda i: (i, 0)),
                  pl.BlockSpec((tm, half), lambda i: (i, 0)),
                  _const_spec(w_out.shape), _const_spec(g2.shape), _const_spec(w_up.shape),
                  _const_spec(conv_w.shape), _const_spec(conv_b.shape), _const_spec(w_down.shape),
                  _const_spec(gf.shape)],
        out_specs=pl.BlockSpec((tm, d), lambda i: (i, 0)),
        out_shape=jax.ShapeDtypeStruct((n, d), F32),
        scratch_shapes=[pltpu.VMEM((SUBLANES, d_ff), F32)],
        compiler_params=pltpu.CompilerParams(dimension_semantics=("arbitrary",), vmem_limit_bytes=VMEM_LIMIT),
        name="outproj_ffn",
    )(x2, out_a, out_b, w_out, g2, w_up, conv_w, conv_b, w_down, gf)


def _gate_weights(wa, wx, groups):
    nb, bd, _ = wa.shape
    per = nb // groups
    w = jnp.stack([wa, wx]).reshape(2, groups, per, bd, bd)
    dense = jnp.einsum("tjbio,bc->jbitco", w, jnp.eye(per, dtype=w.dtype))
    return dense.reshape(groups, per * bd, 2 * per * bd)


def kernel(x, norm1_g, w_in, lru_conv_w, lru_conv_b, lru_wa, lru_ba, lru_wx, lru_bx, lru_lambda, diff_lq1, diff_lk1, diff_lq2, diff_lk2, diff_subln_g, rel_bias, w_out, norm2_g, ffn_w_up, ffn_conv_w, ffn_conv_b, ffn_w_down, final_norm_g):
    bsz, seq, d = x.shape
    assert w_in.shape[0] == 1, "single-layer block"
    n = bsz * seq
    x2 = x.reshape(n, d)
    row = lambda p: p.reshape(1, -1)

    lru_w = lru_wa.shape[1] * lru_wa.shape[2]
    attn_w = (w_in.shape[2] - 2 * lru_w) // 3
    qk_dim = attn_w // DIFF_HEADS // 2
    tq = 128
    assert attn_w == lru_w, "the projection kernel slices five equal column groups"
    w_gates = _gate_weights(lru_wa[0], lru_wx[0], LRU_GATE_GROUPS).astype(BF16)
    b_gates = jnp.stack([lru_ba[0].reshape(lru_w), lru_bx[0].reshape(lru_w)])
    out_a, k, qt, vt = _inproj(x2, row(norm1_g[0]), w_in[0].astype(BF16), w_in[0].T.astype(BF16),
                               lru_conv_w[0], row(lru_conv_b[0]),
                               w_gates, b_gates, row(lru_lambda[0]), bsz, seq, tm=512, lru_tile=256,
                               qk_scale=qk_dim ** -0.5 * LOG2E)

    bias = _bias_tiles(rel_bias, tq)
    out_b = _attn(qt, k.reshape(bsz, seq, attn_w), vt, bias, rel_bias,
                  row(diff_lq1[0]), row(diff_lk1[0]), row(diff_lq2[0]), row(diff_lk2[0]),
                  row(diff_subln_g[0]), tq)

    out = _ffn(x2, out_a, out_b.reshape(n, -1), w_out[0].astype(BF16), row(norm2_g[0]),
               ffn_w_up[0].astype(BF16), ffn_conv_w[0], row(ffn_conv_b[0]), ffn_w_down[0].astype(BF16),
               row(final_norm_g), seq, tm=512, n_chunks=1)
    return out.reshape(bsz, seq, d)
```

```python
import functools
import math

import jax
import jax.numpy as jnp
import numpy as np
from jax import lax
from jax.experimental import pallas as pl
from jax.experimental.pallas import tpu as pltpu

CHUNK = 64
LRU_CONV = 4
LRU_C = 8.0
LRU_GATE_GROUPS = 2
DIFF_HEADS = 4
REL_BUCKETS = 32
REL_MAX_DIST = 128
FFN_CONV = 3
EPS = 1e-6
SUBLN_EPS = 1e-5
NEG_INF = -1e30
LAMBDA_INIT = 0.8 - 0.6 * math.exp(-0.3 * 0)

LOG2E = math.log2(math.e)
LOOKAHEAD = 6
MATMUL_ROWS = 256
KEY_BLOCK = 256

SUBLANES = 8
BF16_ROWS = 16
VMEM_LIMIT = 56 * 1024 * 1024

BF16 = jnp.bfloat16
F32 = jnp.float32


def _rms(x, g, eps):
    return x * lax.rsqrt(jnp.mean(x * x, axis=-1, keepdims=True) + eps) * g


def _gelu_tanh(x):
    inner = math.sqrt(2.0 / math.pi) * (x + 0.044715 * (x * x * x))
    return 0.5 * x * (1.0 + jnp.tanh(inner))


def _const_spec(shape):
    nd = len(shape)
    return pl.BlockSpec(shape, lambda *_: (0,) * nd, pipeline_mode=pl.Buffered(1))


def _shift_rows(ext, s):
    return pltpu.roll(ext, s, 0)[SUBLANES:]


def _segment_perm(tile):
    seg_len = tile // SUBLANES
    r = np.arange(tile)
    perm = np.zeros((tile, tile), np.float32)
    perm[r, (r % SUBLANES) * seg_len + r // SUBLANES] = 1.0
    return perm


def _lru_conv(lx, lg, p_ref, cw_ref, cb_ref, xtail_ref):
    tile, width = lx.shape
    npos = tile // SUBLANES
    ntap = LRU_CONV - 1
    sub = lax.broadcasted_iota(jnp.int32, (SUBLANES, width), 0)
    slab = lambda v, p: v[p * SUBLANES:(p + 1) * SUBLANES]
    x = jnp.dot(p_ref[...], lx, preferred_element_type=F32)
    gate = jnp.dot(p_ref[...], lg, preferred_element_type=F32)
    wrapped = []
    for q in range(ntap):
        cur = slab(x, npos - ntap + q)
        prev = slab(xtail_ref[...], q)
        wrapped.append(pltpu.roll(jnp.where(sub == SUBLANES - 1, prev, cur), 1, 0))
    xtail_ref[...] = x[tile - ntap * SUBLANES:]
    xc = cb_ref[...] + cw_ref[LRU_CONV - 1:LRU_CONV, :] * x
    for s in range(1, LRU_CONV):
        shifted = jnp.concatenate(wrapped[ntap - s:] + [x[:tile - SUBLANES * s]], axis=0)
        xc = xc + cw_ref[LRU_CONV - 1 - s:LRU_CONV - s, :] * shifted
    return xc, gate


def _lru_gates(xc, wg_ref, bg_ref):
    xcb = xc.astype(BF16)
    ngrp, gw, _ = wg_ref.shape
    pres = [jnp.dot(xcb[:, j * gw:(j + 1) * gw], wg_ref[j], preferred_element_type=F32) for j in range(ngrp)]
    pre_r = jnp.concatenate([pj[:, :gw] for pj in pres], axis=1) + bg_ref[0:1, :]
    pre_i = jnp.concatenate([pj[:, gw:] for pj in pres], axis=1) + bg_ref[1:2, :]
    return pre_r, pre_i


def _lru_scan(xc, gate, pre_r, pre_i, lam_ref, hend_ref):
    tile, width = xc.shape
    npos = tile // SUBLANES
    sub = lax.broadcasted_iota(jnp.int32, (SUBLANES, width), 0)
    slab = lambda v, p: v[p * SUBLANES:(p + 1) * SUBLANES]
    r = jax.nn.sigmoid(pre_r)
    gate_i = jax.nn.sigmoid(pre_i)
    z = -lam_ref[...]
    softplus = jnp.maximum(z, 0.0) + jnp.log(1.0 + jnp.exp(-jnp.abs(z)))
    a = jnp.exp((-LRU_C) * r * softplus)
    y = 1.0 - a * a
    u = jnp.where(y > 0.0, y * lax.rsqrt(y), 0.0) * (gate_i * xc)

    end, decay = slab(u, 0), slab(a, 0)
    for p in range(1, npos):
        ap = slab(a, p)
        end = ap * end + slab(u, p)
        decay = ap * decay
    inc = pltpu.roll(jnp.where(sub == SUBLANES - 1, hend_ref[...], end), 1, 0)
    dec = jnp.where(sub == 0, 0.0, pltpu.roll(decay, 1, 0))
    d = 1
    while d < SUBLANES:
        keep = sub >= d
        inc = jnp.where(keep, dec * pltpu.roll(inc, d, 0) + inc, inc)
        dec = jnp.where(keep, dec * pltpu.roll(dec, d, 0), dec)
        d *= 2
    state = inc
    hs = []
    for p in range(npos):
        state = slab(a, p) * state + slab(u, p)
        hs.append(state)
    hend_ref[...] = state
    return (_gelu_tanh(gate) * jnp.concatenate(hs, axis=0)).astype(BF16)


def _inproj_kernel(qk_scale, steps_per_seq, x_ref, g_ref, w_ref, wt_ref, p_ref, pt_ref, cw_ref, cb_ref, wg_ref,
                   bg_ref, lam_ref, oa_ref, k_ref, qt_ref, vt_ref, lx_ref, lg_ref, xtail_ref, hend_ref):
    i = pl.program_id(0)

    @pl.when(i == 0)
    def _():
        lx_ref[...] = jnp.zeros_like(lx_ref)
        lg_ref[...] = jnp.zeros_like(lg_ref)

    @pl.when((i + steps_per_seq - 1) % steps_per_seq == 0)
    def _():
        xtail_ref[...] = jnp.zeros_like(xtail_ref)
        hend_ref[...] = jnp.zeros_like(hend_ref)

    lx_prev = lx_ref[...]
    lg_prev = lg_ref[...]
    tm, width = oa_ref.shape
    tile = p_ref.shape[0]
    assert tm == 2 * tile, "the interleaving below is written for two scan tiles per step"
    rows = (slice(0, tile), slice(tile, tm))
    nt = (((1,), (1,)), ((), ()))
    unperm = lambda out: jnp.dot(pt_ref[...], out, preferred_element_type=F32).astype(oa_ref.dtype)

    h = _rms(x_ref[...], g_ref[...], EPS).astype(BF16)
    xc0, gate0 = _lru_conv(lx_prev[rows[0]], lg_prev[rows[0]], p_ref, cw_ref, cb_ref, xtail_ref)
    lx_ref[...] = jnp.dot(h, w_ref[:, :width], preferred_element_type=F32).astype(BF16)
    pre0 = _lru_gates(xc0, wg_ref, bg_ref)
    xc1, gate1 = _lru_conv(lx_prev[rows[1]], lg_prev[rows[1]], p_ref, cw_ref, cb_ref, xtail_ref)
    lg_ref[...] = jnp.dot(h, w_ref[:, width:2 * width], preferred_element_type=F32).astype(BF16)
    pre1 = _lru_gates(xc1, wg_ref, bg_ref)
    out0 = _lru_scan(xc0, gate0, *pre0, lam_ref, hend_ref)
    k_ref[...] = jnp.dot(h, w_ref[:, 3 * width:4 * width], preferred_element_type=F32).astype(k_ref.dtype)
    oa_ref[rows[0], :] = unperm(out0)
    out1 = _lru_scan(xc1, gate1, *pre1, lam_ref, hend_ref)
    qt = lax.dot_general(wt_ref[2 * width:3 * width, :], h, nt, preferred_element_type=F32)
    qt_ref[...] = (qt * qk_scale).astype(qt_ref.dtype)
    oa_ref[rows[1], :] = unperm(out1)
    vt = lax.dot_general(wt_ref[4 * width:, :], h, nt, preferred_element_type=F32)
    vt_ref[...] = vt.astype(vt_ref.dtype)


def _inproj(x2, g, w, w_t, conv_w, conv_b, w_gates, b_gates, lam, bsz, seq, tm, lru_tile, qk_scale):
    n, d = x2.shape
    width = w.shape[1] // 5
    spb = seq // tm
    last = n // tm - 1
    perm = _segment_perm(lru_tile)
    p_mat, pt_mat = jnp.asarray(perm, BF16), jnp.asarray(perm.T, BF16)
    consts = (g, w, w_t, p_mat, pt_mat, conv_w, conv_b, w_gates, b_gates, lam)
    proj = lambda i: jnp.minimum(i, last)
    col_spec = pl.BlockSpec((None, width, tm), lambda i: (proj(i) // spb, 0, proj(i) % spb))
    return pl.pallas_call(
        functools.partial(_inproj_kernel, qk_scale, spb),
        grid=(n // tm + 1,),
        in_specs=[pl.BlockSpec((tm, d), lambda i: (proj(i), 0))] + [_const_spec(c.shape) for c in consts],
        out_specs=[pl.BlockSpec((tm, width), lambda i: (jnp.maximum(i - 1, 0), 0)),
                   pl.BlockSpec((tm, width), lambda i: (proj(i), 0)), col_spec, col_spec],
        out_shape=[jax.ShapeDtypeStruct((n, width), BF16), jax.ShapeDtypeStruct((n, width), BF16),
                   jax.ShapeDtypeStruct((bsz, width, seq), BF16), jax.ShapeDtypeStruct((bsz, width, seq), BF16)],
        scratch_shapes=[pltpu.VMEM((tm, width), BF16), pltpu.VMEM((tm, width), BF16),
                        pltpu.VMEM(((LRU_CONV - 1) * SUBLANES, width), F32), pltpu.VMEM((SUBLANES, width), F32)],
        compiler_params=pltpu.CompilerParams(dimension_semantics=("arbitrary",), vmem_limit_bytes=VMEM_LIMIT),
        name="inproj_lru",
    )(x2, *consts)


def _rel_bucket_np(rel):
    half = REL_BUCKETS // 2
    max_exact = half // 2
    ret = (rel > 0).astype(np.int32) * half
    n = np.abs(rel)
    nf = np.maximum(n, 1).astype(np.float32)
    frac = np.log(nf / np.float32(max_exact)) / np.float32(math.log(REL_MAX_DIST / max_exact))
    large = max_exact + (frac * np.float32(half - max_exact)).astype(np.int32)
    large = np.minimum(large, half - 1)
    return ret + np.where(n < max_exact, n, large)


def _bucket_tile(tq):
    ql = np.arange(tq)[None, :]
    kpos = np.arange(2 * tq)[:, None] - tq
    bucket = _rel_bucket_np(kpos - ql).astype(np.int32)
    allowed = (kpos // CHUNK) <= (ql // CHUNK)
    return np.where(allowed, bucket, -1).astype(np.int32)


def _far_bucket(tq, seq):
    far = _rel_bucket_np(-np.arange(tq + 1, seq))
    assert (far == far[0]).all()
    return int(far[0])


def _bias_kernel(rb_ref, bucket_ref, o_ref):
    h = pl.program_id(0)
    bucket = bucket_ref[...]
    acc = jnp.zeros(bucket.shape, F32)
    for b in range(REL_BUCKETS):
        acc = jnp.where(bucket == b, rb_ref[b, h], acc)
    tile = jnp.where(bucket < 0, NEG_INF, acc) * LOG2E
    o_ref[...] = jnp.concatenate([tile, tile], axis=1)


def _bias_tiles(rel_bias, tq):
    bucket = jnp.asarray(_bucket_tile(tq))
    tk2 = bucket.shape[0]
    return pl.pallas_call(
        _bias_kernel,
        grid=(DIFF_HEADS,),
        in_specs=[pl.BlockSpec(memory_space=pltpu.SMEM), pl.BlockSpec((tk2, tq), lambda h: (0, 0))],
        out_specs=pl.BlockSpec((None, tk2, 2 * tq), lambda h: (h, 0, 0)),
        out_shape=jax.ShapeDtypeStruct((DIFF_HEADS, tk2, 2 * tq), F32),
        name="relbias",
    )(rel_bias, bucket)


def _attn_kernel(far_bucket, tq, rb_ref, qt_ref, k_ref, vt_ref, bias_ref, lq1_ref, lk1_ref, lq2_ref, lk2_ref,
                 sg_ref, o_ref):
    h = pl.program_id(1)
    hd, seq = qt_ref.shape
    qk = hd // 2
    far_bias = rb_ref[far_bucket, h] * LOG2E
    lam = (jnp.exp(jnp.sum(lq1_ref[...] * lk1_ref[...], axis=1, keepdims=True))
           - jnp.exp(jnp.sum(lq2_ref[...] * lk2_ref[...], axis=1, keepdims=True)) + LAMBDA_INIT)
    sub = lax.broadcasted_iota(jnp.int32, (hd, 2 * tq), 0)
    lane = lax.broadcasted_iota(jnp.int32, (hd, 2 * tq), 1)
    own_rows = (sub < qk) == (lane < tq)
    near_lo = lambda i: max(i - 1, 0) * tq
    ones = jnp.ones((BF16_ROWS, seq), BF16)
    values = lambda lo, hi: jnp.concatenate([vt_ref[:, lo:hi], ones[:, lo:hi]], axis=0)

    def logits(i):
        qt = qt_ref[:, i * tq:(i + 1) * tq]
        qt = jnp.concatenate([qt, qt], axis=1)
        qtc = jnp.where(own_rows, qt, jnp.zeros_like(qt))
        bias = bias_ref[...] if i > 0 else bias_ref[tq:, :]
        s_near = jnp.dot(k_ref[near_lo(i):(i + 1) * tq, :], qtc, preferred_element_type=F32) + bias
        s_far = jnp.dot(k_ref[:near_lo(i), :], qtc, preferred_element_type=F32) if near_lo(i) else None
        return s_near, s_far

    def softmax_pv(i, s_near, s_far):
        m = jnp.max(s_near, axis=0, keepdims=True)
        if s_far is not None:
            m = jnp.maximum(m, jnp.max(s_far, axis=0, keepdims=True) + far_bias)
        pv = None
        pieces = [(s_near, near_lo(i), m)] + ([(s_far, 0, m - far_bias)] if s_far is not None else [])
        for s, lo, shift in pieces:
            for r in range(0, s.shape[0], KEY_BLOCK):
                e = min(r + KEY_BLOCK, s.shape[0])
                p = jnp.exp2(s[r:e] - shift).astype(BF16)
                d = jnp.dot(values(lo + r, lo + e), p, preferred_element_type=F32)
                pv = d if pv is None else pv + d
        return pv[:hd] * (1.0 / pv[hd:hd + 1])

    nblk = seq // tq
    pending = [logits(i) for i in range(min(LOOKAHEAD, nblk))]
    for i in range(nblk):
        current = pending.pop(0)
        if i + LOOKAHEAD < nblk:
            pending.append(logits(i + LOOKAHEAD))
        maps = softmax_pv(i, *current)
        out = maps[:, :tq] - lam * maps[:, tq:]
        out = out * lax.rsqrt(jnp.mean(out * out, axis=0, keepdims=True) + SUBLN_EPS)
        o_ref[i * tq:(i + 1) * tq, :] = (out.T * sg_ref[...] * (1.0 - LAMBDA_INIT)).astype(o_ref.dtype)


def _attn(qt, k, vt, bias, rel_bias, lq1, lk1, lq2, lk2, subln_g, tq):
    bsz, seq, width = k.shape
    hd = width // DIFF_HEADS
    far_bucket = _far_bucket(tq, seq)
    col_spec = pl.BlockSpec((None, hd, seq), lambda b, h: (b, h, 0))
    row_spec = pl.BlockSpec((None, seq, hd), lambda b, h: (b, 0, h))
    return pl.pallas_call(
        functools.partial(_attn_kernel, far_bucket, tq),
        grid=(bsz, DIFF_HEADS),
        in_specs=[pl.BlockSpec(memory_space=pltpu.SMEM), col_spec, row_spec, col_spec,
                  pl.BlockSpec((None, 2 * tq, 2 * tq), lambda b, h: (h, 0, 0)),
                  _const_spec(lq1.shape), _const_spec(lk1.shape), _const_spec(lq2.shape),
                  _const_spec(lk2.shape), _const_spec(subln_g.shape)],
        out_specs=row_spec,
        out_shape=jax.ShapeDtypeStruct((bsz, seq, width), BF16),
        compiler_params=pltpu.CompilerParams(dimension_semantics=("arbitrary",) * 2,
                                             vmem_limit_bytes=VMEM_LIMIT),
        name="diffattn",
    )(rel_bias, qt, k, vt, bias, lq1, lk1, lq2, lk2, subln_g)


def _ffn_kernel(steps_per_seq, n_chunks, x_ref, a_ref, b_ref, wo_ref, g2_ref, wu_ref, cw_ref, cb_ref,
                wd_ref, gf_ref, o_ref, tail_ref):
    tm = x_ref.shape[0]
    half = a_ref.shape[1]
    d_ff = wd_ref.shape[0]
    fc = d_ff // n_chunks

    @pl.when(pl.program_id(0) % steps_per_seq == 0)
    def _():
        tail_ref[...] = jnp.zeros_like(tail_ref)

    def rows_dot(lhs, rhs):
        parts = [jnp.dot(lhs[r:r + MATMUL_ROWS], rhs, preferred_element_type=F32)
                 for r in range(0, tm, MATMUL_ROWS)]
        return jnp.concatenate(parts, axis=0)

    x1 = x_ref[...] + (rows_dot(a_ref[...], wo_ref[:half, :]) + rows_dot(b_ref[...], wo_ref[half:, :]))
    h2 = _rms(x1, g2_ref[...], EPS).astype(BF16)

    def up(c):
        gate = rows_dot(h2, wu_ref[:, c * fc:(c + 1) * fc])
        val = rows_dot(h2, wu_ref[:, d_ff + c * fc:d_ff + (c + 1) * fc])
        return gate, val

    def activate(c, gate, val):
        cols = slice(c * fc, (c + 1) * fc)
        ext = jnp.concatenate([tail_ref[:, cols], gate], axis=0)
        conv = cb_ref[:, cols] + cw_ref[FFN_CONV - 1:FFN_CONV, cols] * gate
        for s in range(1, FFN_CONV):
            conv = conv + cw_ref[FFN_CONV - 1 - s:FFN_CONV - s, cols] * _shift_rows(ext, s)
        tail_ref[:, cols] = gate[tm - SUBLANES:]
        return (_gelu_tanh(conv) * val).astype(BF16)

    ffn = None
    pending = up(0)
    for c in range(n_chunks):
        current = pending
        if c + 1 < n_chunks:
            pending = up(c + 1)
        act = activate(c, *current)
        down = rows_dot(act, wd_ref[c * fc:(c + 1) * fc, :])
        ffn = down if ffn is None else ffn + down
    o_ref[...] = _rms(x1 + ffn, gf_ref[...], EPS)


def _ffn(x2, out_a, out_b, w_out, g2, w_up, conv_w, conv_b, w_down, gf, seq, tm, n_chunks):
    n, d = x2.shape
    half = out_a.shape[1]
    d_ff = w_down.shape[0]
    return pl.pallas_call(
        functools.partial(_ffn_kernel, seq // tm, n_chunks),
        grid=(n // tm,),
        in_specs=[pl.BlockSpec((tm, d), lambda i: (i, 0)),
                  pl.BlockSpec((tm, half), lambda i: (i, 0)),
                  pl.BlockSpec((tm, half), lambda i: (i, 0)),
                  _const_spec(w_out.shape), _const_spec(g2.shape), _const_spec(w_up.shape),
                  _const_spec(conv_w.shape), _const_spec(conv_b.shape), _const_spec(w_down.shape),
                  _const_spec(gf.shape)],
        out_specs=pl.BlockSpec((tm, d), lambda i: (i, 0)),
        out_shape=jax.ShapeDtypeStruct((n, d), F32),
        scratch_shapes=[pltpu.VMEM((SUBLANES, d_ff), F32)],
        compiler_params=pltpu.CompilerParams(dimension_semantics=("arbitrary",), vmem_limit_bytes=VMEM_LIMIT),
        name="outproj_ffn",
    )(x2, out_a, out_b, w_out, g2, w_up, conv_w, conv_b, w_down, gf)


def _gate_weights(wa, wx, groups):
    nb, bd, _ = wa.shape
    per = nb // groups
    w = jnp.stack([wa, wx]).reshape(2, groups, per, bd, bd)
    dense = jnp.einsum("tjbio,bc->jbitco", w, jnp.eye(per, dtype=w.dtype))
    return dense.reshape(groups, per * bd, 2 * per * bd)


def kernel(x, norm1_g, w_in, lru_conv_w, lru_conv_b, lru_wa, lru_ba, lru_wx, lru_bx, lru_lambda, diff_lq1, diff_lk1, diff_lq2, diff_lk2, diff_subln_g, rel_bias, w_out, norm2_g, ffn_w_up, ffn_conv_w, ffn_conv_b, ffn_w_down, final_norm_g):
    bsz, seq, d = x.shape
    assert w_in.shape[0] == 1, "single-layer block"
    n = bsz * seq
    x2 = x.reshape(n, d)
    row = lambda p: p.reshape(1, -1)

    lru_w = lru_wa.shape[1] * lru_wa.shape[2]
    attn_w = (w_in.shape[2] - 2 * lru_w) // 3
    qk_dim = attn_w // DIFF_HEADS // 2
    tq = 128
    assert attn_w == lru_w, "the projection kernel slices five equal column groups"
    w_gates = _gate_weights(lru_wa[0], lru_wx[0], LRU_GATE_GROUPS).astype(BF16)
    b_gates = jnp.stack([lru_ba[0].reshape(lru_w), lru_bx[0].reshape(lru_w)])
    out_a, k, qt, vt = _inproj(x2, row(norm1_g[0]), w_in[0].astype(BF16), w_in[0].T.astype(BF16),
                               lru_conv_w[0], row(lru_conv_b[0]),
                               w_gates, b_gates, row(lru_lambda[0]), bsz, seq, tm=512, lru_tile=256,
                               qk_scale=qk_dim ** -0.5 * LOG2E)

    bias = _bias_tiles(rel_bias, tq)
    out_b = _attn(qt, k.reshape(bsz, seq, attn_w), vt, bias, rel_bias,
                  row(diff_lq1[0]), row(diff_lk1[0]), row(diff_lq2[0]), row(diff_lk2[0]),
                  row(diff_subln_g[0]), tq)

    out = _ffn(x2, out_a, out_b.reshape(n, -1), w_out[0].astype(BF16), row(norm2_g[0]),
               ffn_w_up[0].astype(BF16), ffn_conv_w[0], row(ffn_conv_b[0]), ffn_w_down[0].astype(BF16),
               row(final_norm_g), seq, tm=512, n_chunks=1)
    return out.reshape(bsz, seq, d)
```

```python
import functools
import math

import jax
import jax.numpy as jnp
import numpy as np
from jax import lax
from jax.experimental import pallas as pl
from jax.experimental.pallas import tpu as pltpu

CHUNK = 64
LRU_CONV = 4
LRU_C = 8.0
LRU_GATE_GROUPS = 2
DIFF_HEADS = 4
REL_BUCKETS = 32
REL_MAX_DIST = 128
FFN_CONV = 3
EPS = 1e-6
SUBLN_EPS = 1e-5
NEG_INF = -1e30
LAMBDA_INIT = 0.8 - 0.6 * math.exp(-0.3 * 0)

LOG2E = math.log2(math.e)

ROW_TILE = 512
LRU_TILE = 256
QUERY_BLOCK = 128
KEY_BLOCK = 256
LOOKAHEAD = 6
MATMUL_ROWS = 256
FFN_CHUNKS = 1

SUBLANES = 8
BF16_ROWS = 16
VMEM_LIMIT = 56 * 1024 * 1024

BF16 = jnp.bfloat16
F32 = jnp.float32


def _rms(x, g, eps):
    return x * lax.rsqrt(jnp.mean(x * x, axis=-1, keepdims=True) + eps) * g


def _gelu_tanh(x):
    inner = math.sqrt(2.0 / math.pi) * (x + 0.044715 * (x * x * x))
    return 0.5 * x * (1.0 + jnp.tanh(inner))


def _const_spec(shape):
    nd = len(shape)
    return pl.BlockSpec(shape, lambda *_: (0,) * nd, pipeline_mode=pl.Buffered(1))


def _shift_rows(ext, s):
    return pltpu.roll(ext, s, 0)[SUBLANES:]


def _segment_perm(tile):
    seg_len = tile // SUBLANES
    r = np.arange(tile)
    perm = np.zeros((tile, tile), np.float32)
    perm[r, (r % SUBLANES) * seg_len + r // SUBLANES] = 1.0
    return perm


def _lru_conv(lx, lg, p_ref, cw_ref, cb_ref, xtail_ref):
    tile, width = lx.shape
    npos = tile // SUBLANES
    ntap = LRU_CONV - 1
    sub = lax.broadcasted_iota(jnp.int32, (SUBLANES, width), 0)
    slab = lambda v, p: v[p * SUBLANES:(p + 1) * SUBLANES]
    x = jnp.dot(p_ref[...], lx, preferred_element_type=F32)
    gate = jnp.dot(p_ref[...], lg, preferred_element_type=F32)
    wrapped = []
    for q in range(ntap):
        cur = slab(x, npos - ntap + q)
        prev = slab(xtail_ref[...], q)
        wrapped.append(pltpu.roll(jnp.where(sub == SUBLANES - 1, prev, cur), 1, 0))
    xtail_ref[...] = x[tile - ntap * SUBLANES:]
    xc = cb_ref[...] + cw_ref[LRU_CONV - 1:LRU_CONV, :] * x
    for s in range(1, LRU_CONV):
        shifted = jnp.concatenate(wrapped[ntap - s:] + [x[:tile - SUBLANES * s]], axis=0)
        xc = xc + cw_ref[LRU_CONV - 1 - s:LRU_CONV - s, :] * shifted
    return xc, gate


def _lru_gates(xc, wg_ref, bg_ref):
    xcb = xc.astype(BF16)
    ngrp, gw, _ = wg_ref.shape
    pres = [jnp.dot(xcb[:, j * gw:(j + 1) * gw], wg_ref[j], preferred_element_type=F32) for j in range(ngrp)]
    pre_r = jnp.concatenate([pj[:, :gw] for pj in pres], axis=1) + bg_ref[0:1, :]
    pre_i = jnp.concatenate([pj[:, gw:] for pj in pres], axis=1) + bg_ref[1:2, :]
    return pre_r, pre_i


def _lru_scan(xc, gate, pre_r, pre_i, lam_ref, hend_ref):
    tile, width = xc.shape
    npos = tile // SUBLANES
    sub = lax.broadcasted_iota(jnp.int32, (SUBLANES, width), 0)
    slab = lambda v, p: v[p * SUBLANES:(p + 1) * SUBLANES]
    r = jax.nn.sigmoid(pre_r)
    gate_i = jax.nn.sigmoid(pre_i)
    z = -lam_ref[...]
    softplus = jnp.maximum(z, 0.0) + jnp.log(1.0 + jnp.exp(-jnp.abs(z)))
    a = jnp.exp((-LRU_C) * r * softplus)
    y = 1.0 - a * a
    u = jnp.where(y > 0.0, y * lax.rsqrt(y), 0.0) * (gate_i * xc)

    end, decay = slab(u, 0), slab(a, 0)
    for p in range(1, npos):
        ap = slab(a, p)
        end = ap * end + slab(u, p)
        decay = ap * decay
    inc = pltpu.roll(jnp.where(sub == SUBLANES - 1, hend_ref[...], end), 1, 0)
    dec = jnp.where(sub == 0, 0.0, pltpu.roll(decay, 1, 0))
    d = 1
    while d < SUBLANES:
        keep = sub >= d
        inc = jnp.where(keep, dec * pltpu.roll(inc, d, 0) + inc, inc)
        dec = jnp.where(keep, dec * pltpu.roll(dec, d, 0), dec)
        d *= 2
    state = inc
    hs = []
    for p in range(npos):
        state = slab(a, p) * state + slab(u, p)
        hs.append(state)
    hend_ref[...] = state
    return (_gelu_tanh(gate) * jnp.concatenate(hs, axis=0)).astype(BF16)


def _inproj_kernel(qk_scale, steps_per_seq, x_ref, g_ref, w_ref, wt_ref, p_ref, pt_ref, cw_ref, cb_ref, wg_ref,
                   bg_ref, lam_ref, oa_ref, k_ref, qt_ref, vt_ref, lx_ref, lg_ref, xtail_ref, hend_ref):
    i = pl.program_id(0)

    @pl.when(i == 0)
    def _():
        lx_ref[...] = jnp.zeros_like(lx_ref)
        lg_ref[...] = jnp.zeros_like(lg_ref)

    @pl.when((i + steps_per_seq - 1) % steps_per_seq == 0)
    def _():
        xtail_ref[...] = jnp.zeros_like(xtail_ref)
        hend_ref[...] = jnp.zeros_like(hend_ref)

    lx_prev = lx_ref[...]
    lg_prev = lg_ref[...]
    tm, width = oa_ref.shape
    tile = p_ref.shape[0]
    assert tm == 2 * tile, "the interleaving below is written for two scan tiles per step"
    rows = (slice(0, tile), slice(tile, tm))
    nt = (((1,), (1,)), ((), ()))
    unperm = lambda out: jnp.dot(pt_ref[...], out, preferred_element_type=F32).astype(oa_ref.dtype)

    h = _rms(x_ref[...], g_ref[...], EPS).astype(BF16)
    xc0, gate0 = _lru_conv(lx_prev[rows[0]], lg_prev[rows[0]], p_ref, cw_ref, cb_ref, xtail_ref)
    lx_ref[...] = jnp.dot(h, w_ref[:, :width], preferred_element_type=F32).astype(BF16)
    pre0 = _lru_gates(xc0, wg_ref, bg_ref)
    xc1, gate1 = _lru_conv(lx_prev[rows[1]], lg_prev[rows[1]], p_ref, cw_ref, cb_ref, xtail_ref)
    lg_ref[...] = jnp.dot(h, w_ref[:, width:2 * width], preferred_element_type=F32).astype(BF16)
    pre1 = _lru_gates(xc1, wg_ref, bg_ref)
    out0 = _lru_scan(xc0, gate0, *pre0, lam_ref, hend_ref)
    k_ref[...] = jnp.dot(h, w_ref[:, 3 * width:4 * width], preferred_element_type=F32).astype(k_ref.dtype)
    oa_ref[rows[0], :] = unperm(out0)
    out1 = _lru_scan(xc1, gate1, *pre1, lam_ref, hend_ref)
    qt = lax.dot_general(wt_ref[2 * width:3 * width, :], h, nt, preferred_element_type=F32)
    qt_ref[...] = (qt * qk_scale).astype(qt_ref.dtype)
    oa_ref[rows[1], :] = unperm(out1)
    vt = lax.dot_general(wt_ref[4 * width:, :], h, nt, preferred_element_type=F32)
    vt_ref[...] = vt.astype(vt_ref.dtype)


def _inproj(x2, g, w, w_t, conv_w, conv_b, w_gates, b_gates, lam, bsz, seq, tm, lru_tile, qk_scale):
    n, d = x2.shape
    width = w.shape[1] // 5
    spb = seq // tm
    last = n // tm - 1
    perm = _segment_perm(lru_tile)
    p_mat, pt_mat = jnp.asarray(perm, BF16), jnp.asarray(perm.T, BF16)
    consts = (g, w, w_t, p_mat, pt_mat, conv_w, conv_b, w_gates, b_gates, lam)
    proj = lambda i: jnp.minimum(i, last)
    col_spec = pl.BlockSpec((None, width, tm), lambda i: (proj(i) // spb, 0, proj(i) % spb))
    return pl.pallas_call(
        functools.partial(_inproj_kernel, qk_scale, spb),
        grid=(n // tm + 1,),
        in_specs=[pl.BlockSpec((tm, d), lambda i: (proj(i), 0))] + [_const_spec(c.shape) for c in consts],
        out_specs=[pl.BlockSpec((tm, width), lambda i: (jnp.maximum(i - 1, 0), 0)),
                   pl.BlockSpec((tm, width), lambda i: (proj(i), 0)), col_spec, col_spec],
        out_shape=[jax.ShapeDtypeStruct((n, width), BF16), jax.ShapeDtypeStruct((n, width), BF16),
                   jax.ShapeDtypeStruct((bsz, width, seq), BF16), jax.ShapeDtypeStruct((bsz, width, seq), BF16)],
        scratch_shapes=[pltpu.VMEM((tm, width), BF16), pltpu.VMEM((tm, width), BF16),
                        pltpu.VMEM(((LRU_CONV - 1) * SUBLANES, width), F32), pltpu.VMEM((SUBLANES, width), F32)],
        compiler_params=pltpu.CompilerParams(dimension_semantics=("arbitrary",), vmem_limit_bytes=VMEM_LIMIT),
        name="inproj_lru",
    )(x2, *consts)


def _rel_bucket_np(rel):
    half = REL_BUCKETS // 2
    max_exact = half // 2
    ret = (rel > 0).astype(np.int32) * half
    n = np.abs(rel)
    nf = np.maximum(n, 1).astype(np.float32)
    frac = np.log(nf / np.float32(max_exact)) / np.float32(math.log(REL_MAX_DIST / max_exact))
    large = max_exact + (frac * np.float32(half - max_exact)).astype(np.int32)
    large = np.minimum(large, half - 1)
    return ret + np.where(n < max_exact, n, large)


def _bucket_tile(tq):
    ql = np.arange(tq)[None, :]
    kpos = np.arange(2 * tq)[:, None] - tq
    bucket = _rel_bucket_np(kpos - ql).astype(np.int32)
    allowed = (kpos // CHUNK) <= (ql // CHUNK)
    return np.where(allowed, bucket, -1).astype(np.int32)


def _far_bucket(tq, seq):
    far = _rel_bucket_np(-np.arange(tq + 1, seq))
    assert (far == far[0]).all()
    return int(far[0])


def _bias_kernel(rb_ref, bucket_ref, o_ref):
    h = pl.program_id(0)
    bucket = bucket_ref[...]
    acc = jnp.zeros(bucket.shape, F32)
    for b in range(REL_BUCKETS):
        acc = jnp.where(bucket == b, rb_ref[b, h], acc)
    tile = jnp.where(bucket < 0, NEG_INF, acc) * LOG2E
    o_ref[...] = jnp.concatenate([tile, tile], axis=1)


def _bias_tiles(rel_bias, tq):
    bucket = jnp.asarray(_bucket_tile(tq))
    tk2 = bucket.shape[0]
    return pl.pallas_call(
        _bias_kernel,
        grid=(DIFF_HEADS,),
        in_specs=[pl.BlockSpec(memory_space=pltpu.SMEM), pl.BlockSpec((tk2, tq), lambda h: (0, 0))],
        out_specs=pl.BlockSpec((None, tk2, 2 * tq), lambda h: (h, 0, 0)),
        out_shape=jax.ShapeDtypeStruct((DIFF_HEADS, tk2, 2 * tq), F32),
        name="relbias",
    )(rel_bias, bucket)


def _attn_kernel(far_bucket, tq, rb_ref, qt_ref, k_ref, vt_ref, bias_ref, lq1_ref, lk1_ref, lq2_ref, lk2_ref,
                 sg_ref, o_ref):
    h = pl.program_id(1)
    hd, seq = qt_ref.shape
    qk = hd // 2
    far_bias = rb_ref[far_bucket, h] * LOG2E
    lam = (jnp.exp(jnp.sum(lq1_ref[...] * lk1_ref[...], axis=1, keepdims=True))
           - jnp.exp(jnp.sum(lq2_ref[...] * lk2_ref[...], axis=1, keepdims=True)) + LAMBDA_INIT)
    sub = lax.broadcasted_iota(jnp.int32, (hd, 2 * tq), 0)
    lane = lax.broadcasted_iota(jnp.int32, (hd, 2 * tq), 1)
    own_rows = (sub < qk) == (lane < tq)
    near_lo = lambda i: max(i - 1, 0) * tq
    ones = jnp.ones((BF16_ROWS, seq), BF16)
    values = lambda lo, hi: jnp.concatenate([vt_ref[:, lo:hi], ones[:, lo:hi]], axis=0)

    def logits(i):
        qt = qt_ref[:, i * tq:(i + 1) * tq]
        qt = jnp.concatenate([qt, qt], axis=1)
        qtc = jnp.where(own_rows, qt, jnp.zeros_like(qt))
        bias = bias_ref[...] if i > 0 else bias_ref[tq:, :]
        s_near = jnp.dot(k_ref[near_lo(i):(i + 1) * tq, :], qtc, preferred_element_type=F32) + bias
        s_far = jnp.dot(k_ref[:near_lo(i), :], qtc, preferred_element_type=F32) if near_lo(i) else None
        return s_near, s_far

    def softmax_pv(i, s_near, s_far):
        m = jnp.max(s_near, axis=0, keepdims=True)
        if s_far is not None:
            m = jnp.maximum(m, jnp.max(s_far, axis=0, keepdims=True) + far_bias)
        pv = None
        pieces = [(s_near, near_lo(i), m)] + ([(s_far, 0, m - far_bias)] if s_far is not None else [])
        for s, lo, shift in pieces:
            for r in range(0, s.shape[0], KEY_BLOCK):
                e = min(r + KEY_BLOCK, s.shape[0])
                p = jnp.exp2(s[r:e] - shift).astype(BF16)
                d = jnp.dot(values(lo + r, lo + e), p, preferred_element_type=F32)
                pv = d if pv is None else pv + d
        return pv[:hd] * (1.0 / pv[hd:hd + 1])

    nblk = seq // tq
    pending = [logits(i) for i in range(min(LOOKAHEAD, nblk))]
    for i in range(nblk):
        current = pending.pop(0)
        if i + LOOKAHEAD < nblk:
            pending.append(logits(i + LOOKAHEAD))
        maps = softmax_pv(i, *current)
        out = maps[:, :tq] - lam * maps[:, tq:]
        out = out * lax.rsqrt(jnp.mean(out * out, axis=0, keepdims=True) + SUBLN_EPS)
        o_ref[i * tq:(i + 1) * tq, :] = (out.T * sg_ref[...] * (1.0 - LAMBDA_INIT)).astype(o_ref.dtype)


def _attn(qt, k, vt, bias, rel_bias, lq1, lk1, lq2, lk2, subln_g, tq):
    bsz, seq, width = k.shape
    hd = width // DIFF_HEADS
    far_bucket = _far_bucket(tq, seq)
    col_spec = pl.BlockSpec((None, hd, seq), lambda b, h: (b, h, 0))
    row_spec = pl.BlockSpec((None, seq, hd), lambda b, h: (b, 0, h))
    return pl.pallas_call(
        functools.partial(_attn_kernel, far_bucket, tq),
        grid=(bsz, DIFF_HEADS),
        in_specs=[pl.BlockSpec(memory_space=pltpu.SMEM), col_spec, row_spec, col_spec,
                  pl.BlockSpec((None, 2 * tq, 2 * tq), lambda b, h: (h, 0, 0)),
                  _const_spec(lq1.shape), _const_spec(lk1.shape), _const_spec(lq2.shape),
                  _const_spec(lk2.shape), _const_spec(subln_g.shape)],
        out_specs=row_spec,
        out_shape=jax.ShapeDtypeStruct((bsz, seq, width), BF16),
        compiler_params=pltpu.CompilerParams(dimension_semantics=("arbitrary",) * 2,
                                             vmem_limit_bytes=VMEM_LIMIT),
        name="diffattn",
    )(rel_bias, qt, k, vt, bias, lq1, lk1, lq2, lk2, subln_g)


def _ffn_kernel(steps_per_seq, n_chunks, x_ref, a_ref, b_ref, wo_ref, g2_ref, wu_ref, cw_ref, cb_ref,
                wd_ref, gf_ref, o_ref, tail_ref):
    tm = x_ref.shape[0]
    half = a_ref.shape[1]
    d_ff = wd_ref.shape[0]
    fc = d_ff // n_chunks

    @pl.when(pl.program_id(0) % steps_per_seq == 0)
    def _():
        tail_ref[...] = jnp.zeros_like(tail_ref)

    def rows_dot(lhs, rhs):
        parts = [jnp.dot(lhs[r:r + MATMUL_ROWS], rhs, preferred_element_type=F32)
                 for r in range(0, tm, MATMUL_ROWS)]
        return jnp.concatenate(parts, axis=0)

    x1 = x_ref[...] + (rows_dot(a_ref[...], wo_ref[:half, :]) + rows_dot(b_ref[...], wo_ref[half:, :]))
    h2 = _rms(x1, g2_ref[...], EPS).astype(BF16)

    def up(c):
        gate = rows_dot(h2, wu_ref[:, c * fc:(c + 1) * fc])
        val = rows_dot(h2, wu_ref[:, d_ff + c * fc:d_ff + (c + 1) * fc])
        return gate, val

    def activate(c, gate, val):
        cols = slice(c * fc, (c + 1) * fc)
        ext = jnp.concatenate([tail_ref[:, cols], gate], axis=0)
        conv = cb_ref[:, cols] + cw_ref[FFN_CONV - 1:FFN_CONV, cols] * gate
        for s in range(1, FFN_CONV):
            conv = conv + cw_ref[FFN_CONV - 1 - s:FFN_CONV - s, cols] * _shift_rows(ext, s)
        tail_ref[:, cols] = gate[tm - SUBLANES:]
        return (_gelu_tanh(conv) * val).astype(BF16)

    ffn = None
    pending = up(0)
    for c in range(n_chunks):
        current = pending
        if c + 1 < n_chunks:
            pending = up(c + 1)
        act = activate(c, *current)
        down = rows_dot(act, wd_ref[c * fc:(c + 1) * fc, :])
        ffn = down if ffn is None else ffn + down
    o_ref[...] = _rms(x1 + ffn, gf_ref[...], EPS)


def _ffn(x2, out_a, out_b, w_out, g2, w_up, conv_w, conv_b, w_down, gf, seq, tm, n_chunks):
    n, d = x2.shape
    half = out_a.shape[1]
    d_ff = w_down.shape[0]
    return pl.pallas_call(
        functools.partial(_ffn_kernel, seq // tm, n_chunks),
        grid=(n // tm,),
        in_specs=[pl.BlockSpec((tm, d), lambda i: (i, 0)),
                  pl.BlockSpec((tm, half), lambda i: (i, 0)),
                  pl.BlockSpec((tm, half), lambda i: (i, 0)),
                  _const_spec(w_out.shape), _const_spec(g2.shape), _const_spec(w_up.shape),
                  _const_spec(conv_w.shape), _const_spec(conv_b.shape), _const_spec(w_down.shape),
                  _const_spec(gf.shape)],
        out_specs=pl.BlockSpec((tm, d), lambda i: (i, 0)),
        out_shape=jax.ShapeDtypeStruct((n, d), F32),
        scratch_shapes=[pltpu.VMEM((SUBLANES, d_ff), F32)],
        compiler_params=pltpu.CompilerParams(dimension_semantics=("arbitrary",), vmem_limit_bytes=VMEM_LIMIT),
        name="outproj_ffn",
    )(x2, out_a, out_b, w_out, g2, w_up, conv_w, conv_b, w_down, gf)


def _gate_weights(wa, wx, groups):
    nb, bd, _ = wa.shape
    per = nb // groups
    w = jnp.stack([wa, wx]).reshape(2, groups, per, bd, bd)
    dense = jnp.einsum("tjbio,bc->jbitco", w, jnp.eye(per, dtype=w.dtype))
    return dense.reshape(groups, per * bd, 2 * per * bd)


def kernel(x, norm1_g, w_in, lru_conv_w, lru_conv_b, lru_wa, lru_ba, lru_wx, lru_bx, lru_lambda, diff_lq1, diff_lk1, diff_lq2, diff_lk2, diff_subln_g, rel_bias, w_out, norm2_g, ffn_w_up, ffn_conv_w, ffn_conv_b, ffn_w_down, final_norm_g):
    bsz, seq, d = x.shape
    assert w_in.shape[0] == 1, "single-layer block"
    assert seq % ROW_TILE == 0 and ROW_TILE == 2 * LRU_TILE and ROW_TILE % MATMUL_ROWS == 0
    assert seq % QUERY_BLOCK == 0 and QUERY_BLOCK % CHUNK == 0 and KEY_BLOCK % QUERY_BLOCK == 0
    n = bsz * seq
    x2 = x.reshape(n, d)
    row = lambda p: p.reshape(1, -1)

    lru_w = lru_wa.shape[1] * lru_wa.shape[2]
    attn_w = (w_in.shape[2] - 2 * lru_w) // 3
    qk_dim = attn_w // DIFF_HEADS // 2
    tq = QUERY_BLOCK
    assert attn_w == lru_w, "the projection kernel slices five equal column groups"
    w_gates = _gate_weights(lru_wa[0], lru_wx[0], LRU_GATE_GROUPS).astype(BF16)
    b_gates = jnp.stack([lru_ba[0].reshape(lru_w), lru_bx[0].reshape(lru_w)])
    out_a, k, qt, vt = _inproj(x2, row(norm1_g[0]), w_in[0].astype(BF16), w_in[0].T.astype(BF16),
                               lru_conv_w[0], row(lru_conv_b[0]),
                               w_gates, b_gates, row(lru_lambda[0]), bsz, seq, tm=ROW_TILE, lru_tile=LRU_TILE,
                               qk_scale=qk_dim ** -0.5 * LOG2E)

    bias = _bias_tiles(rel_bias, tq)
    out_b = _attn(qt, k.reshape(bsz, seq, attn_w), vt, bias, rel_bias,
                  row(diff_lq1[0]), row(diff_lk1[0]), row(diff_lq2[0]), row(diff_lk2[0]),
                  row(diff_subln_g[0]), tq)

    out = _ffn(x2, out_a, out_b.reshape(n, -1), w_out[0].astype(BF16), row(norm2_g[0]),
               ffn_w_up[0].astype(BF16), ffn_conv_w[0], row(ffn_conv_b[0]), ffn_w_down[0].astype(BF16),
               row(final_norm_g), seq, tm=ROW_TILE, n_chunks=FFN_CHUNKS)
    return out.reshape(bsz, seq, d)
```

```python
import functools
import math

import jax
import jax.numpy as jnp
import numpy as np
from jax import lax
from jax.experimental import pallas as pl
from jax.experimental.pallas import tpu as pltpu

CHUNK = 64
LRU_CONV = 4
LRU_C = 8.0
LRU_GATE_GROUPS = 2
DIFF_HEADS = 4
REL_BUCKETS = 32
REL_MAX_DIST = 128
FFN_CONV = 3
EPS = 1e-6
SUBLN_EPS = 1e-5
NEG_INF = -1e30
LAMBDA_INIT = 0.8 - 0.6 * math.exp(-0.3 * 0)

LOG2E = math.log2(math.e)

ROW_TILE = 512
LRU_TILE = 256
QUERY_BLOCK = 128
KEY_BLOCK = 256
LOOKAHEAD = 6
MATMUL_ROWS = 256
FFN_CHUNKS = 1

SUBLANES = 8
BF16_ROWS = 16
VMEM_LIMIT = 56 * 1024 * 1024

BF16 = jnp.bfloat16
F32 = jnp.float32


def _rms(x, g, eps):
    return x * lax.rsqrt(jnp.mean(x * x, axis=-1, keepdims=True) + eps) * g


def _gelu_tanh(x):
    inner = math.sqrt(2.0 / math.pi) * (x + 0.044715 * (x * x * x))
    return 0.5 * x * (1.0 + jnp.tanh(inner))


def _const_spec(shape):
    nd = len(shape)
    return pl.BlockSpec(shape, lambda *_: (0,) * nd, pipeline_mode=pl.Buffered(1))


def _shift_rows(ext, s):
    return pltpu.roll(ext, s, 0)[SUBLANES:]


def _segment_perm(tile):
    seg_len = tile // SUBLANES
    r = np.arange(tile)
    perm = np.zeros((tile, tile), np.float32)
    perm[r, (r % SUBLANES) * seg_len + r // SUBLANES] = 1.0
    return perm


def _lru_conv(lx, lg, p_ref, cw_ref, cb_ref, xtail_ref):
    tile, width = lx.shape
    npos = tile // SUBLANES
    ntap = LRU_CONV - 1
    sub = lax.broadcasted_iota(jnp.int32, (SUBLANES, width), 0)
    slab = lambda v, p: v[p * SUBLANES:(p + 1) * SUBLANES]
    x = jnp.dot(p_ref[...], lx, preferred_element_type=F32)
    gate = jnp.dot(p_ref[...], lg, preferred_element_type=F32)
    wrapped = []
    for q in range(ntap):
        cur = slab(x, npos - ntap + q)
        prev = slab(xtail_ref[...], q)
        wrapped.append(pltpu.roll(jnp.where(sub == SUBLANES - 1, prev, cur), 1, 0))
    xtail_ref[...] = x[tile - ntap * SUBLANES:]
    xc = cb_ref[...] + cw_ref[LRU_CONV - 1:LRU_CONV, :] * x
    for s in range(1, LRU_CONV):
        shifted = jnp.concatenate(wrapped[ntap - s:] + [x[:tile - SUBLANES * s]], axis=0)
        xc = xc + cw_ref[LRU_CONV - 1 - s:LRU_CONV - s, :] * shifted
    return xc, gate


def _lru_gates(xc, wg_ref, bg_ref):
    xcb = xc.astype(BF16)
    ngrp, gw, _ = wg_ref.shape
    pres = [jnp.dot(xcb[:, j * gw:(j + 1) * gw], wg_ref[j], preferred_element_type=F32) for j in range(ngrp)]
    pre_r = jnp.concatenate([pj[:, :gw] for pj in pres], axis=1) + bg_ref[0:1, :]
    pre_i = jnp.concatenate([pj[:, gw:] for pj in pres], axis=1) + bg_ref[1:2, :]
    return pre_r, pre_i


def _lru_scan(xc, gate, pre_r, pre_i, lam_ref, hend_ref):
    tile, width = xc.shape
    npos = tile // SUBLANES
    sub = lax.broadcasted_iota(jnp.int32, (SUBLANES, width), 0)
    slab = lambda v, p: v[p * SUBLANES:(p + 1) * SUBLANES]
    r = jax.nn.sigmoid(pre_r)
    gate_i = jax.nn.sigmoid(pre_i)
    z = -lam_ref[...]
    softplus = jnp.maximum(z, 0.0) + jnp.log(1.0 + jnp.exp(-jnp.abs(z)))
    a = jnp.exp((-LRU_C) * r * softplus)
    y = 1.0 - a * a
    u = jnp.where(y > 0.0, y * lax.rsqrt(y), 0.0) * (gate_i * xc)

    end, decay = slab(u, 0), slab(a, 0)
    for p in range(1, npos):
        ap = slab(a, p)
        end = ap * end + slab(u, p)
        decay = ap * decay
    inc = pltpu.roll(jnp.where(sub == SUBLANES - 1, hend_ref[...], end), 1, 0)
    dec = jnp.where(sub == 0, 0.0, pltpu.roll(decay, 1, 0))
    d = 1
    while d < SUBLANES:
        keep = sub >= d
        inc = jnp.where(keep, dec * pltpu.roll(inc, d, 0) + inc, inc)
        dec = jnp.where(keep, dec * pltpu.roll(dec, d, 0), dec)
        d *= 2
    state = inc
    hs = []
    for p in range(npos):
        state = slab(a, p) * state + slab(u, p)
        hs.append(state)
    hend_ref[...] = state
    return (_gelu_tanh(gate) * jnp.concatenate(hs, axis=0)).astype(BF16)


def _inproj_kernel(qk_scale, steps_per_seq, x_ref, g_ref, w_ref, p_ref, pt_ref, cw_ref, cb_ref, wg_ref,
                   bg_ref, lam_ref, oa_ref, k_ref, qt_ref, vt_ref, wt_ref, lx_ref, lg_ref, xtail_ref, hend_ref):
    i = pl.program_id(0)

    tm, width = oa_ref.shape

    @pl.when(i == 0)
    def _():
        lx_ref[...] = jnp.zeros_like(lx_ref)
        lg_ref[...] = jnp.zeros_like(lg_ref)
        for j, col in enumerate((2, 4)):
            wcols = w_ref[:, col * width:(col + 1) * width].astype(F32)
            wt_ref[j * width:(j + 1) * width, :] = wcols.T.astype(wt_ref.dtype)

    @pl.when((i + steps_per_seq - 1) % steps_per_seq == 0)
    def _():
        xtail_ref[...] = jnp.zeros_like(xtail_ref)
        hend_ref[...] = jnp.zeros_like(hend_ref)

    lx_prev = lx_ref[...]
    lg_prev = lg_ref[...]
    tile = p_ref.shape[0]
    assert tm == 2 * tile, "the interleaving below is written for two scan tiles per step"
    rows = (slice(0, tile), slice(tile, tm))
    nt = (((1,), (1,)), ((), ()))
    unperm = lambda out: jnp.dot(pt_ref[...], out, preferred_element_type=F32).astype(oa_ref.dtype)

    h = _rms(x_ref[...], g_ref[...], EPS).astype(BF16)
    xc0, gate0 = _lru_conv(lx_prev[rows[0]], lg_prev[rows[0]], p_ref, cw_ref, cb_ref, xtail_ref)
    lx_ref[...] = jnp.dot(h, w_ref[:, :width], preferred_element_type=F32).astype(BF16)
    pre0 = _lru_gates(xc0, wg_ref, bg_ref)
    xc1, gate1 = _lru_conv(lx_prev[rows[1]], lg_prev[rows[1]], p_ref, cw_ref, cb_ref, xtail_ref)
    lg_ref[...] = jnp.dot(h, w_ref[:, width:2 * width], preferred_element_type=F32).astype(BF16)
    pre1 = _lru_gates(xc1, wg_ref, bg_ref)
    out0 = _lru_scan(xc0, gate0, *pre0, lam_ref, hend_ref)
    k_ref[...] = jnp.dot(h, w_ref[:, 3 * width:4 * width], preferred_element_type=F32).astype(k_ref.dtype)
    oa_ref[rows[0], :] = unperm(out0)
    out1 = _lru_scan(xc1, gate1, *pre1, lam_ref, hend_ref)
    qt = lax.dot_general(wt_ref[:width, :], h, nt, preferred_element_type=F32)
    qt_ref[...] = (qt * qk_scale).astype(qt_ref.dtype)
    oa_ref[rows[1], :] = unperm(out1)
    vt = lax.dot_general(wt_ref[width:, :], h, nt, preferred_element_type=F32)
    vt_ref[...] = vt.astype(vt_ref.dtype)


def _inproj(x2, g, w, conv_w, conv_b, w_gates, b_gates, lam, bsz, seq, tm, lru_tile, qk_scale):
    n, d = x2.shape
    width = w.shape[1] // 5
    spb = seq // tm
    last = n // tm - 1
    perm = _segment_perm(lru_tile)
    p_mat, pt_mat = jnp.asarray(perm, BF16), jnp.asarray(perm.T, BF16)
    consts = (g, w, p_mat, pt_mat, conv_w, conv_b, w_gates, b_gates, lam)
    proj = lambda i: jnp.minimum(i, last)
    col_spec = pl.BlockSpec((None, width, tm), lambda i: (proj(i) // spb, 0, proj(i) % spb))
    return pl.pallas_call(
        functools.partial(_inproj_kernel, qk_scale, spb),
        grid=(n // tm + 1,),
        in_specs=[pl.BlockSpec((tm, d), lambda i: (proj(i), 0))] + [_const_spec(c.shape) for c in consts],
        out_specs=[pl.BlockSpec((tm, width), lambda i: (jnp.maximum(i - 1, 0), 0)),
                   pl.BlockSpec((tm, width), lambda i: (proj(i), 0)), col_spec, col_spec],
        out_shape=[jax.ShapeDtypeStruct((n, width), BF16), jax.ShapeDtypeStruct((n, width), BF16),
                   jax.ShapeDtypeStruct((bsz, width, seq), BF16), jax.ShapeDtypeStruct((bsz, width, seq), BF16)],
        scratch_shapes=[pltpu.VMEM((2 * width, d), BF16), pltpu.VMEM((tm, width), BF16), pltpu.VMEM((tm, width), BF16),
                        pltpu.VMEM(((LRU_CONV - 1) * SUBLANES, width), F32), pltpu.VMEM((SUBLANES, width), F32)],
        compiler_params=pltpu.CompilerParams(dimension_semantics=("arbitrary",), vmem_limit_bytes=VMEM_LIMIT),
        name="inproj_lru",
    )(x2, *consts)


def _rel_bucket_np(rel):
    half = REL_BUCKETS // 2
    max_exact = half // 2
    ret = (rel > 0).astype(np.int32) * half
    n = np.abs(rel)
    nf = np.maximum(n, 1).astype(np.float32)
    frac = np.log(nf / np.float32(max_exact)) / np.float32(math.log(REL_MAX_DIST / max_exact))
    large = max_exact + (frac * np.float32(half - max_exact)).astype(np.int32)
    large = np.minimum(large, half - 1)
    return ret + np.where(n < max_exact, n, large)


def _bucket_tile(tq):
    ql = np.arange(tq)[None, :]
    kpos = np.arange(2 * tq)[:, None] - tq
    bucket = _rel_bucket_np(kpos - ql).astype(np.int32)
    allowed = (kpos // CHUNK) <= (ql // CHUNK)
    return np.where(allowed, bucket, -1).astype(np.int32)


def _far_bucket(tq, seq):
    far = _rel_bucket_np(-np.arange(tq + 1, seq))
    assert (far == far[0]).all()
    return int(far[0])


def _bias_kernel(rb_ref, bucket_ref, o_ref):
    h = pl.program_id(0)
    bucket = bucket_ref[...]
    acc = jnp.zeros(bucket.shape, F32)
    for b in range(REL_BUCKETS):
        acc = jnp.where(bucket == b, rb_ref[b, h], acc)
    tile = jnp.where(bucket < 0, NEG_INF, acc) * LOG2E
    o_ref[...] = jnp.concatenate([tile, tile], axis=1)


def _bias_tiles(rel_bias, tq):
    bucket = jnp.asarray(_bucket_tile(tq))
    tk2 = bucket.shape[0]
    return pl.pallas_call(
        _bias_kernel,
        grid=(DIFF_HEADS,),
        in_specs=[pl.BlockSpec(memory_space=pltpu.SMEM), pl.BlockSpec((tk2, tq), lambda h: (0, 0))],
        out_specs=pl.BlockSpec((None, tk2, 2 * tq), lambda h: (h, 0, 0)),
        out_shape=jax.ShapeDtypeStruct((DIFF_HEADS, tk2, 2 * tq), F32),
        name="relbias",
    )(rel_bias, bucket)


def _attn_kernel(far_bucket, tq, rb_ref, qt_ref, k_ref, vt_ref, bias_ref, lq1_ref, lk1_ref, lq2_ref, lk2_ref,
                 sg_ref, o_ref):
    h = pl.program_id(1)
    hd, seq = qt_ref.shape
    qk = hd // 2
    far_bias = rb_ref[far_bucket, h] * LOG2E
    lam = (jnp.exp(jnp.sum(lq1_ref[...] * lk1_ref[...], axis=1, keepdims=True))
           - jnp.exp(jnp.sum(lq2_ref[...] * lk2_ref[...], axis=1, keepdims=True)) + LAMBDA_INIT)
    sub = lax.broadcasted_iota(jnp.int32, (hd, 2 * tq), 0)
    lane = lax.broadcasted_iota(jnp.int32, (hd, 2 * tq), 1)
    own_rows = (sub < qk) == (lane < tq)
    near_lo = lambda i: max(i - 1, 0) * tq
    ones = jnp.ones((BF16_ROWS, seq), BF16)
    values = lambda lo, hi: jnp.concatenate([vt_ref[:, lo:hi], ones[:, lo:hi]], axis=0)

    def logits(i):
        qt = qt_ref[:, i * tq:(i + 1) * tq]
        qt = jnp.concatenate([qt, qt], axis=1)
        qtc = jnp.where(own_rows, qt, jnp.zeros_like(qt))
        bias = bias_ref[...] if i > 0 else bias_ref[tq:, :]
        s_near = jnp.dot(k_ref[near_lo(i):(i + 1) * tq, :], qtc, preferred_element_type=F32) + bias
        s_far = jnp.dot(k_ref[:near_lo(i), :], qtc, preferred_element_type=F32) if near_lo(i) else None
        return s_near, s_far

    def softmax_pv(i, s_near, s_far):
        m = jnp.max(s_near, axis=0, keepdims=True)
        if s_far is not None:
            m = jnp.maximum(m, jnp.max(s_far, axis=0, keepdims=True) + far_bias)
        pv = None
        pieces = [(s_near, near_lo(i), m)] + ([(s_far, 0, m - far_bias)] if s_far is not None else [])
        for s, lo, shift in pieces:
            for r in range(0, s.shape[0], KEY_BLOCK):
                e = min(r + KEY_BLOCK, s.shape[0])
                p = jnp.exp2(s[r:e] - shift).astype(BF16)
                d = jnp.dot(values(lo + r, lo + e), p, preferred_element_type=F32)
                pv = d if pv is None else pv + d
        return pv[:hd] * (1.0 / pv[hd:hd + 1])

    nblk = seq // tq
    pending = [logits(i) for i in range(min(LOOKAHEAD, nblk))]
    for i in range(nblk):
        current = pending.pop(0)
        if i + LOOKAHEAD < nblk:
            pending.append(logits(i + LOOKAHEAD))
        maps = softmax_pv(i, *current)
        out = maps[:, :tq] - lam * maps[:, tq:]
        out = out * lax.rsqrt(jnp.mean(out * out, axis=0, keepdims=True) + SUBLN_EPS)
        o_ref[i * tq:(i + 1) * tq, :] = (out.T * sg_ref[...] * (1.0 - LAMBDA_INIT)).astype(o_ref.dtype)


def _attn(qt, k, vt, bias, rel_bias, lq1, lk1, lq2, lk2, subln_g, tq):
    bsz, seq, width = k.shape
    hd = width // DIFF_HEADS
    far_bucket = _far_bucket(tq, seq)
    col_spec = pl.BlockSpec((None, hd, seq), lambda b, h: (b, h, 0))
    row_spec = pl.BlockSpec((None, seq, hd), lambda b, h: (b, 0, h))
    return pl.pallas_call(
        functools.partial(_attn_kernel, far_bucket, tq),
        grid=(bsz, DIFF_HEADS),
        in_specs=[pl.BlockSpec(memory_space=pltpu.SMEM), col_spec, row_spec, col_spec,
                  pl.BlockSpec((None, 2 * tq, 2 * tq), lambda b, h: (h, 0, 0)),
                  _const_spec(lq1.shape), _const_spec(lk1.shape), _const_spec(lq2.shape),
                  _const_spec(lk2.shape), _const_spec(subln_g.shape)],
        out_specs=row_spec,
        out_shape=jax.ShapeDtypeStruct((bsz, seq, width), BF16),
        compiler_params=pltpu.CompilerParams(dimension_semantics=("arbitrary",) * 2,
                                             vmem_limit_bytes=VMEM_LIMIT),
        name="diffattn",
    )(rel_bias, qt, k, vt, bias, lq1, lk1, lq2, lk2, subln_g)


def _ffn_kernel(steps_per_seq, n_chunks, x_ref, a_ref, b_ref, wo_ref, g2_ref, wu_ref, cw_ref, cb_ref,
                wd_ref, gf_ref, o_ref, tail_ref):
    tm = x_ref.shape[0]
    half = a_ref.shape[1]
    d_ff = wd_ref.shape[0]
    fc = d_ff // n_chunks

    @pl.when(pl.program_id(0) % steps_per_seq == 0)
    def _():
        tail_ref[...] = jnp.zeros_like(tail_ref)

    def rows_dot(lhs, rhs):
        parts = [jnp.dot(lhs[r:r + MATMUL_ROWS], rhs, preferred_element_type=F32)
                 for r in range(0, tm, MATMUL_ROWS)]
        return jnp.concatenate(parts, axis=0)

    x1 = x_ref[...] + (rows_dot(a_ref[...], wo_ref[:half, :]) + rows_dot(b_ref[...], wo_ref[half:, :]))
    h2 = _rms(x1, g2_ref[...], EPS).astype(BF16)

    def up(c):
        gate = rows_dot(h2, wu_ref[:, c * fc:(c + 1) * fc])
        val = rows_dot(h2, wu_ref[:, d_ff + c * fc:d_ff + (c + 1) * fc])
        return gate, val

    def activate(c, gate, val):
        cols = slice(c * fc, (c + 1) * fc)
        ext = jnp.concatenate([tail_ref[:, cols], gate], axis=0)
        conv = cb_ref[:, cols] + cw_ref[FFN_CONV - 1:FFN_CONV, cols] * gate
        for s in range(1, FFN_CONV):
            conv = conv + cw_ref[FFN_CONV - 1 - s:FFN_CONV - s, cols] * _shift_rows(ext, s)
        tail_ref[:, cols] = gate[tm - SUBLANES:]
        return (_gelu_tanh(conv) * val).astype(BF16)

    ffn = None
    pending = up(0)
    for c in range(n_chunks):
        current = pending
        if c + 1 < n_chunks:
            pending = up(c + 1)
        act = activate(c, *current)
        down = rows_dot(act, wd_ref[c * fc:(c + 1) * fc, :])
        ffn = down if ffn is None else ffn + down
    o_ref[...] = _rms(x1 + ffn, gf_ref[...], EPS)


def _ffn(x2, out_a, out_b, w_out, g2, w_up, conv_w, conv_b, w_down, gf, seq, tm, n_chunks):
    n, d = x2.shape
    half = out_a.shape[1]
    d_ff = w_down.shape[0]
    return pl.pallas_call(
        functools.partial(_ffn_kernel, seq // tm, n_chunks),
        grid=(n // tm,),
        in_specs=[pl.BlockSpec((tm, d), lambda i: (i, 0)),
                  pl.BlockSpec((tm, half), lambda i: (i, 0)),
                  pl.BlockSpec((tm, half), lambda i: (i, 0)),
                  _const_spec(w_out.shape), _const_spec(g2.shape), _const_spec(w_up.shape),
                  _const_spec(conv_w.shape), _const_spec(conv_b.shape), _const_spec(w_down.shape),
                  _const_spec(gf.shape)],
        out_specs=pl.BlockSpec((tm, d), lambda i: (i, 0)),
        out_shape=jax.ShapeDtypeStruct((n, d), F32),
        scratch_shapes=[pltpu.VMEM((SUBLANES, d_ff), F32)],
        compiler_params=pltpu.CompilerParams(dimension_semantics=("arbitrary",), vmem_limit_bytes=VMEM_LIMIT),
        name="outproj_ffn",
    )(x2, out_a, out_b, w_out, g2, w_up, conv_w, conv_b, w_down, gf)


def _gate_weights(wa, wx, groups):
    nb, bd, _ = wa.shape
    per = nb // groups
    w = jnp.stack([wa, wx]).reshape(2, groups, per, bd, bd)
    dense = jnp.einsum("tjbio,bc->jbitco", w, jnp.eye(per, dtype=w.dtype))
    return dense.reshape(groups, per * bd, 2 * per * bd)


def kernel(x, norm1_g, w_in, lru_conv_w, lru_conv_b, lru_wa, lru_ba, lru_wx, lru_bx, lru_lambda, diff_lq1, diff_lk1, diff_lq2, diff_lk2, diff_subln_g, rel_bias, w_out, norm2_g, ffn_w_up, ffn_conv_w, ffn_conv_b, ffn_w_down, final_norm_g):
    bsz, seq, d = x.shape
    assert w_in.shape[0] == 1, "single-layer block"
    assert seq % ROW_TILE == 0 and ROW_TILE == 2 * LRU_TILE and ROW_TILE % MATMUL_ROWS == 0
    assert seq % QUERY_BLOCK == 0 and QUERY_BLOCK % CHUNK == 0 and KEY_BLOCK % QUERY_BLOCK == 0
    n = bsz * seq
    x2 = x.reshape(n, d)
    row = lambda p: p.reshape(1, -1)

    lru_w = lru_wa.shape[1] * lru_wa.shape[2]
    attn_w = (w_in.shape[2] - 2 * lru_w) // 3
    qk_dim = attn_w // DIFF_HEADS // 2
    tq = QUERY_BLOCK
    assert attn_w == lru_w, "the projection kernel slices five equal column groups"
    w_gates = _gate_weights(lru_wa[0], lru_wx[0], LRU_GATE_GROUPS).astype(BF16)
    b_gates = jnp.stack([lru_ba[0].reshape(lru_w), lru_bx[0].reshape(lru_w)])
    out_a, k, qt, vt = _inproj(x2, row(norm1_g[0]), w_in[0].astype(BF16), lru_conv_w[0], row(lru_conv_b[0]),
                               w_gates, b_gates, row(lru_lambda[0]), bsz, seq, tm=ROW_TILE, lru_tile=LRU_TILE,
                               qk_scale=qk_dim ** -0.5 * LOG2E)

    bias = _bias_tiles(rel_bias, tq)
    out_b = _attn(qt, k.reshape(bsz, seq, attn_w), vt, bias, rel_bias,
                  row(diff_lq1[0]), row(diff_lk1[0]), row(diff_lq2[0]), row(diff_lk2[0]),
                  row(diff_subln_g[0]), tq)

    out = _ffn(x2, out_a, out_b.reshape(n, -1), w_out[0].astype(BF16), row(norm2_g[0]),
               ffn_w_up[0].astype(BF16), ffn_conv_w[0], row(ffn_conv_b[0]), ffn_w_down[0].astype(BF16),
               row(final_norm_g), seq, tm=ROW_TILE, n_chunks=FFN_CHUNKS)
    return out.reshape(bsz, seq, d)
```

```python
import functools
import math

import jax
import jax.numpy as jnp
import numpy as np
from jax import lax
from jax.experimental import pallas as pl
from jax.experimental.pallas import tpu as pltpu

CHUNK = 64
LRU_CONV = 4
LRU_C = 8.0
LRU_GATE_GROUPS = 2
DIFF_HEADS = 4
REL_BUCKETS = 32
REL_MAX_DIST = 128
FFN_CONV = 3
EPS = 1e-6
SUBLN_EPS = 1e-5
NEG_INF = -1e30
LAMBDA_INIT = 0.8 - 0.6 * math.exp(-0.3 * 0)

LOG2E = math.log2(math.e)

ROW_TILE = 512
LRU_TILE = 256
QUERY_BLOCK = 128
KEY_BLOCK = 256
LOOKAHEAD = 6
MATMUL_ROWS = 256
FFN_CHUNKS = 1

SUBLANES = 8
BF16_ROWS = 16
VMEM_LIMIT = 56 * 1024 * 1024

BF16 = jnp.bfloat16
F32 = jnp.float32


def _rms(x, g, eps):
    return x * lax.rsqrt(jnp.mean(x * x, axis=-1, keepdims=True) + eps) * g


def _gelu_tanh(x):
    inner = math.sqrt(2.0 / math.pi) * (x + 0.044715 * (x * x * x))
    return 0.5 * x * (1.0 + jnp.tanh(inner))


def _const_spec(shape):
    nd = len(shape)
    return pl.BlockSpec(shape, lambda *_: (0,) * nd, pipeline_mode=pl.Buffered(1))


def _shift_rows(ext, s):
    return pltpu.roll(ext, s, 0)[SUBLANES:]


def _segment_perm(tile):
    seg_len = tile // SUBLANES
    r = np.arange(tile)
    perm = np.zeros((tile, tile), np.float32)
    perm[r, (r % SUBLANES) * seg_len + r // SUBLANES] = 1.0
    return perm


def _lru_conv(lx, lg, p_ref, cw_ref, cb_ref, xtail_ref):
    tile, width = lx.shape
    npos = tile // SUBLANES
    ntap = LRU_CONV - 1
    sub = lax.broadcasted_iota(jnp.int32, (SUBLANES, width), 0)
    slab = lambda v, p: v[p * SUBLANES:(p + 1) * SUBLANES]
    x = jnp.dot(p_ref[...], lx, preferred_element_type=F32)
    gate = jnp.dot(p_ref[...], lg, preferred_element_type=F32)
    wrapped = []
    for q in range(ntap):
        cur = slab(x, npos - ntap + q)
        prev = slab(xtail_ref[...], q)
        wrapped.append(pltpu.roll(jnp.where(sub == SUBLANES - 1, prev, cur), 1, 0))
    xtail_ref[...] = x[tile - ntap * SUBLANES:]
    xc = cb_ref[...] + cw_ref[LRU_CONV - 1:LRU_CONV, :] * x
    for s in range(1, LRU_CONV):
        shifted = jnp.concatenate(wrapped[ntap - s:] + [x[:tile - SUBLANES * s]], axis=0)
        xc = xc + cw_ref[LRU_CONV - 1 - s:LRU_CONV - s, :] * shifted
    return xc, gate


def _lru_gates(xc, wg_ref, bg_ref):
    xcb = xc.astype(BF16)
    ngrp, gw, _ = wg_ref.shape
    pres = [jnp.dot(xcb[:, j * gw:(j + 1) * gw], wg_ref[j], preferred_element_type=F32) for j in range(ngrp)]
    pre_r = jnp.concatenate([pj[:, :gw] for pj in pres], axis=1) + bg_ref[0:1, :]
    pre_i = jnp.concatenate([pj[:, gw:] for pj in pres], axis=1) + bg_ref[1:2, :]
    return pre_r, pre_i


def _lru_scan(xc, gate, pre_r, pre_i, lam_ref, hend_ref):
    tile, width = xc.shape
    npos = tile // SUBLANES
    sub = lax.broadcasted_iota(jnp.int32, (SUBLANES, width), 0)
    slab = lambda v, p: v[p * SUBLANES:(p + 1) * SUBLANES]
    r = jax.nn.sigmoid(pre_r)
    gate_i = jax.nn.sigmoid(pre_i)
    z = -lam_ref[...]
    softplus = jnp.maximum(z, 0.0) + jnp.log(1.0 + jnp.exp(-jnp.abs(z)))
    a = jnp.exp((-LRU_C) * r * softplus)
    y = 1.0 - a * a
    u = jnp.where(y > 0.0, y * lax.rsqrt(y), 0.0) * (gate_i * xc)

    end, decay = slab(u, 0), slab(a, 0)
    for p in range(1, npos):
        ap = slab(a, p)
        end = ap * end + slab(u, p)
        decay = ap * decay
    inc = pltpu.roll(jnp.where(sub == SUBLANES - 1, hend_ref[...], end), 1, 0)
    dec = jnp.where(sub == 0, 0.0, pltpu.roll(decay, 1, 0))
    d = 1
    while d < SUBLANES:
        keep = sub >= d
        inc = jnp.where(keep, dec * pltpu.roll(inc, d, 0) + inc, inc)
        dec = jnp.where(keep, dec * pltpu.roll(dec, d, 0), dec)
        d *= 2
    state = inc
    hs = []
    for p in range(npos):
        state = slab(a, p) * state + slab(u, p)
        hs.append(state)
    hend_ref[...] = state
    return (_gelu_tanh(gate) * jnp.concatenate(hs, axis=0)).astype(BF16)


def _inproj_kernel(qk_scale, steps_per_seq, x_ref, g_ref, w_ref, p_ref, pt_ref, cw_ref, cb_ref, wg_ref,
                   bg_ref, lam_ref, oa_ref, k_ref, qt_ref, vt_ref, wb_ref, wt_ref, lx_ref, lg_ref, xtail_ref, hend_ref):
    i = pl.program_id(0)

    tm, width = oa_ref.shape

    @pl.when(i == 0)
    def _():
        lx_ref[...] = jnp.zeros_like(lx_ref)
        lg_ref[...] = jnp.zeros_like(lg_ref)
        for j, col in enumerate((0, 1, 3)):
            wb_ref[:, j * width:(j + 1) * width] = w_ref[:, col * width:(col + 1) * width].astype(wb_ref.dtype)
        for j, col in enumerate((2, 4)):
            wt_ref[j * width:(j + 1) * width, :] = w_ref[:, col * width:(col + 1) * width].T.astype(wt_ref.dtype)

    @pl.when((i + steps_per_seq - 1) % steps_per_seq == 0)
    def _():
        xtail_ref[...] = jnp.zeros_like(xtail_ref)
        hend_ref[...] = jnp.zeros_like(hend_ref)

    lx_prev = lx_ref[...]
    lg_prev = lg_ref[...]
    tile = p_ref.shape[0]
    assert tm == 2 * tile, "the interleaving below is written for two scan tiles per step"
    rows = (slice(0, tile), slice(tile, tm))
    nt = (((1,), (1,)), ((), ()))
    unperm = lambda out: jnp.dot(pt_ref[...], out, preferred_element_type=F32).astype(oa_ref.dtype)

    h = _rms(x_ref[...], g_ref[...], EPS).astype(BF16)
    xc0, gate0 = _lru_conv(lx_prev[rows[0]], lg_prev[rows[0]], p_ref, cw_ref, cb_ref, xtail_ref)
    lx_ref[...] = jnp.dot(h, wb_ref[:, :width], preferred_element_type=F32).astype(BF16)
    pre0 = _lru_gates(xc0, wg_ref, bg_ref)
    xc1, gate1 = _lru_conv(lx_prev[rows[1]], lg_prev[rows[1]], p_ref, cw_ref, cb_ref, xtail_ref)
    lg_ref[...] = jnp.dot(h, wb_ref[:, width:2 * width], preferred_element_type=F32).astype(BF16)
    pre1 = _lru_gates(xc1, wg_ref, bg_ref)
    out0 = _lru_scan(xc0, gate0, *pre0, lam_ref, hend_ref)
    k_ref[...] = jnp.dot(h, wb_ref[:, 2 * width:], preferred_element_type=F32).astype(k_ref.dtype)
    oa_ref[rows[0], :] = unperm(out0)
    out1 = _lru_scan(xc1, gate1, *pre1, lam_ref, hend_ref)
    qt = lax.dot_general(wt_ref[:width, :], h, nt, preferred_element_type=F32)
    qt_ref[...] = (qt * qk_scale).astype(qt_ref.dtype)
    oa_ref[rows[1], :] = unperm(out1)
    vt = lax.dot_general(wt_ref[width:, :], h, nt, preferred_element_type=F32)
    vt_ref[...] = vt.astype(vt_ref.dtype)


def _inproj(x2, g, w, conv_w, conv_b, w_gates, b_gates, lam, bsz, seq, tm, lru_tile, qk_scale):
    n, d = x2.shape
    width = w.shape[1] // 5
    spb = seq // tm
    last = n // tm - 1
    perm = _segment_perm(lru_tile)
    p_mat, pt_mat = jnp.asarray(perm, BF16), jnp.asarray(perm.T, BF16)
    consts = (g, w, p_mat, pt_mat, conv_w, conv_b, w_gates, b_gates, lam)
    proj = lambda i: jnp.minimum(i, last)
    col_spec = pl.BlockSpec((None, width, tm), lambda i: (proj(i) // spb, 0, proj(i) % spb))
    return pl.pallas_call(
        functools.partial(_inproj_kernel, qk_scale, spb),
        grid=(n // tm + 1,),
        in_specs=[pl.BlockSpec((tm, d), lambda i: (proj(i), 0))] + [_const_spec(c.shape) for c in consts],
        out_specs=[pl.BlockSpec((tm, width), lambda i: (jnp.maximum(i - 1, 0), 0)),
                   pl.BlockSpec((tm, width), lambda i: (proj(i), 0)), col_spec, col_spec],
        out_shape=[jax.ShapeDtypeStruct((n, width), BF16), jax.ShapeDtypeStruct((n, width), BF16),
                   jax.ShapeDtypeStruct((bsz, width, seq), BF16), jax.ShapeDtypeStruct((bsz, width, seq), BF16)],
        scratch_shapes=[pltpu.VMEM((d, 3 * width), BF16), pltpu.VMEM((2 * width, d), BF16),
                        pltpu.VMEM((tm, width), BF16), pltpu.VMEM((tm, width), BF16),
                        pltpu.VMEM(((LRU_CONV - 1) * SUBLANES, width), F32), pltpu.VMEM((SUBLANES, width), F32)],
        compiler_params=pltpu.CompilerParams(dimension_semantics=("arbitrary",), vmem_limit_bytes=VMEM_LIMIT),
        name="inproj_lru",
    )(x2, *consts)


def _rel_bucket_np(rel):
    half = REL_BUCKETS // 2
    max_exact = half // 2
    ret = (rel > 0).astype(np.int32) * half
    n = np.abs(rel)
    nf = np.maximum(n, 1).astype(np.float32)
    frac = np.log(nf / np.float32(max_exact)) / np.float32(math.log(REL_MAX_DIST / max_exact))
    large = max_exact + (frac * np.float32(half - max_exact)).astype(np.int32)
    large = np.minimum(large, half - 1)
    return ret + np.where(n < max_exact, n, large)


def _bucket_tile(tq):
    ql = np.arange(tq)[None, :]
    kpos = np.arange(2 * tq)[:, None] - tq
    bucket = _rel_bucket_np(kpos - ql).astype(np.int32)
    allowed = (kpos // CHUNK) <= (ql // CHUNK)
    return np.where(allowed, bucket, -1).astype(np.int32)


def _far_bucket(tq, seq):
    far = _rel_bucket_np(-np.arange(tq + 1, seq))
    assert (far == far[0]).all()
    return int(far[0])


def _bias_kernel(rb_ref, bucket_ref, o_ref):
    h = pl.program_id(0)
    bucket = bucket_ref[...]
    acc = jnp.zeros(bucket.shape, F32)
    for b in range(REL_BUCKETS):
        acc = jnp.where(bucket == b, rb_ref[b, h], acc)
    tile = jnp.where(bucket < 0, NEG_INF, acc) * LOG2E
    o_ref[...] = jnp.concatenate([tile, tile], axis=1)


def _bias_tiles(rel_bias, tq):
    bucket = jnp.asarray(_bucket_tile(tq))
    tk2 = bucket.shape[0]
    return pl.pallas_call(
        _bias_kernel,
        grid=(DIFF_HEADS,),
        in_specs=[pl.BlockSpec(memory_space=pltpu.SMEM), pl.BlockSpec((tk2, tq), lambda h: (0, 0))],
        out_specs=pl.BlockSpec((None, tk2, 2 * tq), lambda h: (h, 0, 0)),
        out_shape=jax.ShapeDtypeStruct((DIFF_HEADS, tk2, 2 * tq), F32),
        name="relbias",
    )(rel_bias, bucket)


def _attn_kernel(far_bucket, tq, rb_ref, qt_ref, k_ref, vt_ref, bias_ref, lq1_ref, lk1_ref, lq2_ref, lk2_ref,
                 sg_ref, o_ref):
    h = pl.program_id(1)
    hd, seq = qt_ref.shape
    qk = hd // 2
    far_bias = rb_ref[far_bucket, h] * LOG2E
    lam = (jnp.exp(jnp.sum(lq1_ref[...] * lk1_ref[...], axis=1, keepdims=True))
           - jnp.exp(jnp.sum(lq2_ref[...] * lk2_ref[...], axis=1, keepdims=True)) + LAMBDA_INIT)
    sub = lax.broadcasted_iota(jnp.int32, (hd, 2 * tq), 0)
    lane = lax.broadcasted_iota(jnp.int32, (hd, 2 * tq), 1)
    own_rows = (sub < qk) == (lane < tq)
    near_lo = lambda i: max(i - 1, 0) * tq
    ones = jnp.ones((BF16_ROWS, seq), BF16)
    values = lambda lo, hi: jnp.concatenate([vt_ref[:, lo:hi], ones[:, lo:hi]], axis=0)

    def logits(i):
        qt = qt_ref[:, i * tq:(i + 1) * tq]
        qt = jnp.concatenate([qt, qt], axis=1)
        qtc = jnp.where(own_rows, qt, jnp.zeros_like(qt))
        bias = bias_ref[...] if i > 0 else bias_ref[tq:, :]
        s_near = jnp.dot(k_ref[near_lo(i):(i + 1) * tq, :], qtc, preferred_element_type=F32) + bias
        s_far = jnp.dot(k_ref[:near_lo(i), :], qtc, preferred_element_type=F32) if near_lo(i) else None
        return s_near, s_far

    def softmax_pv(i, s_near, s_far):
        m = jnp.max(s_near, axis=0, keepdims=True)
        if s_far is not None:
            m = jnp.maximum(m, jnp.max(s_far, axis=0, keepdims=True) + far_bias)
        pv = None
        pieces = [(s_near, near_lo(i), m)] + ([(s_far, 0, m - far_bias)] if s_far is not None else [])
        for s, lo, shift in pieces:
            for r in range(0, s.shape[0], KEY_BLOCK):
                e = min(r + KEY_BLOCK, s.shape[0])
                p = jnp.exp2(s[r:e] - shift).astype(BF16)
                d = jnp.dot(values(lo + r, lo + e), p, preferred_element_type=F32)
                pv = d if pv is None else pv + d
        return pv[:hd] * (1.0 / pv[hd:hd + 1])

    nblk = seq // tq
    pending = [logits(i) for i in range(min(LOOKAHEAD, nblk))]
    for i in range(nblk):
        current = pending.pop(0)
        if i + LOOKAHEAD < nblk:
            pending.append(logits(i + LOOKAHEAD))
        maps = softmax_pv(i, *current)
        out = maps[:, :tq] - lam * maps[:, tq:]
        out = out * lax.rsqrt(jnp.mean(out * out, axis=0, keepdims=True) + SUBLN_EPS)
        o_ref[i * tq:(i + 1) * tq, :] = (out.T * sg_ref[...] * (1.0 - LAMBDA_INIT)).astype(o_ref.dtype)


def _attn(qt, k, vt, bias, rel_bias, lq1, lk1, lq2, lk2, subln_g, tq):
    bsz, seq, width = k.shape
    hd = width // DIFF_HEADS
    far_bucket = _far_bucket(tq, seq)
    col_spec = pl.BlockSpec((None, hd, seq), lambda b, h: (b, h, 0))
    row_spec = pl.BlockSpec((None, seq, hd), lambda b, h: (b, 0, h))
    return pl.pallas_call(
        functools.partial(_attn_kernel, far_bucket, tq),
        grid=(bsz, DIFF_HEADS),
        in_specs=[pl.BlockSpec(memory_space=pltpu.SMEM), col_spec, row_spec, col_spec,
                  pl.BlockSpec((None, 2 * tq, 2 * tq), lambda b, h: (h, 0, 0)),
                  _const_spec(lq1.shape), _const_spec(lk1.shape), _const_spec(lq2.shape),
                  _const_spec(lk2.shape), _const_spec(subln_g.shape)],
        out_specs=row_spec,
        out_shape=jax.ShapeDtypeStruct((bsz, seq, width), BF16),
        compiler_params=pltpu.CompilerParams(dimension_semantics=("arbitrary",) * 2,
                                             vmem_limit_bytes=VMEM_LIMIT),
        name="diffattn",
    )(rel_bias, qt, k, vt, bias, lq1, lk1, lq2, lk2, subln_g)


def _ffn_kernel(steps_per_seq, n_chunks, x_ref, a_ref, b_ref, wo_ref, g2_ref, wu_ref, cw_ref, cb_ref,
                wd_ref, gf_ref, o_ref, tail_ref):
    tm = x_ref.shape[0]
    half = a_ref.shape[1]
    d_ff = wd_ref.shape[0]
    fc = d_ff // n_chunks

    @pl.when(pl.program_id(0) % steps_per_seq == 0)
    def _():
        tail_ref[...] = jnp.zeros_like(tail_ref)

    def rows_dot(lhs, rhs):
        parts = [jnp.dot(lhs[r:r + MATMUL_ROWS], rhs, preferred_element_type=F32)
                 for r in range(0, tm, MATMUL_ROWS)]
        return jnp.concatenate(parts, axis=0)

    x1 = x_ref[...] + (rows_dot(a_ref[...], wo_ref[:half, :]) + rows_dot(b_ref[...], wo_ref[half:, :]))
    h2 = _rms(x1, g2_ref[...], EPS).astype(BF16)

    def up(c):
        gate = rows_dot(h2, wu_ref[:, c * fc:(c + 1) * fc])
        val = rows_dot(h2, wu_ref[:, d_ff + c * fc:d_ff + (c + 1) * fc])
        return gate, val

    def activate(c, gate, val):
        cols = slice(c * fc, (c + 1) * fc)
        ext = jnp.concatenate([tail_ref[:, cols], gate], axis=0)
        conv = cb_ref[:, cols] + cw_ref[FFN_CONV - 1:FFN_CONV, cols] * gate
        for s in range(1, FFN_CONV):
            conv = conv + cw_ref[FFN_CONV - 1 - s:FFN_CONV - s, cols] * _shift_rows(ext, s)
        tail_ref[:, cols] = gate[tm - SUBLANES:]
        return (_gelu_tanh(conv) * val).astype(BF16)

    ffn = None
    pending = up(0)
    for c in range(n_chunks):
        current = pending
        if c + 1 < n_chunks:
            pending = up(c + 1)
        act = activate(c, *current)
        down = rows_dot(act, wd_ref[c * fc:(c + 1) * fc, :])
        ffn = down if ffn is None else ffn + down
    o_ref[...] = _rms(x1 + ffn, gf_ref[...], EPS)


def _ffn(x2, out_a, out_b, w_out, g2, w_up, conv_w, conv_b, w_down, gf, seq, tm, n_chunks):
    n, d = x2.shape
    half = out_a.shape[1]
    d_ff = w_down.shape[0]
    return pl.pallas_call(
        functools.partial(_ffn_kernel, seq // tm, n_chunks),
        grid=(n // tm,),
        in_specs=[pl.BlockSpec((tm, d), lambda i: (i, 0)),
                  pl.BlockSpec((tm, half), lambda i: (i, 0)),
                  pl.BlockSpec((tm, half), lambda i: (i, 0)),
                  _const_spec(w_out.shape), _const_spec(g2.shape), _const_spec(w_up.shape),
                  _const_spec(conv_w.shape), _const_spec(conv_b.shape), _const_spec(w_down.shape),
                  _const_spec(gf.shape)],
        out_specs=pl.BlockSpec((tm, d), lambda i: (i, 0)),
        out_shape=jax.ShapeDtypeStruct((n, d), F32),
        scratch_shapes=[pltpu.VMEM((SUBLANES, d_ff), F32)],
        compiler_params=pltpu.CompilerParams(dimension_semantics=("arbitrary",), vmem_limit_bytes=VMEM_LIMIT),
        name="outproj_ffn",
    )(x2, out_a, out_b, w_out, g2, w_up, conv_w, conv_b, w_down, gf)


def _gate_weights(wa, wx, groups):
    nb, bd, _ = wa.shape
    per = nb // groups
    w = jnp.stack([wa, wx]).reshape(2, groups, per, bd, bd)
    dense = jnp.einsum("tjbio,bc->jbitco", w, jnp.eye(per, dtype=w.dtype))
    return dense.reshape(groups, per * bd, 2 * per * bd)


def kernel(x, norm1_g, w_in, lru_conv_w, lru_conv_b, lru_wa, lru_ba, lru_wx, lru_bx, lru_lambda, diff_lq1, diff_lk1, diff_lq2, diff_lk2, diff_subln_g, rel_bias, w_out, norm2_g, ffn_w_up, ffn_conv_w, ffn_conv_b, ffn_w_down, final_norm_g):
    bsz, seq, d = x.shape
    assert w_in.shape[0] == 1, "single-layer block"
    assert seq % ROW_TILE == 0 and ROW_TILE == 2 * LRU_TILE and ROW_TILE % MATMUL_ROWS == 0
    assert seq % QUERY_BLOCK == 0 and QUERY_BLOCK % CHUNK == 0 and KEY_BLOCK % QUERY_BLOCK == 0
    n = bsz * seq
    x2 = x.reshape(n, d)
    row = lambda p: p.reshape(1, -1)

    lru_w = lru_wa.shape[1] * lru_wa.shape[2]
    attn_w = (w_in.shape[2] - 2 * lru_w) // 3
    qk_dim = attn_w // DIFF_HEADS // 2
    tq = QUERY_BLOCK
    assert attn_w == lru_w, "the projection kernel slices five equal column groups"
    w_gates = _gate_weights(lru_wa[0], lru_wx[0], LRU_GATE_GROUPS).astype(BF16)
    b_gates = jnp.stack([lru_ba[0].reshape(lru_w), lru_bx[0].reshape(lru_w)])
    out_a, k, qt, vt = _inproj(x2, row(norm1_g[0]), w_in[0], lru_conv_w[0], row(lru_conv_b[0]),
                               w_gates, b_gates, row(lru_lambda[0]), bsz, seq, tm=ROW_TILE, lru_tile=LRU_TILE,
                               qk_scale=qk_dim ** -0.5 * LOG2E)

    bias = _bias_tiles(rel_bias, tq)
    out_b = _attn(qt, k.reshape(bsz, seq, attn_w), vt, bias, rel_bias,
                  row(diff_lq1[0]), row(diff_lk1[0]), row(diff_lq2[0]), row(diff_lk2[0]),
                  row(diff_subln_g[0]), tq)

    out = _ffn(x2, out_a, out_b.reshape(n, -1), w_out[0].astype(BF16), row(norm2_g[0]),
               ffn_w_up[0].astype(BF16), ffn_conv_w[0], row(ffn_conv_b[0]), ffn_w_down[0].astype(BF16),
               row(final_norm_g), seq, tm=ROW_TILE, n_chunks=FFN_CHUNKS)
    return out.reshape(bsz, seq, d)
```

```python
import functools
import math

import jax
import jax.numpy as jnp
import numpy as np
from jax import lax
from jax.experimental import pallas as pl
from jax.experimental.pallas import tpu as pltpu

CHUNK = 64
LRU_CONV = 4
LRU_C = 8.0
LRU_GATE_GROUPS = 2
DIFF_HEADS = 4
REL_BUCKETS = 32
REL_MAX_DIST = 128
FFN_CONV = 3
EPS = 1e-6
SUBLN_EPS = 1e-5
NEG_INF = -1e30
LAMBDA_INIT = 0.8 - 0.6 * math.exp(-0.3 * 0)

LOG2E = math.log2(math.e)

ROW_TILE = 512
LRU_TILE = 256
QUERY_BLOCK = 128
KEY_BLOCK = 256
LOOKAHEAD = 6
MATMUL_ROWS = 256
FFN_CHUNKS = 1

SUBLANES = 8
BF16_ROWS = 16
VMEM_LIMIT = 56 * 1024 * 1024

BF16 = jnp.bfloat16
F32 = jnp.float32


def _rms(x, g, eps):
    return x * lax.rsqrt(jnp.mean(x * x, axis=-1, keepdims=True) + eps) * g


def _gelu_tanh(x):
    inner = math.sqrt(2.0 / math.pi) * (x + 0.044715 * (x * x * x))
    return 0.5 * x * (1.0 + jnp.tanh(inner))


def _const_spec(shape):
    nd = len(shape)
    return pl.BlockSpec(shape, lambda *_: (0,) * nd, pipeline_mode=pl.Buffered(1))


def _shift_rows(ext, s):
    return pltpu.roll(ext, s, 0)[SUBLANES:]


def _segment_perm(tile):
    seg_len = tile // SUBLANES
    r = np.arange(tile)
    perm = np.zeros((tile, tile), np.float32)
    perm[r, (r % SUBLANES) * seg_len + r // SUBLANES] = 1.0
    return perm


def _lru_conv(lx, lg, p_ref, cw_ref, cb_ref, xtail_ref):
    tile, width = lx.shape
    npos = tile // SUBLANES
    ntap = LRU_CONV - 1
    sub = lax.broadcasted_iota(jnp.int32, (SUBLANES, width), 0)
    slab = lambda v, p: v[p * SUBLANES:(p + 1) * SUBLANES]
    x = jnp.dot(p_ref[...], lx, preferred_element_type=F32)
    gate = jnp.dot(p_ref[...], lg, preferred_element_type=F32)
    wrapped = []
    for q in range(ntap):
        cur = slab(x, npos - ntap + q)
        prev = slab(xtail_ref[...], q)
        wrapped.append(pltpu.roll(jnp.where(sub == SUBLANES - 1, prev, cur), 1, 0))
    xtail_ref[...] = x[tile - ntap * SUBLANES:]
    xc = cb_ref[...] + cw_ref[LRU_CONV - 1:LRU_CONV, :] * x
    for s in range(1, LRU_CONV):
        shifted = jnp.concatenate(wrapped[ntap - s:] + [x[:tile - SUBLANES * s]], axis=0)
        xc = xc + cw_ref[LRU_CONV - 1 - s:LRU_CONV - s, :] * shifted
    return xc, gate


def _lru_gates(xc, wg_ref, bg_ref):
    xcb = xc.astype(BF16)
    ngrp, gw, _ = wg_ref.shape
    pres = [jnp.dot(xcb[:, j * gw:(j + 1) * gw], wg_ref[j], preferred_element_type=F32) for j in range(ngrp)]
    pre_r = jnp.concatenate([pj[:, :gw] for pj in pres], axis=1) + bg_ref[0:1, :]
    pre_i = jnp.concatenate([pj[:, gw:] for pj in pres], axis=1) + bg_ref[1:2, :]
    return pre_r, pre_i


def _lru_scan(xc, gate, pre_r, pre_i, lam_ref, hend_ref):
    tile, width = xc.shape
    npos = tile // SUBLANES
    sub = lax.broadcasted_iota(jnp.int32, (SUBLANES, width), 0)
    slab = lambda v, p: v[p * SUBLANES:(p + 1) * SUBLANES]
    r = jax.nn.sigmoid(pre_r)
    gate_i = jax.nn.sigmoid(pre_i)
    z = -lam_ref[...]
    softplus = jnp.maximum(z, 0.0) + jnp.log(1.0 + jnp.exp(-jnp.abs(z)))
    a = jnp.exp((-LRU_C) * r * softplus)
    y = 1.0 - a * a
    u = jnp.where(y > 0.0, y * lax.rsqrt(y), 0.0) * (gate_i * xc)

    end, decay = slab(u, 0), slab(a, 0)
    for p in range(1, npos):
        ap = slab(a, p)
        end = ap * end + slab(u, p)
        decay = ap * decay
    inc = pltpu.roll(jnp.where(sub == SUBLANES - 1, hend_ref[...], end), 1, 0)
    dec = jnp.where(sub == 0, 0.0, pltpu.roll(decay, 1, 0))
    d = 1
    while d < SUBLANES:
        keep = sub >= d
        inc = jnp.where(keep, dec * pltpu.roll(inc, d, 0) + inc, inc)
        dec = jnp.where(keep, dec * pltpu.roll(dec, d, 0), dec)
        d *= 2
    state = inc
    hs = []
    for p in range(npos):
        state = slab(a, p) * state + slab(u, p)
        hs.append(state)
    hend_ref[...] = state
    return (_gelu_tanh(gate) * jnp.concatenate(hs, axis=0)).astype(BF16)


def _inproj_kernel(qk_scale, steps_per_seq, x_ref, g_ref, w_ref, p_ref, pt_ref, cw_ref, cb_ref, wg_ref,
                   bg_ref, lam_ref, oa_ref, k_ref, qt_ref, vt_ref, wb_ref, wt_ref, lx_ref, lg_ref, xtail_ref, hend_ref):
    i = pl.program_id(0)

    tm, width = oa_ref.shape

    @pl.when(i == 0)
    def _():
        lx_ref[...] = jnp.zeros_like(lx_ref)
        lg_ref[...] = jnp.zeros_like(lg_ref)
        for j, col in enumerate((0, 1, 3)):
            wb_ref[:, j * width:(j + 1) * width] = w_ref[:, col * width:(col + 1) * width].astype(wb_ref.dtype)
        for j, col in enumerate((2, 4)):
            wt_ref[j * width:(j + 1) * width, :] = w_ref[:, col * width:(col + 1) * width].T.astype(wt_ref.dtype)

    @pl.when((i + steps_per_seq - 1) % steps_per_seq == 0)
    def _():
        xtail_ref[...] = jnp.zeros_like(xtail_ref)
        hend_ref[...] = jnp.zeros_like(hend_ref)

    lx_prev = lx_ref[...]
    lg_prev = lg_ref[...]
    tile = p_ref.shape[0]
    assert tm == 2 * tile, "the interleaving below is written for two scan tiles per step"
    rows = (slice(0, tile), slice(tile, tm))
    nt = (((1,), (1,)), ((), ()))
    unperm = lambda out: jnp.dot(pt_ref[...], out, preferred_element_type=F32).astype(oa_ref.dtype)

    h = _rms(x_ref[...], g_ref[...], EPS).astype(BF16)
    xc0, gate0 = _lru_conv(lx_prev[rows[0]], lg_prev[rows[0]], p_ref, cw_ref, cb_ref, xtail_ref)
    lx_ref[...] = jnp.dot(h, wb_ref[:, :width], preferred_element_type=F32).astype(BF16)
    pre0 = _lru_gates(xc0, wg_ref, bg_ref)
    xc1, gate1 = _lru_conv(lx_prev[rows[1]], lg_prev[rows[1]], p_ref, cw_ref, cb_ref, xtail_ref)
    lg_ref[...] = jnp.dot(h, wb_ref[:, width:2 * width], preferred_element_type=F32).astype(BF16)
    pre1 = _lru_gates(xc1, wg_ref, bg_ref)
    out0 = _lru_scan(xc0, gate0, *pre0, lam_ref, hend_ref)
    k_ref[...] = jnp.dot(h, wb_ref[:, 2 * width:], preferred_element_type=F32).astype(k_ref.dtype)
    oa_ref[rows[0], :] = unperm(out0)
    out1 = _lru_scan(xc1, gate1, *pre1, lam_ref, hend_ref)
    qt = lax.dot_general(wt_ref[:width, :], h, nt, preferred_element_type=F32)
    qt_ref[...] = (qt * qk_scale).astype(qt_ref.dtype)
    oa_ref[rows[1], :] = unperm(out1)
    vt = lax.dot_general(wt_ref[width:, :], h, nt, preferred_element_type=F32)
    vt_ref[...] = vt.astype(vt_ref.dtype)


def _inproj(x2, g, w, conv_w, conv_b, w_gates, b_gates, lam, bsz, seq, tm, lru_tile, qk_scale):
    n, d = x2.shape
    width = w.shape[1] // 5
    spb = seq // tm
    last = n // tm - 1
    perm = _segment_perm(lru_tile)
    p_mat, pt_mat = jnp.asarray(perm, BF16), jnp.asarray(perm.T, BF16)
    consts = (g, w, p_mat, pt_mat, conv_w, conv_b, w_gates, b_gates, lam)
    proj = lambda i: jnp.minimum(i, last)
    col_spec = pl.BlockSpec((None, width, tm), lambda i: (proj(i) // spb, 0, proj(i) % spb))
    return pl.pallas_call(
        functools.partial(_inproj_kernel, qk_scale, spb),
        grid=(n // tm + 1,),
        in_specs=[pl.BlockSpec((tm, d), lambda i: (proj(i), 0))] + [_const_spec(c.shape) for c in consts],
        out_specs=[pl.BlockSpec((tm, width), lambda i: (jnp.maximum(i - 1, 0), 0)),
                   pl.BlockSpec((tm, width), lambda i: (proj(i), 0)), col_spec, col_spec],
        out_shape=[jax.ShapeDtypeStruct((n, width), BF16), jax.ShapeDtypeStruct((n, width), BF16),
                   jax.ShapeDtypeStruct((bsz, width, seq), BF16), jax.ShapeDtypeStruct((bsz, width, seq), BF16)],
        scratch_shapes=[pltpu.VMEM((d, 3 * width), BF16), pltpu.VMEM((2 * width, d), BF16),
                        pltpu.VMEM((tm, width), BF16), pltpu.VMEM((tm, width), BF16),
                        pltpu.VMEM(((LRU_CONV - 1) * SUBLANES, width), F32), pltpu.VMEM((SUBLANES, width), F32)],
        compiler_params=pltpu.CompilerParams(dimension_semantics=("arbitrary",), vmem_limit_bytes=VMEM_LIMIT),
        name="inproj_lru",
    )(x2, *consts)


def _rel_bucket_np(rel):
    half = REL_BUCKETS // 2
    max_exact = half // 2
    ret = (rel > 0).astype(np.int32) * half
    n = np.abs(rel)
    nf = np.maximum(n, 1).astype(np.float32)
    frac = np.log(nf / np.float32(max_exact)) / np.float32(math.log(REL_MAX_DIST / max_exact))
    large = max_exact + (frac * np.float32(half - max_exact)).astype(np.int32)
    large = np.minimum(large, half - 1)
    return ret + np.where(n < max_exact, n, large)


def _bucket_tile(tq):
    ql = np.arange(tq)[None, :]
    kpos = np.arange(2 * tq)[:, None] - tq
    bucket = _rel_bucket_np(kpos - ql).astype(np.int32)
    allowed = (kpos // CHUNK) <= (ql // CHUNK)
    return np.where(allowed, bucket, -1).astype(np.int32)


def _far_bucket(tq, seq):
    far = _rel_bucket_np(-np.arange(tq + 1, seq))
    assert (far == far[0]).all()
    return int(far[0])


def _bias_kernel(rb_ref, bucket_ref, o_ref):
    h = pl.program_id(0)
    bucket = bucket_ref[...]
    acc = jnp.zeros(bucket.shape, F32)
    for b in range(REL_BUCKETS):
        acc = jnp.where(bucket == b, rb_ref[b, h], acc)
    tile = jnp.where(bucket < 0, NEG_INF, acc) * LOG2E
    o_ref[...] = jnp.concatenate([tile, tile], axis=1)


def _bias_tiles(rel_bias, tq):
    bucket = jnp.asarray(_bucket_tile(tq))
    tk2 = bucket.shape[0]
    return pl.pallas_call(
        _bias_kernel,
        grid=(DIFF_HEADS,),
        in_specs=[pl.BlockSpec(memory_space=pltpu.SMEM), pl.BlockSpec((tk2, tq), lambda h: (0, 0))],
        out_specs=pl.BlockSpec((None, tk2, 2 * tq), lambda h: (h, 0, 0)),
        out_shape=jax.ShapeDtypeStruct((DIFF_HEADS, tk2, 2 * tq), F32),
        name="relbias",
    )(rel_bias, bucket)


def _attn_kernel(far_bucket, tq, rb_ref, qt_ref, k_ref, vt_ref, bias_ref, lq1_ref, lk1_ref, lq2_ref, lk2_ref,
                 sg_ref, o_ref):
    h = pl.program_id(1)
    hd, seq = qt_ref.shape
    qk = hd // 2
    far_bias = rb_ref[far_bucket, h] * LOG2E
    lam = (jnp.exp(jnp.sum(lq1_ref[...] * lk1_ref[...], axis=1, keepdims=True))
           - jnp.exp(jnp.sum(lq2_ref[...] * lk2_ref[...], axis=1, keepdims=True)) + LAMBDA_INIT)
    sub = lax.broadcasted_iota(jnp.int32, (hd, 2 * tq), 0)
    lane = lax.broadcasted_iota(jnp.int32, (hd, 2 * tq), 1)
    own_rows = (sub < qk) == (lane < tq)
    near_lo = lambda i: max(i - 1, 0) * tq
    ones = jnp.ones((BF16_ROWS, seq), BF16)
    values = lambda lo, hi: jnp.concatenate([vt_ref[:, lo:hi], ones[:, lo:hi]], axis=0)

    def logits(i):
        qt = qt_ref[:, i * tq:(i + 1) * tq]
        qt = jnp.concatenate([qt, qt], axis=1)
        qtc = jnp.where(own_rows, qt, jnp.zeros_like(qt))
        bias = bias_ref[...] if i > 0 else bias_ref[tq:, :]
        s_near = jnp.dot(k_ref[near_lo(i):(i + 1) * tq, :], qtc, preferred_element_type=F32) + bias
        s_far = jnp.dot(k_ref[:near_lo(i), :], qtc, preferred_element_type=F32) if near_lo(i) else None
        return s_near, s_far

    def softmax_pv(i, s_near, s_far):
        m = jnp.max(s_near, axis=0, keepdims=True)
        if s_far is not None:
            m = jnp.maximum(m, jnp.max(s_far, axis=0, keepdims=True) + far_bias)
        pv = None
        pieces = [(s_near, near_lo(i), m)] + ([(s_far, 0, m - far_bias)] if s_far is not None else [])
        for s, lo, shift in pieces:
            for r in range(0, s.shape[0], KEY_BLOCK):
                e = min(r + KEY_BLOCK, s.shape[0])
                p = jnp.exp2(s[r:e] - shift).astype(BF16)
                d = jnp.dot(values(lo + r, lo + e), p, preferred_element_type=F32)
                pv = d if pv is None else pv + d
        return pv[:hd] * (1.0 / pv[hd:hd + 1])

    nblk = seq // tq
    pending = [logits(i) for i in range(min(LOOKAHEAD, nblk))]
    for i in range(nblk):
        current = pending.pop(0)
        if i + LOOKAHEAD < nblk:
            pending.append(logits(i + LOOKAHEAD))
        maps = softmax_pv(i, *current)
        out = maps[:, :tq] - lam * maps[:, tq:]
        out = out * lax.rsqrt(jnp.mean(out * out, axis=0, keepdims=True) + SUBLN_EPS)
        o_ref[i * tq:(i + 1) * tq, :] = (out.T * sg_ref[...] * (1.0 - LAMBDA_INIT)).astype(o_ref.dtype)


def _attn(qt, k, vt, bias, rel_bias, lq1, lk1, lq2, lk2, subln_g, tq):
    bsz, seq, width = k.shape
    hd = width // DIFF_HEADS
    far_bucket = _far_bucket(tq, seq)
    col_spec = pl.BlockSpec((None, hd, seq), lambda b, h: (b, h, 0))
    row_spec = pl.BlockSpec((None, seq, hd), lambda b, h: (b, 0, h))
    return pl.pallas_call(
        functools.partial(_attn_kernel, far_bucket, tq),
        grid=(bsz, DIFF_HEADS),
        in_specs=[pl.BlockSpec(memory_space=pltpu.SMEM), col_spec, row_spec, col_spec,
                  pl.BlockSpec((None, 2 * tq, 2 * tq), lambda b, h: (h, 0, 0)),
                  _const_spec(lq1.shape), _const_spec(lk1.shape), _const_spec(lq2.shape),
                  _const_spec(lk2.shape), _const_spec(subln_g.shape)],
        out_specs=row_spec,
        out_shape=jax.ShapeDtypeStruct((bsz, seq, width), BF16),
        compiler_params=pltpu.CompilerParams(dimension_semantics=("arbitrary",) * 2,
                                             vmem_limit_bytes=VMEM_LIMIT),
        name="diffattn",
    )(rel_bias, qt, k, vt, bias, lq1, lk1, lq2, lk2, subln_g)


def _ffn_kernel(steps_per_seq, n_chunks, x_ref, a_ref, b_ref, wo32_ref, g2_ref, wu_ref, cw_ref, cb_ref,
                wd_ref, gf_ref, o_ref, wo_ref, tail_ref):
    tm = x_ref.shape[0]
    half = a_ref.shape[1]
    d_ff = wd_ref.shape[0]
    fc = d_ff // n_chunks

    @pl.when(pl.program_id(0) == 0)
    def _():
        wo_ref[...] = wo32_ref[...].astype(wo_ref.dtype)

    @pl.when(pl.program_id(0) % steps_per_seq == 0)
    def _():
        tail_ref[...] = jnp.zeros_like(tail_ref)

    def rows_dot(lhs, rhs):
        parts = [jnp.dot(lhs[r:r + MATMUL_ROWS], rhs, preferred_element_type=F32)
                 for r in range(0, tm, MATMUL_ROWS)]
        return jnp.concatenate(parts, axis=0)

    x1 = x_ref[...] + (rows_dot(a_ref[...], wo_ref[:half, :]) + rows_dot(b_ref[...], wo_ref[half:, :]))
    h2 = _rms(x1, g2_ref[...], EPS).astype(BF16)

    def up(c):
        gate = rows_dot(h2, wu_ref[:, c * fc:(c + 1) * fc])
        val = rows_dot(h2, wu_ref[:, d_ff + c * fc:d_ff + (c + 1) * fc])
        return gate, val

    def activate(c, gate, val):
        cols = slice(c * fc, (c + 1) * fc)
        ext = jnp.concatenate([tail_ref[:, cols], gate], axis=0)
        conv = cb_ref[:, cols] + cw_ref[FFN_CONV - 1:FFN_CONV, cols] * gate
        for s in range(1, FFN_CONV):
            conv = conv + cw_ref[FFN_CONV - 1 - s:FFN_CONV - s, cols] * _shift_rows(ext, s)
        tail_ref[:, cols] = gate[tm - SUBLANES:]
        return (_gelu_tanh(conv) * val).astype(BF16)

    ffn = None
    pending = up(0)
    for c in range(n_chunks):
        current = pending
        if c + 1 < n_chunks:
            pending = up(c + 1)
        act = activate(c, *current)
        down = rows_dot(act, wd_ref[c * fc:(c + 1) * fc, :])
        ffn = down if ffn is None else ffn + down
    o_ref[...] = _rms(x1 + ffn, gf_ref[...], EPS)


def _ffn(x2, out_a, out_b, w_out, g2, w_up, conv_w, conv_b, w_down, gf, seq, tm, n_chunks):
    n, d = x2.shape
    half = out_a.shape[1]
    d_ff = w_down.shape[0]
    return pl.pallas_call(
        functools.partial(_ffn_kernel, seq // tm, n_chunks),
        grid=(n // tm,),
        in_specs=[pl.BlockSpec((tm, d), lambda i: (i, 0)),
                  pl.BlockSpec((tm, half), lambda i: (i, 0)),
                  pl.BlockSpec((tm, half), lambda i: (i, 0)),
                  _const_spec(w_out.shape), _const_spec(g2.shape), _const_spec(w_up.shape),
                  _const_spec(conv_w.shape), _const_spec(conv_b.shape), _const_spec(w_down.shape),
                  _const_spec(gf.shape)],
        out_specs=pl.BlockSpec((tm, d), lambda i: (i, 0)),
        out_shape=jax.ShapeDtypeStruct((n, d), F32),
        scratch_shapes=[pltpu.VMEM(w_out.shape, BF16), pltpu.VMEM((SUBLANES, d_ff), F32)],
        compiler_params=pltpu.CompilerParams(dimension_semantics=("arbitrary",), vmem_limit_bytes=VMEM_LIMIT),
        name="outproj_ffn",
    )(x2, out_a, out_b, w_out, g2, w_up, conv_w, conv_b, w_down, gf)


def _gate_weights(wa, wx, groups):
    nb, bd, _ = wa.shape
    per = nb // groups
    w = jnp.stack([wa, wx]).reshape(2, groups, per, bd, bd)
    dense = jnp.einsum("tjbio,bc->jbitco", w, jnp.eye(per, dtype=w.dtype))
    return dense.reshape(groups, per * bd, 2 * per * bd)


def kernel(x, norm1_g, w_in, lru_conv_w, lru_conv_b, lru_wa, lru_ba, lru_wx, lru_bx, lru_lambda, diff_lq1, diff_lk1, diff_lq2, diff_lk2, diff_subln_g, rel_bias, w_out, norm2_g, ffn_w_up, ffn_conv_w, ffn_conv_b, ffn_w_down, final_norm_g):
    bsz, seq, d = x.shape
    assert w_in.shape[0] == 1, "single-layer block"
    assert seq % ROW_TILE == 0 and ROW_TILE == 2 * LRU_TILE and ROW_TILE % MATMUL_ROWS == 0
    assert seq % QUERY_BLOCK == 0 and QUERY_BLOCK % CHUNK == 0 and KEY_BLOCK % QUERY_BLOCK == 0
    n = bsz * seq
    x2 = x.reshape(n, d)
    row = lambda p: p.reshape(1, -1)

    lru_w = lru_wa.shape[1] * lru_wa.shape[2]
    attn_w = (w_in.shape[2] - 2 * lru_w) // 3
    qk_dim = attn_w // DIFF_HEADS // 2
    tq = QUERY_BLOCK
    assert attn_w == lru_w, "the projection kernel slices five equal column groups"
    w_gates = _gate_weights(lru_wa[0], lru_wx[0], LRU_GATE_GROUPS).astype(BF16)
    b_gates = jnp.stack([lru_ba[0].reshape(lru_w), lru_bx[0].reshape(lru_w)])
    out_a, k, qt, vt = _inproj(x2, row(norm1_g[0]), w_in[0], lru_conv_w[0], row(lru_conv_b[0]),
                               w_gates, b_gates, row(lru_lambda[0]), bsz, seq, tm=ROW_TILE, lru_tile=LRU_TILE,
                               qk_scale=qk_dim ** -0.5 * LOG2E)

    bias = _bias_tiles(rel_bias, tq)
    out_b = _attn(qt, k.reshape(bsz, seq, attn_w), vt, bias, rel_bias,
                  row(diff_lq1[0]), row(diff_lk1[0]), row(diff_lq2[0]), row(diff_lk2[0]),
                  row(diff_subln_g[0]), tq)

    out = _ffn(x2, out_a, out_b.reshape(n, -1), w_out[0], row(norm2_g[0]),
               ffn_w_up[0].astype(BF16), ffn_conv_w[0], row(ffn_conv_b[0]), ffn_w_down[0].astype(BF16),
               row(final_norm_g), seq, tm=ROW_TILE, n_chunks=FFN_CHUNKS)
    return out.reshape(bsz, seq, d)
```

```python
import functools
import math

import jax
import jax.numpy as jnp
import numpy as np
from jax import lax
from jax.experimental import pallas as pl
from jax.experimental.pallas import tpu as pltpu

CHUNK = 64
LRU_CONV = 4
LRU_C = 8.0
LRU_GATE_GROUPS = 2
DIFF_HEADS = 4
REL_BUCKETS = 32
REL_MAX_DIST = 128
FFN_CONV = 3
EPS = 1e-6
SUBLN_EPS = 1e-5
NEG_INF = -1e30
LAMBDA_INIT = 0.8 - 0.6 * math.exp(-0.3 * 0)

LOG2E = math.log2(math.e)

ROW_TILE = 512
LRU_TILE = 256
QUERY_BLOCK = 128
KEY_BLOCK = 256
LOOKAHEAD = 6
MATMUL_ROWS = 256
FFN_CHUNKS = 1

SUBLANES = 8
BF16_ROWS = 16
VMEM_LIMIT = 56 * 1024 * 1024

BF16 = jnp.bfloat16
F32 = jnp.float32


def _rms(x, g, eps):
    return x * lax.rsqrt(jnp.mean(x * x, axis=-1, keepdims=True) + eps) * g


def _gelu_tanh(x):
    inner = math.sqrt(2.0 / math.pi) * (x + 0.044715 * (x * x * x))
    return 0.5 * x * (1.0 + jnp.tanh(inner))


def _const_spec(shape):
    nd = len(shape)
    return pl.BlockSpec(shape, lambda *_: (0,) * nd, pipeline_mode=pl.Buffered(1))


def _shift_rows(ext, s):
    return pltpu.roll(ext, s, 0)[SUBLANES:]


def _segment_perm(tile):
    seg_len = tile // SUBLANES
    r = np.arange(tile)
    perm = np.zeros((tile, tile), np.float32)
    perm[r, (r % SUBLANES) * seg_len + r // SUBLANES] = 1.0
    return perm


def _lru_conv(lx, lg, p_ref, cw_ref, cb_ref, xtail_ref):
    tile, width = lx.shape
    npos = tile // SUBLANES
    ntap = LRU_CONV - 1
    sub = lax.broadcasted_iota(jnp.int32, (SUBLANES, width), 0)
    slab = lambda v, p: v[p * SUBLANES:(p + 1) * SUBLANES]
    x = jnp.dot(p_ref[...], lx, preferred_element_type=F32)
    gate = jnp.dot(p_ref[...], lg, preferred_element_type=F32)
    wrapped = []
    for q in range(ntap):
        cur = slab(x, npos - ntap + q)
        prev = slab(xtail_ref[...], q)
        wrapped.append(pltpu.roll(jnp.where(sub == SUBLANES - 1, prev, cur), 1, 0))
    xtail_ref[...] = x[tile - ntap * SUBLANES:]
    xc = cb_ref[...] + cw_ref[LRU_CONV - 1:LRU_CONV, :] * x
    for s in range(1, LRU_CONV):
        shifted = jnp.concatenate(wrapped[ntap - s:] + [x[:tile - SUBLANES * s]], axis=0)
        xc = xc + cw_ref[LRU_CONV - 1 - s:LRU_CONV - s, :] * shifted
    return xc, gate


def _lru_gates(xc, wg_ref, bg_ref):
    xcb = xc.astype(BF16)
    ngrp, gw, _ = wg_ref.shape
    pres = [jnp.dot(xcb[:, j * gw:(j + 1) * gw], wg_ref[j], preferred_element_type=F32) for j in range(ngrp)]
    pre_r = jnp.concatenate([pj[:, :gw] for pj in pres], axis=1) + bg_ref[0:1, :]
    pre_i = jnp.concatenate([pj[:, gw:] for pj in pres], axis=1) + bg_ref[1:2, :]
    return pre_r, pre_i


def _lru_scan(xc, gate, pre_r, pre_i, lam_ref, hend_ref):
    tile, width = xc.shape
    npos = tile // SUBLANES
    sub = lax.broadcasted_iota(jnp.int32, (SUBLANES, width), 0)
    slab = lambda v, p: v[p * SUBLANES:(p + 1) * SUBLANES]
    r = jax.nn.sigmoid(pre_r)
    gate_i = jax.nn.sigmoid(pre_i)
    z = -lam_ref[...]
    softplus = jnp.maximum(z, 0.0) + jnp.log(1.0 + jnp.exp(-jnp.abs(z)))
    a = jnp.exp((-LRU_C) * r * softplus)
    y = 1.0 - a * a
    u = jnp.where(y > 0.0, y * lax.rsqrt(y), 0.0) * (gate_i * xc)

    end, decay = slab(u, 0), slab(a, 0)
    for p in range(1, npos):
        ap = slab(a, p)
        end = ap * end + slab(u, p)
        decay = ap * decay
    inc = pltpu.roll(jnp.where(sub == SUBLANES - 1, hend_ref[...], end), 1, 0)
    dec = jnp.where(sub == 0, 0.0, pltpu.roll(decay, 1, 0))
    d = 1
    while d < SUBLANES:
        keep = sub >= d
        inc = jnp.where(keep, dec * pltpu.roll(inc, d, 0) + inc, inc)
        dec = jnp.where(keep, dec * pltpu.roll(dec, d, 0), dec)
        d *= 2
    state = inc
    hs = []
    for p in range(npos):
        state = slab(a, p) * state + slab(u, p)
        hs.append(state)
    hend_ref[...] = state
    return (_gelu_tanh(gate) * jnp.concatenate(hs, axis=0)).astype(BF16)


def _inproj_kernel(qk_scale, steps_per_seq, x_ref, g_ref, w_ref, p_ref, pt_ref, cw_ref, cb_ref, wg_ref,
                   bg_ref, lam_ref, wo32_ref, wu32_ref, wd32_ref, oa_ref, k_ref, qt_ref, vt_ref, wo_ref, wu_ref,
                   wd_ref, wb_ref, wt_ref, lx_ref, lg_ref, xtail_ref, hend_ref):
    i = pl.program_id(0)

    tm, width = oa_ref.shape
    for src, dst in ((wo32_ref, wo_ref), (wu32_ref, wu_ref), (wd32_ref, wd_ref)):
        dst[...] = src[...].astype(dst.dtype)

    @pl.when(i == 0)
    def _():
        lx_ref[...] = jnp.zeros_like(lx_ref)
        lg_ref[...] = jnp.zeros_like(lg_ref)
        for j, col in enumerate((0, 1, 3)):
            wb_ref[:, j * width:(j + 1) * width] = w_ref[:, col * width:(col + 1) * width].astype(wb_ref.dtype)
        for j, col in enumerate((2, 4)):
            wt_ref[j * width:(j + 1) * width, :] = w_ref[:, col * width:(col + 1) * width].T.astype(wt_ref.dtype)

    @pl.when((i + steps_per_seq - 1) % steps_per_seq == 0)
    def _():
        xtail_ref[...] = jnp.zeros_like(xtail_ref)
        hend_ref[...] = jnp.zeros_like(hend_ref)

    lx_prev = lx_ref[...]
    lg_prev = lg_ref[...]
    tile = p_ref.shape[0]
    assert tm == 2 * tile, "the interleaving below is written for two scan tiles per step"
    rows = (slice(0, tile), slice(tile, tm))
    nt = (((1,), (1,)), ((), ()))
    unperm = lambda out: jnp.dot(pt_ref[...], out, preferred_element_type=F32).astype(oa_ref.dtype)

    h = _rms(x_ref[...], g_ref[...], EPS).astype(BF16)
    xc0, gate0 = _lru_conv(lx_prev[rows[0]], lg_prev[rows[0]], p_ref, cw_ref, cb_ref, xtail_ref)
    lx_ref[...] = jnp.dot(h, wb_ref[:, :width], preferred_element_type=F32).astype(BF16)
    pre0 = _lru_gates(xc0, wg_ref, bg_ref)
    xc1, gate1 = _lru_conv(lx_prev[rows[1]], lg_prev[rows[1]], p_ref, cw_ref, cb_ref, xtail_ref)
    lg_ref[...] = jnp.dot(h, wb_ref[:, width:2 * width], preferred_element_type=F32).astype(BF16)
    pre1 = _lru_gates(xc1, wg_ref, bg_ref)
    out0 = _lru_scan(xc0, gate0, *pre0, lam_ref, hend_ref)
    k_ref[...] = jnp.dot(h, wb_ref[:, 2 * width:], preferred_element_type=F32).astype(k_ref.dtype)
    oa_ref[rows[0], :] = unperm(out0)
    out1 = _lru_scan(xc1, gate1, *pre1, lam_ref, hend_ref)
    qt = lax.dot_general(wt_ref[:width, :], h, nt, preferred_element_type=F32)
    qt_ref[...] = (qt * qk_scale).astype(qt_ref.dtype)
    oa_ref[rows[1], :] = unperm(out1)
    vt = lax.dot_general(wt_ref[width:, :], h, nt, preferred_element_type=F32)
    vt_ref[...] = vt.astype(vt_ref.dtype)


def _inproj(x2, g, w, conv_w, conv_b, w_gates, b_gates, lam, later_weights, bsz, seq, tm, lru_tile, qk_scale):
    n, d = x2.shape
    width = w.shape[1] // 5
    spb = seq // tm
    last = n // tm - 1
    perm = _segment_perm(lru_tile)
    p_mat, pt_mat = jnp.asarray(perm, BF16), jnp.asarray(perm.T, BF16)
    consts = (g, w, p_mat, pt_mat, conv_w, conv_b, w_gates, b_gates, lam)
    proj = lambda i: jnp.minimum(i, last)
    col_spec = pl.BlockSpec((None, width, tm), lambda i: (proj(i) // spb, 0, proj(i) % spb))
    assert all(m.shape[0] % ((last + 1) * BF16_ROWS) == 0 for m in later_weights)
    chunk_specs = [pl.BlockSpec((m.shape[0] // (last + 1), m.shape[1]), lambda i: (proj(i), 0))
                   for m in later_weights]
    return pl.pallas_call(
        functools.partial(_inproj_kernel, qk_scale, spb),
        grid=(n // tm + 1,),
        in_specs=[pl.BlockSpec((tm, d), lambda i: (proj(i), 0))] + [_const_spec(c.shape) for c in consts]
        + chunk_specs,
        out_specs=[pl.BlockSpec((tm, width), lambda i: (jnp.maximum(i - 1, 0), 0)),
                   pl.BlockSpec((tm, width), lambda i: (proj(i), 0)), col_spec, col_spec] + chunk_specs,
        out_shape=[jax.ShapeDtypeStruct((n, width), BF16), jax.ShapeDtypeStruct((n, width), BF16),
                   jax.ShapeDtypeStruct((bsz, width, seq), BF16), jax.ShapeDtypeStruct((bsz, width, seq), BF16)]
        + [jax.ShapeDtypeStruct(m.shape, BF16) for m in later_weights],
        scratch_shapes=[pltpu.VMEM((d, 3 * width), BF16), pltpu.VMEM((2 * width, d), BF16),
                        pltpu.VMEM((tm, width), BF16), pltpu.VMEM((tm, width), BF16),
                        pltpu.VMEM(((LRU_CONV - 1) * SUBLANES, width), F32), pltpu.VMEM((SUBLANES, width), F32)],
        compiler_params=pltpu.CompilerParams(dimension_semantics=("arbitrary",), vmem_limit_bytes=VMEM_LIMIT),
        name="inproj_lru",
    )(x2, *consts, *later_weights)


def _rel_bucket_np(rel):
    half = REL_BUCKETS // 2
    max_exact = half // 2
    ret = (rel > 0).astype(np.int32) * half
    n = np.abs(rel)
    nf = np.maximum(n, 1).astype(np.float32)
    frac = np.log(nf / np.float32(max_exact)) / np.float32(math.log(REL_MAX_DIST / max_exact))
    large = max_exact + (frac * np.float32(half - max_exact)).astype(np.int32)
    large = np.minimum(large, half - 1)
    return ret + np.where(n < max_exact, n, large)


def _bucket_tile(tq):
    ql = np.arange(tq)[None, :]
    kpos = np.arange(2 * tq)[:, None] - tq
    bucket = _rel_bucket_np(kpos - ql).astype(np.int32)
    allowed = (kpos // CHUNK) <= (ql // CHUNK)
    return np.where(allowed, bucket, -1).astype(np.int32)


def _far_bucket(tq, seq):
    far = _rel_bucket_np(-np.arange(tq + 1, seq))
    assert (far == far[0]).all()
    return int(far[0])


def _bias_kernel(rb_ref, bucket_ref, o_ref):
    h = pl.program_id(0)
    bucket = bucket_ref[...]
    acc = jnp.zeros(bucket.shape, F32)
    for b in range(REL_BUCKETS):
        acc = jnp.where(bucket == b, rb_ref[b, h], acc)
    tile = jnp.where(bucket < 0, NEG_INF, acc) * LOG2E
    o_ref[...] = jnp.concatenate([tile, tile], axis=1)


def _bias_tiles(rel_bias, tq):
    bucket = jnp.asarray(_bucket_tile(tq))
    tk2 = bucket.shape[0]
    return pl.pallas_call(
        _bias_kernel,
        grid=(DIFF_HEADS,),
        in_specs=[pl.BlockSpec(memory_space=pltpu.SMEM), pl.BlockSpec((tk2, tq), lambda h: (0, 0))],
        out_specs=pl.BlockSpec((None, tk2, 2 * tq), lambda h: (h, 0, 0)),
        out_shape=jax.ShapeDtypeStruct((DIFF_HEADS, tk2, 2 * tq), F32),
        name="relbias",
    )(rel_bias, bucket)


def _attn_kernel(far_bucket, tq, rb_ref, qt_ref, k_ref, vt_ref, bias_ref, lq1_ref, lk1_ref, lq2_ref, lk2_ref,
                 sg_ref, o_ref):
    h = pl.program_id(1)
    hd, seq = qt_ref.shape
    qk = hd // 2
    far_bias = rb_ref[far_bucket, h] * LOG2E
    lam = (jnp.exp(jnp.sum(lq1_ref[...] * lk1_ref[...], axis=1, keepdims=True))
           - jnp.exp(jnp.sum(lq2_ref[...] * lk2_ref[...], axis=1, keepdims=True)) + LAMBDA_INIT)
    sub = lax.broadcasted_iota(jnp.int32, (hd, 2 * tq), 0)
    lane = lax.broadcasted_iota(jnp.int32, (hd, 2 * tq), 1)
    own_rows = (sub < qk) == (lane < tq)
    near_lo = lambda i: max(i - 1, 0) * tq
    ones = jnp.ones((BF16_ROWS, seq), BF16)
    values = lambda lo, hi: jnp.concatenate([vt_ref[:, lo:hi], ones[:, lo:hi]], axis=0)

    def logits(i):
        qt = qt_ref[:, i * tq:(i + 1) * tq]
        qt = jnp.concatenate([qt, qt], axis=1)
        qtc = jnp.where(own_rows, qt, jnp.zeros_like(qt))
        bias = bias_ref[...] if i > 0 else bias_ref[tq:, :]
        s_near = jnp.dot(k_ref[near_lo(i):(i + 1) * tq, :], qtc, preferred_element_type=F32) + bias
        s_far = jnp.dot(k_ref[:near_lo(i), :], qtc, preferred_element_type=F32) if near_lo(i) else None
        return s_near, s_far

    def softmax_pv(i, s_near, s_far):
        m = jnp.max(s_near, axis=0, keepdims=True)
        if s_far is not None:
            m = jnp.maximum(m, jnp.max(s_far, axis=0, keepdims=True) + far_bias)
        pv = None
        pieces = [(s_near, near_lo(i), m)] + ([(s_far, 0, m - far_bias)] if s_far is not None else [])
        for s, lo, shift in pieces:
            for r in range(0, s.shape[0], KEY_BLOCK):
                e = min(r + KEY_BLOCK, s.shape[0])
                p = jnp.exp2(s[r:e] - shift).astype(BF16)
                d = jnp.dot(values(lo + r, lo + e), p, preferred_element_type=F32)
                pv = d if pv is None else pv + d
        return pv[:hd] * (1.0 / pv[hd:hd + 1])

    nblk = seq // tq
    pending = [logits(i) for i in range(min(LOOKAHEAD, nblk))]
    for i in range(nblk):
        current = pending.pop(0)
        if i + LOOKAHEAD < nblk:
            pending.append(logits(i + LOOKAHEAD))
        maps = softmax_pv(i, *current)
        out = maps[:, :tq] - lam * maps[:, tq:]
        out = out * lax.rsqrt(jnp.mean(out * out, axis=0, keepdims=True) + SUBLN_EPS)
        o_ref[i * tq:(i + 1) * tq, :] = (out.T * sg_ref[...] * (1.0 - LAMBDA_INIT)).astype(o_ref.dtype)


def _attn(qt, k, vt, bias, rel_bias, lq1, lk1, lq2, lk2, subln_g, tq):
    bsz, seq, width = k.shape
    hd = width // DIFF_HEADS
    far_bucket = _far_bucket(tq, seq)
    col_spec = pl.BlockSpec((None, hd, seq), lambda b, h: (b, h, 0))
    row_spec = pl.BlockSpec((None, seq, hd), lambda b, h: (b, 0, h))
    return pl.pallas_call(
        functools.partial(_attn_kernel, far_bucket, tq),
        grid=(bsz, DIFF_HEADS),
        in_specs=[pl.BlockSpec(memory_space=pltpu.SMEM), col_spec, row_spec, col_spec,
                  pl.BlockSpec((None, 2 * tq, 2 * tq), lambda b, h: (h, 0, 0)),
                  _const_spec(lq1.shape), _const_spec(lk1.shape), _const_spec(lq2.shape),
                  _const_spec(lk2.shape), _const_spec(subln_g.shape)],
        out_specs=row_spec,
        out_shape=jax.ShapeDtypeStruct((bsz, seq, width), BF16),
        compiler_params=pltpu.CompilerParams(dimension_semantics=("arbitrary",) * 2,
                                             vmem_limit_bytes=VMEM_LIMIT),
        name="diffattn",
    )(rel_bias, qt, k, vt, bias, lq1, lk1, lq2, lk2, subln_g)


def _ffn_kernel(steps_per_seq, n_chunks, x_ref, a_ref, b_ref, wo_ref, g2_ref, wu_ref, cw_ref, cb_ref,
                wd_ref, gf_ref, o_ref, tail_ref):
    tm = x_ref.shape[0]
    half = a_ref.shape[1]
    d_ff = wd_ref.shape[0]
    fc = d_ff // n_chunks

    @pl.when(pl.program_id(0) % steps_per_seq == 0)
    def _():
        tail_ref[...] = jnp.zeros_like(tail_ref)

    def rows_dot(lhs, rhs):
        parts = [jnp.dot(lhs[r:r + MATMUL_ROWS], rhs, preferred_element_type=F32)
                 for r in range(0, tm, MATMUL_ROWS)]
        return jnp.concatenate(parts, axis=0)

    x1 = x_ref[...] + (rows_dot(a_ref[...], wo_ref[:half, :]) + rows_dot(b_ref[...], wo_ref[half:, :]))
    h2 = _rms(x1, g2_ref[...], EPS).astype(BF16)

    def up(c):
        gate = rows_dot(h2, wu_ref[:, c * fc:(c + 1) * fc])
        val = rows_dot(h2, wu_ref[:, d_ff + c * fc:d_ff + (c + 1) * fc])
        return gate, val

    def activate(c, gate, val):
        cols = slice(c * fc, (c + 1) * fc)
        ext = jnp.concatenate([tail_ref[:, cols], gate], axis=0)
        conv = cb_ref[:, cols] + cw_ref[FFN_CONV - 1:FFN_CONV, cols] * gate
        for s in range(1, FFN_CONV):
            conv = conv + cw_ref[FFN_CONV - 1 - s:FFN_CONV - s, cols] * _shift_rows(ext, s)
        tail_ref[:, cols] = gate[tm - SUBLANES:]
        return (_gelu_tanh(conv) * val).astype(BF16)

    ffn = None
    pending = up(0)
    for c in range(n_chunks):
        current = pending
        if c + 1 < n_chunks:
            pending = up(c + 1)
        act = activate(c, *current)
        down = rows_dot(act, wd_ref[c * fc:(c + 1) * fc, :])
        ffn = down if ffn is None else ffn + down
    o_ref[...] = _rms(x1 + ffn, gf_ref[...], EPS)


def _ffn(x2, out_a, out_b, w_out, g2, w_up, conv_w, conv_b, w_down, gf, seq, tm, n_chunks):
    n, d = x2.shape
    half = out_a.shape[1]
    d_ff = w_down.shape[0]
    return pl.pallas_call(
        functools.partial(_ffn_kernel, seq // tm, n_chunks),
        grid=(n // tm,),
        in_specs=[pl.BlockSpec((tm, d), lambda i: (i, 0)),
                  pl.BlockSpec((tm, half), lambda i: (i, 0)),
                  pl.BlockSpec((tm, half), lambda i: (i, 0)),
                  _const_spec(w_out.shape), _const_spec(g2.shape), _const_spec(w_up.shape),
                  _const_spec(conv_w.shape), _const_spec(conv_b.shape), _const_spec(w_down.shape),
                  _const_spec(gf.shape)],
        out_specs=pl.BlockSpec((tm, d), lambda i: (i, 0)),
        out_shape=jax.ShapeDtypeStruct((n, d), F32),
        scratch_shapes=[pltpu.VMEM((SUBLANES, d_ff), F32)],
        compiler_params=pltpu.CompilerParams(dimension_semantics=("arbitrary",), vmem_limit_bytes=VMEM_LIMIT),
        name="outproj_ffn",
    )(x2, out_a, out_b, w_out, g2, w_up, conv_w, conv_b, w_down, gf)


def _gate_weights(wa, wx, groups):
    nb, bd, _ = wa.shape
    per = nb // groups
    w = jnp.stack([wa, wx]).reshape(2, groups, per, bd, bd)
    dense = jnp.einsum("tjbio,bc->jbitco", w, jnp.eye(per, dtype=w.dtype))
    return dense.reshape(groups, per * bd, 2 * per * bd)


def kernel(x, norm1_g, w_in, lru_conv_w, lru_conv_b, lru_wa, lru_ba, lru_wx, lru_bx, lru_lambda, diff_lq1, diff_lk1, diff_lq2, diff_lk2, diff_subln_g, rel_bias, w_out, norm2_g, ffn_w_up, ffn_conv_w, ffn_conv_b, ffn_w_down, final_norm_g):
    bsz, seq, d = x.shape
    assert w_in.shape[0] == 1, "single-layer block"
    assert seq % ROW_TILE == 0 and ROW_TILE == 2 * LRU_TILE and ROW_TILE % MATMUL_ROWS == 0
    assert seq % QUERY_BLOCK == 0 and QUERY_BLOCK % CHUNK == 0 and KEY_BLOCK % QUERY_BLOCK == 0
    n = bsz * seq
    x2 = x.reshape(n, d)
    row = lambda p: p.reshape(1, -1)

    lru_w = lru_wa.shape[1] * lru_wa.shape[2]
    attn_w = (w_in.shape[2] - 2 * lru_w) // 3
    qk_dim = attn_w // DIFF_HEADS // 2
    tq = QUERY_BLOCK
    assert attn_w == lru_w, "the projection kernel slices five equal column groups"
    w_gates = _gate_weights(lru_wa[0], lru_wx[0], LRU_GATE_GROUPS).astype(BF16)
    b_gates = jnp.stack([lru_ba[0].reshape(lru_w), lru_bx[0].reshape(lru_w)])
    out_a, k, qt, vt, wo, wu, wd = _inproj(
        x2, row(norm1_g[0]), w_in[0], lru_conv_w[0], row(lru_conv_b[0]), w_gates, b_gates, row(lru_lambda[0]),
        (w_out[0], ffn_w_up[0], ffn_w_down[0]), bsz, seq, tm=ROW_TILE, lru_tile=LRU_TILE,
        qk_scale=qk_dim ** -0.5 * LOG2E)

    bias = _bias_tiles(rel_bias, tq)
    out_b = _attn(qt, k.reshape(bsz, seq, attn_w), vt, bias, rel_bias,
                  row(diff_lq1[0]), row(diff_lk1[0]), row(diff_lq2[0]), row(diff_lk2[0]),
                  row(diff_subln_g[0]), tq)

    out = _ffn(x2, out_a, out_b.reshape(n, -1), wo, row(norm2_g[0]), wu, ffn_conv_w[0], row(ffn_conv_b[0]), wd,
               row(final_norm_g), seq, tm=ROW_TILE, n_chunks=FFN_CHUNKS)
    return out.reshape(bsz, seq, d)
```

```python
import functools
import math

import jax
import jax.numpy as jnp
import numpy as np
from jax import lax
from jax.experimental import pallas as pl
from jax.experimental.pallas import tpu as pltpu

CHUNK = 64
LRU_CONV = 4
LRU_C = 8.0
LRU_GATE_GROUPS = 2
DIFF_HEADS = 4
REL_BUCKETS = 32
REL_MAX_DIST = 128
FFN_CONV = 3
EPS = 1e-6
SUBLN_EPS = 1e-5
NEG_INF = -1e30
LAMBDA_INIT = 0.8 - 0.6 * math.exp(-0.3 * 0)

LOG2E = math.log2(math.e)

ROW_TILE = 512
LRU_TILE = 256
QUERY_BLOCK = 128
KEY_BLOCK = 256
LOOKAHEAD = 6
MATMUL_ROWS = 256
FFN_CHUNKS = 1

SUBLANES = 8
BF16_ROWS = 16
VMEM_LIMIT = 56 * 1024 * 1024

BF16 = jnp.bfloat16
F32 = jnp.float32


def _rms(x, g, eps):
    return x * lax.rsqrt(jnp.mean(x * x, axis=-1, keepdims=True) + eps) * g


def _gelu_tanh(x):
    inner = math.sqrt(2.0 / math.pi) * (x + 0.044715 * (x * x * x))
    return 0.5 * x * (1.0 + jnp.tanh(inner))


def _const_spec(shape):
    nd = len(shape)
    return pl.BlockSpec(shape, lambda *_: (0,) * nd, pipeline_mode=pl.Buffered(1))


def _shift_rows(ext, s):
    return pltpu.roll(ext, s, 0)[SUBLANES:]


def _segment_perm(tile):
    seg_len = tile // SUBLANES
    r = np.arange(tile)
    perm = np.zeros((tile, tile), np.float32)
    perm[r, (r % SUBLANES) * seg_len + r // SUBLANES] = 1.0
    return perm


def _lru_conv(lx, lg, p_ref, cw_ref, cb_ref, xtail_ref):
    tile, width = lx.shape
    npos = tile // SUBLANES
    ntap = LRU_CONV - 1
    sub = lax.broadcasted_iota(jnp.int32, (SUBLANES, width), 0)
    slab = lambda v, p: v[p * SUBLANES:(p + 1) * SUBLANES]
    x = jnp.dot(p_ref[...], lx, preferred_element_type=F32)
    gate = jnp.dot(p_ref[...], lg, preferred_element_type=F32)
    wrapped = []
    for q in range(ntap):
        cur = slab(x, npos - ntap + q)
        prev = slab(xtail_ref[...], q)
        wrapped.append(pltpu.roll(jnp.where(sub == SUBLANES - 1, prev, cur), 1, 0))
    xtail_ref[...] = x[tile - ntap * SUBLANES:]
    xc = cb_ref[...] + cw_ref[LRU_CONV - 1:LRU_CONV, :] * x
    for s in range(1, LRU_CONV):
        shifted = jnp.concatenate(wrapped[ntap - s:] + [x[:tile - SUBLANES * s]], axis=0)
        xc = xc + cw_ref[LRU_CONV - 1 - s:LRU_CONV - s, :] * shifted
    return xc, gate


def _lru_gates(xc, wg_ref, bg_ref):
    xcb = xc.astype(BF16)
    ngrp, gw, _ = wg_ref.shape
    pres = [jnp.dot(xcb[:, j * gw:(j + 1) * gw], wg_ref[j], preferred_element_type=F32) for j in range(ngrp)]
    pre_r = jnp.concatenate([pj[:, :gw] for pj in pres], axis=1) + bg_ref[0:1, :]
    pre_i = jnp.concatenate([pj[:, gw:] for pj in pres], axis=1) + bg_ref[1:2, :]
    return pre_r, pre_i


def _lru_scan(xc, gate, pre_r, pre_i, lam_ref, hend_ref):
    tile, width = xc.shape
    npos = tile // SUBLANES
    sub = lax.broadcasted_iota(jnp.int32, (SUBLANES, width), 0)
    slab = lambda v, p: v[p * SUBLANES:(p + 1) * SUBLANES]
    r = jax.nn.sigmoid(pre_r)
    gate_i = jax.nn.sigmoid(pre_i)
    z = -lam_ref[...]
    softplus = jnp.maximum(z, 0.0) + jnp.log(1.0 + jnp.exp(-jnp.abs(z)))
    a = jnp.exp((-LRU_C) * r * softplus)
    y = 1.0 - a * a
    u = jnp.where(y > 0.0, y * lax.rsqrt(y), 0.0) * (gate_i * xc)

    end, decay = slab(u, 0), slab(a, 0)
    for p in range(1, npos):
        ap = slab(a, p)
        end = ap * end + slab(u, p)
        decay = ap * decay
    inc = pltpu.roll(jnp.where(sub == SUBLANES - 1, hend_ref[...], end), 1, 0)
    dec = jnp.where(sub == 0, 0.0, pltpu.roll(decay, 1, 0))
    d = 1
    while d < SUBLANES:
        keep = sub >= d
        inc = jnp.where(keep, dec * pltpu.roll(inc, d, 0) + inc, inc)
        dec = jnp.where(keep, dec * pltpu.roll(dec, d, 0), dec)
        d *= 2
    state = inc
    hs = []
    for p in range(npos):
        state = slab(a, p) * state + slab(u, p)
        hs.append(state)
    hend_ref[...] = state
    return (_gelu_tanh(gate) * jnp.concatenate(hs, axis=0)).astype(BF16)


def _inproj_kernel(qk_scale, steps_per_seq, x_ref, g_ref, w_ref, p_ref, pt_ref, cw_ref, cb_ref, wg_ref,
                   bg_ref, lam_ref, wo32_ref, wu32_ref, wd32_ref, oa_ref, k_ref, qt_ref, vt_ref, wo_ref, wu_ref,
                   wd_ref, wb_ref, wt_ref, lx_ref, lg_ref, xtail_ref, hend_ref):
    i = pl.program_id(0)

    tm, width = oa_ref.shape

    @pl.when(i == 0)
    def _():
        lx_ref[...] = jnp.zeros_like(lx_ref)
        lg_ref[...] = jnp.zeros_like(lg_ref)
        for j, col in enumerate((0, 1, 3)):
            wb_ref[:, j * width:(j + 1) * width] = w_ref[:, col * width:(col + 1) * width].astype(wb_ref.dtype)
        for j, col in enumerate((2, 4)):
            wt_ref[j * width:(j + 1) * width, :] = w_ref[:, col * width:(col + 1) * width].T.astype(wt_ref.dtype)

    @pl.when((i + steps_per_seq - 1) % steps_per_seq == 0)
    def _():
        xtail_ref[...] = jnp.zeros_like(xtail_ref)
        hend_ref[...] = jnp.zeros_like(hend_ref)

    tile = p_ref.shape[0]
    assert tm == 2 * tile, "the interleaving below is written for two scan tiles per step"
    rows = (slice(0, tile), slice(tile, tm))
    nt = (((1,), (1,)), ((), ()))
    unperm = lambda out: jnp.dot(pt_ref[...], out, preferred_element_type=F32).astype(oa_ref.dtype)

    def step(project):
        lx_prev = lx_ref[...]
        lg_prev = lg_ref[...]
        if project:
            for src, dst in ((wo32_ref, wo_ref), (wu32_ref, wu_ref), (wd32_ref, wd_ref)):
                dst[...] = src[...].astype(dst.dtype)
            h = _rms(x_ref[...], g_ref[...], EPS).astype(BF16)
        xc0, gate0 = _lru_conv(lx_prev[rows[0]], lg_prev[rows[0]], p_ref, cw_ref, cb_ref, xtail_ref)
        if project:
            lx_ref[...] = jnp.dot(h, wb_ref[:, :width], preferred_element_type=F32).astype(BF16)
        pre0 = _lru_gates(xc0, wg_ref, bg_ref)
        xc1, gate1 = _lru_conv(lx_prev[rows[1]], lg_prev[rows[1]], p_ref, cw_ref, cb_ref, xtail_ref)
        if project:
            lg_ref[...] = jnp.dot(h, wb_ref[:, width:2 * width], preferred_element_type=F32).astype(BF16)
        pre1 = _lru_gates(xc1, wg_ref, bg_ref)
        out0 = _lru_scan(xc0, gate0, *pre0, lam_ref, hend_ref)
        if project:
            k_ref[...] = jnp.dot(h, wb_ref[:, 2 * width:], preferred_element_type=F32).astype(k_ref.dtype)
        oa_ref[rows[0], :] = unperm(out0)
        out1 = _lru_scan(xc1, gate1, *pre1, lam_ref, hend_ref)
        if project:
            qt = lax.dot_general(wt_ref[:width, :], h, nt, preferred_element_type=F32)
            qt_ref[...] = (qt * qk_scale).astype(qt_ref.dtype)
        oa_ref[rows[1], :] = unperm(out1)
        if project:
            vt = lax.dot_general(wt_ref[width:, :], h, nt, preferred_element_type=F32)
            vt_ref[...] = vt.astype(vt_ref.dtype)

    is_extra = i == pl.num_programs(0) - 1
    pl.when(jnp.logical_not(is_extra))(functools.partial(step, True))
    pl.when(is_extra)(functools.partial(step, False))


def _inproj(x2, g, w, conv_w, conv_b, w_gates, b_gates, lam, later_weights, bsz, seq, tm, lru_tile, qk_scale):
    n, d = x2.shape
    width = w.shape[1] // 5
    spb = seq // tm
    last = n // tm - 1
    perm = _segment_perm(lru_tile)
    p_mat, pt_mat = jnp.asarray(perm, BF16), jnp.asarray(perm.T, BF16)
    consts = (g, w, p_mat, pt_mat, conv_w, conv_b, w_gates, b_gates, lam)
    proj = lambda i: jnp.minimum(i, last)
    col_spec = pl.BlockSpec((None, width, tm), lambda i: (proj(i) // spb, 0, proj(i) % spb))
    assert all(m.shape[0] % ((last + 1) * BF16_ROWS) == 0 for m in later_weights)
    chunk_specs = [pl.BlockSpec((m.shape[0] // (last + 1), m.shape[1]), lambda i: (proj(i), 0))
                   for m in later_weights]
    return pl.pallas_call(
        functools.partial(_inproj_kernel, qk_scale, spb),
        grid=(n // tm + 1,),
        in_specs=[pl.BlockSpec((tm, d), lambda i: (proj(i), 0))] + [_const_spec(c.shape) for c in consts]
        + chunk_specs,
        out_specs=[pl.BlockSpec((tm, width), lambda i: (jnp.maximum(i - 1, 0), 0)),
                   pl.BlockSpec((tm, width), lambda i: (proj(i), 0)), col_spec, col_spec] + chunk_specs,
        out_shape=[jax.ShapeDtypeStruct((n, width), BF16), jax.ShapeDtypeStruct((n, width), BF16),
                   jax.ShapeDtypeStruct((bsz, width, seq), BF16), jax.ShapeDtypeStruct((bsz, width, seq), BF16)]
        + [jax.ShapeDtypeStruct(m.shape, BF16) for m in later_weights],
        scratch_shapes=[pltpu.VMEM((d, 3 * width), BF16), pltpu.VMEM((2 * width, d), BF16),
                        pltpu.VMEM((tm, width), BF16), pltpu.VMEM((tm, width), BF16),
                        pltpu.VMEM(((LRU_CONV - 1) * SUBLANES, width), F32), pltpu.VMEM((SUBLANES, width), F32)],
        compiler_params=pltpu.CompilerParams(dimension_semantics=("arbitrary",), vmem_limit_bytes=VMEM_LIMIT),
        name="inproj_lru",
    )(x2, *consts, *later_weights)


def _rel_bucket_np(rel):
    half = REL_BUCKETS // 2
    max_exact = half // 2
    ret = (rel > 0).astype(np.int32) * half
    n = np.abs(rel)
    nf = np.maximum(n, 1).astype(np.float32)
    frac = np.log(nf / np.float32(max_exact)) / np.float32(math.log(REL_MAX_DIST / max_exact))
    large = max_exact + (frac * np.float32(half - max_exact)).astype(np.int32)
    large = np.minimum(large, half - 1)
    return ret + np.where(n < max_exact, n, large)


def _bucket_tile(tq):
    ql = np.arange(tq)[None, :]
    kpos = np.arange(2 * tq)[:, None] - tq
    bucket = _rel_bucket_np(kpos - ql).astype(np.int32)
    allowed = (kpos // CHUNK) <= (ql // CHUNK)
    return np.where(allowed, bucket, -1).astype(np.int32)


def _far_bucket(tq, seq):
    far = _rel_bucket_np(-np.arange(tq + 1, seq))
    assert (far == far[0]).all()
    return int(far[0])


def _bias_kernel(rb_ref, bucket_ref, o_ref):
    h = pl.program_id(0)
    bucket = bucket_ref[...]
    acc = jnp.zeros(bucket.shape, F32)
    for b in range(REL_BUCKETS):
        acc = jnp.where(bucket == b, rb_ref[b, h], acc)
    tile = jnp.where(bucket < 0, NEG_INF, acc) * LOG2E
    o_ref[...] = jnp.concatenate([tile, tile], axis=1)


def _bias_tiles(rel_bias, tq):
    bucket = jnp.asarray(_bucket_tile(tq))
    tk2 = bucket.shape[0]
    return pl.pallas_call(
        _bias_kernel,
        grid=(DIFF_HEADS,),
        in_specs=[pl.BlockSpec(memory_space=pltpu.SMEM), pl.BlockSpec((tk2, tq), lambda h: (0, 0))],
        out_specs=pl.BlockSpec((None, tk2, 2 * tq), lambda h: (h, 0, 0)),
        out_shape=jax.ShapeDtypeStruct((DIFF_HEADS, tk2, 2 * tq), F32),
        name="relbias",
    )(rel_bias, bucket)


def _attn_kernel(far_bucket, tq, rb_ref, qt_ref, k_ref, vt_ref, bias_ref, lq1_ref, lk1_ref, lq2_ref, lk2_ref,
                 sg_ref, o_ref):
    h = pl.program_id(1)
    hd, seq = qt_ref.shape
    qk = hd // 2
    far_bias = rb_ref[far_bucket, h] * LOG2E
    lam = (jnp.exp(jnp.sum(lq1_ref[...] * lk1_ref[...], axis=1, keepdims=True))
           - jnp.exp(jnp.sum(lq2_ref[...] * lk2_ref[...], axis=1, keepdims=True)) + LAMBDA_INIT)
    sub = lax.broadcasted_iota(jnp.int32, (hd, 2 * tq), 0)
    lane = lax.broadcasted_iota(jnp.int32, (hd, 2 * tq), 1)
    own_rows = (sub < qk) == (lane < tq)
    near_lo = lambda i: max(i - 1, 0) * tq
    ones = jnp.ones((BF16_ROWS, seq), BF16)
    values = lambda lo, hi: jnp.concatenate([vt_ref[:, lo:hi], ones[:, lo:hi]], axis=0)

    def logits(i):
        qt = qt_ref[:, i * tq:(i + 1) * tq]
        qt = jnp.concatenate([qt, qt], axis=1)
        qtc = jnp.where(own_rows, qt, jnp.zeros_like(qt))
        bias = bias_ref[...] if i > 0 else bias_ref[tq:, :]
        s_near = jnp.dot(k_ref[near_lo(i):(i + 1) * tq, :], qtc, preferred_element_type=F32) + bias
        s_far = jnp.dot(k_ref[:near_lo(i), :], qtc, preferred_element_type=F32) if near_lo(i) else None
        return s_near, s_far

    def softmax_pv(i, s_near, s_far):
        m = jnp.max(s_near, axis=0, keepdims=True)
        if s_far is not None:
            m = jnp.maximum(m, jnp.max(s_far, axis=0, keepdims=True) + far_bias)
        pv = None
        pieces = [(s_near, near_lo(i), m)] + ([(s_far, 0, m - far_bias)] if s_far is not None else [])
        for s, lo, shift in pieces:
            for r in range(0, s.shape[0], KEY_BLOCK):
                e = min(r + KEY_BLOCK, s.shape[0])
                p = jnp.exp2(s[r:e] - shift).astype(BF16)
                d = jnp.dot(values(lo + r, lo + e), p, preferred_element_type=F32)
                pv = d if pv is None else pv + d
        return pv[:hd] * (1.0 / pv[hd:hd + 1])

    nblk = seq // tq
    pending = [logits(i) for i in range(min(LOOKAHEAD, nblk))]
    for i in range(nblk):
        current = pending.pop(0)
        if i + LOOKAHEAD < nblk:
            pending.append(logits(i + LOOKAHEAD))
        maps = softmax_pv(i, *current)
        out = maps[:, :tq] - lam * maps[:, tq:]
        out = out * lax.rsqrt(jnp.mean(out * out, axis=0, keepdims=True) + SUBLN_EPS)
        o_ref[i * tq:(i + 1) * tq, :] = (out.T * sg_ref[...] * (1.0 - LAMBDA_INIT)).astype(o_ref.dtype)


def _attn(qt, k, vt, bias, rel_bias, lq1, lk1, lq2, lk2, subln_g, tq):
    bsz, seq, width = k.shape
    hd = width // DIFF_HEADS
    far_bucket = _far_bucket(tq, seq)
    col_spec = pl.BlockSpec((None, hd, seq), lambda b, h: (b, h, 0))
    row_spec = pl.BlockSpec((None, seq, hd), lambda b, h: (b, 0, h))
    return pl.pallas_call(
        functools.partial(_attn_kernel, far_bucket, tq),
        grid=(bsz, DIFF_HEADS),
        in_specs=[pl.BlockSpec(memory_space=pltpu.SMEM), col_spec, row_spec, col_spec,
                  pl.BlockSpec((None, 2 * tq, 2 * tq), lambda b, h: (h, 0, 0)),
                  _const_spec(lq1.shape), _const_spec(lk1.shape), _const_spec(lq2.shape),
                  _const_spec(lk2.shape), _const_spec(subln_g.shape)],
        out_specs=row_spec,
        out_shape=jax.ShapeDtypeStruct((bsz, seq, width), BF16),
        compiler_params=pltpu.CompilerParams(dimension_semantics=("arbitrary",) * 2,
                                             vmem_limit_bytes=VMEM_LIMIT),
        name="diffattn",
    )(rel_bias, qt, k, vt, bias, lq1, lk1, lq2, lk2, subln_g)


def _ffn_kernel(steps_per_seq, n_chunks, x_ref, a_ref, b_ref, wo_ref, g2_ref, wu_ref, cw_ref, cb_ref,
                wd_ref, gf_ref, o_ref, tail_ref):
    tm = x_ref.shape[0]
    half = a_ref.shape[1]
    d_ff = wd_ref.shape[0]
    fc = d_ff // n_chunks

    @pl.when(pl.program_id(0) % steps_per_seq == 0)
    def _():
        tail_ref[...] = jnp.zeros_like(tail_ref)

    def rows_dot(lhs, rhs):
        parts = [jnp.dot(lhs[r:r + MATMUL_ROWS], rhs, preferred_element_type=F32)
                 for r in range(0, tm, MATMUL_ROWS)]
        return jnp.concatenate(parts, axis=0)

    x1 = x_ref[...] + (rows_dot(a_ref[...], wo_ref[:half, :]) + rows_dot(b_ref[...], wo_ref[half:, :]))
    h2 = _rms(x1, g2_ref[...], EPS).astype(BF16)

    def up(c):
        gate = rows_dot(h2, wu_ref[:, c * fc:(c + 1) * fc])
        val = rows_dot(h2, wu_ref[:, d_ff + c * fc:d_ff + (c + 1) * fc])
        return gate, val

    def activate(c, gate, val):
        cols = slice(c * fc, (c + 1) * fc)
        ext = jnp.concatenate([tail_ref[:, cols], gate], axis=0)
        conv = cb_ref[:, cols] + cw_ref[FFN_CONV - 1:FFN_CONV, cols] * gate
        for s in range(1, FFN_CONV):
            conv = conv + cw_ref[FFN_CONV - 1 - s:FFN_CONV - s, cols] * _shift_rows(ext, s)
        tail_ref[:, cols] = gate[tm - SUBLANES:]
        return (_gelu_tanh(conv) * val).astype(BF16)

    ffn = None
    pending = up(0)
    for c in range(n_chunks):
        current = pending
        if c + 1 < n_chunks:
            pending = up(c + 1)
        act = activate(c, *current)
        down = rows_dot(act, wd_ref[c * fc:(c + 1) * fc, :])
        ffn = down if ffn is None else ffn + down
    o_ref[...] = _rms(x1 + ffn, gf_ref[...], EPS)


def _ffn(x2, out_a, out_b, w_out, g2, w_up, conv_w, conv_b, w_down, gf, seq, tm, n_chunks):
    n, d = x2.shape
    half = out_a.shape[1]
    d_ff = w_down.shape[0]
    return pl.pallas_call(
        functools.partial(_ffn_kernel, seq // tm, n_chunks),
        grid=(n // tm,),
        in_specs=[pl.BlockSpec((tm, d), lambda i: (i, 0)),
                  pl.BlockSpec((tm, half), lambda i: (i, 0)),
                  pl.BlockSpec((tm, half), lambda i: (i, 0)),
                  _const_spec(w_out.shape), _const_spec(g2.shape), _const_spec(w_up.shape),
                  _const_spec(conv_w.shape), _const_spec(conv_b.shape), _const_spec(w_down.shape),
                  _const_spec(gf.shape)],
        out_specs=pl.BlockSpec((tm, d), lambda i: (i, 0)),
        out_shape=jax.ShapeDtypeStruct((n, d), F32),
        scratch_shapes=[pltpu.VMEM((SUBLANES, d_ff), F32)],
        compiler_params=pltpu.CompilerParams(dimension_semantics=("arbitrary",), vmem_limit_bytes=VMEM_LIMIT),
        name="outproj_ffn",
    )(x2, out_a, out_b, w_out, g2, w_up, conv_w, conv_b, w_down, gf)


def _gate_weights(wa, wx, groups):
    nb, bd, _ = wa.shape
    per = nb // groups
    w = jnp.stack([wa, wx]).reshape(2, groups, per, bd, bd)
    dense = jnp.einsum("tjbio,bc->jbitco", w, jnp.eye(per, dtype=w.dtype))
    return dense.reshape(groups, per * bd, 2 * per * bd)


def kernel(x, norm1_g, w_in, lru_conv_w, lru_conv_b, lru_wa, lru_ba, lru_wx, lru_bx, lru_lambda, diff_lq1, diff_lk1, diff_lq2, diff_lk2, diff_subln_g, rel_bias, w_out, norm2_g, ffn_w_up, ffn_conv_w, ffn_conv_b, ffn_w_down, final_norm_g):
    bsz, seq, d = x.shape
    assert w_in.shape[0] == 1, "single-layer block"
    assert seq % ROW_TILE == 0 and ROW_TILE == 2 * LRU_TILE and ROW_TILE % MATMUL_ROWS == 0
    assert seq % QUERY_BLOCK == 0 and QUERY_BLOCK % CHUNK == 0 and KEY_BLOCK % QUERY_BLOCK == 0
    n = bsz * seq
    x2 = x.reshape(n, d)
    row = lambda p: p.reshape(1, -1)

    lru_w = lru_wa.shape[1] * lru_wa.shape[2]
    attn_w = (w_in.shape[2] - 2 * lru_w) // 3
    qk_dim = attn_w // DIFF_HEADS // 2
    tq = QUERY_BLOCK
    assert attn_w == lru_w, "the projection kernel slices five equal column groups"
    w_gates = _gate_weights(lru_wa[0], lru_wx[0], LRU_GATE_GROUPS).astype(BF16)
    b_gates = jnp.stack([lru_ba[0].reshape(lru_w), lru_bx[0].reshape(lru_w)])
    out_a, k, qt, vt, wo, wu, wd = _inproj(
        x2, row(norm1_g[0]), w_in[0], lru_conv_w[0], row(lru_conv_b[0]), w_gates, b_gates, row(lru_lambda[0]),
        (w_out[0], ffn_w_up[0], ffn_w_down[0]), bsz, seq, tm=ROW_TILE, lru_tile=LRU_TILE,
        qk_scale=qk_dim ** -0.5 * LOG2E)

    bias = _bias_tiles(rel_bias, tq)
    out_b = _attn(qt, k.reshape(bsz, seq, attn_w), vt, bias, rel_bias,
                  row(diff_lq1[0]), row(diff_lk1[0]), row(diff_lq2[0]), row(diff_lk2[0]),
                  row(diff_subln_g[0]), tq)

    out = _ffn(x2, out_a, out_b.reshape(n, -1), wo, row(norm2_g[0]), wu, ffn_conv_w[0], row(ffn_conv_b[0]), wd,
               row(final_norm_g), seq, tm=ROW_TILE, n_chunks=FFN_CHUNKS)
    return out.reshape(bsz, seq, d)
```
